```python
import math
import jax, jax.numpy as jnp
from jax import lax
import numpy as np

D_MODEL = 1024
BATCH = 8
SEQ = 2048
DEPTH = 4
DEC_BATCH = 128
DEC_SEQ = 8
PAST_LEN = 8192
PAGE_SIZE = 128

N_HEADS = 8
N_KV_HEADS = 2
HEAD_DIM = 64
WINDOW = 128
ATTN_W = N_HEADS * HEAD_DIM
KV_W = N_KV_HEADS * HEAD_DIM
Q_PER_KV = N_HEADS // N_KV_HEADS
GLA_HEADS = 4
GLA_DK = 64
GLA_DV = 128
GLA_KW = GLA_HEADS * GLA_DK
GLA_VW = GLA_HEADS * GLA_DV
GLA_RANK = 16
GLA_TAU = 16.0
GLA_CHUNK = 64
POOL_WINDOWS = (2, 4, 8, 16)
POOL_GROUPS = 4
POOL_GW = 128
POOL_W = POOL_GROUPS * POOL_GW
POOL_STATE = max(POOL_WINDOWS) - 1
N_BRANCH = 3
BRANCH_W = 512
N_EXPERTS = 16
N_GROUPS = 4
EXPERTS_PER_GROUP = N_EXPERTS // N_GROUPS
TOP_K = 2
D_EXPERT = 512
DN_ALPHA = (2.0 * DEPTH) ** 0.25
DN_BETA = (8.0 * DEPTH) ** -0.25
LN_EPS = 1e-5
RMS_EPS = 1e-6
NEG_INF = -1e30

IN_SPLITS = (ATTN_W, KV_W, KV_W, GLA_KW, GLA_KW, GLA_VW, GLA_VW, GLA_RANK, POOL_W)
IN_W = sum(IN_SPLITS)
IN_OFFSETS = tuple(sum(IN_SPLITS[:i + 1]) for i in range(len(IN_SPLITS) - 1))

kernel_name = 'hybrid_swa_gla_pool_moe_decoder_step'


def _layer_norm(x, g, b):
    xf = x.astype(jnp.float32)
    xc = xf - jnp.mean(xf, -1, keepdims=True)
    var = jnp.mean(xc * xc, -1, keepdims=True)
    return (xc * lax.rsqrt(var + LN_EPS) * g.astype(jnp.float32) + b.astype(jnp.float32)).astype(x.dtype)


def _alibi_slopes():
    h = jnp.arange(1, N_HEADS + 1, dtype=jnp.float32)
    return jnp.exp2(-8.0 * h / N_HEADS)


def _swa_attend(q, k, v, q_pos, k_pos, sink):
    slopes = _alibi_slopes().reshape(N_KV_HEADS, Q_PER_KV)
    s = jnp.einsum('nqhgd,nkhd->nhgqk', q, k).astype(jnp.float32) * (HEAD_DIM ** -0.5)
    dist = q_pos[:, :, None] - k_pos[:, None, :]
    mask = (k_pos[:, None, :] >= 0) & (dist >= 0) & (dist < WINDOW)
    s = s - slopes[None, :, :, None, None] * dist[:, None, None].astype(jnp.float32)
    s = jnp.where(mask[:, None, None], s, NEG_INF)
    sk = jnp.broadcast_to(sink.astype(jnp.float32).reshape(1, N_KV_HEADS, Q_PER_KV, 1, 1), s.shape[:-1] + (1,))
    p = jax.nn.softmax(jnp.concatenate([s, sk], axis=-1), axis=-1)[..., :-1]
    return jnp.einsum('nhgqk,nkhd->nqhgd', p.astype(v.dtype), v)


def _attn_prompt(q, k, v, sink):
    B, T, _ = q.shape
    nb = T // WINDOW
    qb = q.reshape(B * nb, WINDOW, N_KV_HEADS, Q_PER_KV, HEAD_DIM)

    def band(a):
        ab = a.reshape(B, nb, WINDOW, N_KV_HEADS, HEAD_DIM)
        prev = jnp.concatenate([jnp.zeros_like(ab[:, :1]), ab[:, :-1]], axis=1)
        return jnp.concatenate([prev, ab], axis=2).reshape(B * nb, 2 * WINDOW, N_KV_HEADS, HEAD_DIM)

    start = jnp.arange(nb, dtype=jnp.int32)[:, None] * WINDOW
    q_pos = jnp.broadcast_to((start + jnp.arange(WINDOW, dtype=jnp.int32))[None], (B, nb, WINDOW)).reshape(B * nb, WINDOW)
    k_pos = jnp.broadcast_to((start - WINDOW + jnp.arange(2 * WINDOW, dtype=jnp.int32))[None], (B, nb, 2 * WINDOW)).reshape(B * nb, 2 * WINDOW)
    o = _swa_attend(qb, band(k), band(v), q_pos, k_pos, sink)
    return o.reshape(B, T, ATTN_W)


def _attn_sample(q, k, v, k_buf, v_buf, pos0, sink):
    B, T, _ = q.shape
    L = k_buf.shape[1]
    kk = jnp.concatenate([k_buf.astype(k.dtype), k], axis=1)
    vv = jnp.concatenate([v_buf.astype(v.dtype), v], axis=1)
    q_pos = jnp.broadcast_to(pos0 + jnp.arange(T, dtype=jnp.int32), (B, T))
    k_pos = jnp.broadcast_to(pos0 - L + jnp.arange(L + T, dtype=jnp.int32), (B, L + T))
    o = _swa_attend(q.reshape(B, T, N_KV_HEADS, Q_PER_KV, HEAD_DIM), kk, vv, q_pos, k_pos, sink)
    return o.reshape(B, T, ATTN_W), kk[:, T:], vv[:, T:]


def _gla_recurrence(q, k, v, log_a, s0):
    B, T = q.shape[:2]
    C = GLA_CHUNK if T % GLA_CHUNK == 0 else T
    n = T // C
    causal = jnp.tril(jnp.ones((C, C), dtype=bool))

    def chunks(a):
        return jnp.moveaxis(a.reshape((B, n, C) + a.shape[2:]), 1, 0)

    def step(S, inp):
        qc, kc, vc, lc = inp
        cum = jnp.cumsum(lc, axis=1)
        tot = cum[:, -1]
        qd = qc * jnp.exp(cum)
        kd = kc * jnp.exp(-cum)
        att = jnp.where(causal, jnp.einsum('bchk,bshk->bhcs', qd, kd), 0.0)
        o = jnp.einsum('bchk,bhkv->bchv', qd, S) + jnp.einsum('bhcs,bshv->bchv', att, vc)
        S = jnp.exp(tot)[..., None] * S + jnp.einsum('bshk,bshv->bhkv', kc * jnp.exp(tot[:, None] - cum), vc)
        return S, o

    S, o = lax.scan(step, s0, (chunks(q), chunks(k), chunks(v), chunks(log_a)))
    o = jnp.moveaxis(o, 0, 1).reshape(B, T, GLA_HEADS, GLA_DV)
    return o, S


def _gla_branch(qg, kg, vg, rg, ag, s0, w_a2, b_a, norm_g):
    B, T, _ = qg.shape
    f32 = jnp.float32
    q = qg.astype(f32).reshape(B, T, GLA_HEADS, GLA_DK) * (GLA_DK ** -0.5)
    k = kg.astype(f32).reshape(B, T, GLA_HEADS, GLA_DK)
    v = vg.astype(f32).reshape(B, T, GLA_HEADS, GLA_DV)
    log_a = jax.nn.log_sigmoid(ag.astype(f32) @ w_a2.astype(f32) + b_a.astype(f32)) / GLA_TAU
    log_a = log_a.reshape(B, T, GLA_HEADS, GLA_DK)
    o, S = _gla_recurrence(q, k, v, log_a, s0.astype(f32))
    o = o * lax.rsqrt(jnp.mean(o * o, -1, keepdims=True) + RMS_EPS)
    o = o.reshape(B, T, GLA_VW) * norm_g.astype(f32) * jax.nn.silu(rg.astype(f32))
    return o.astype(qg.dtype), S


def _pool_branch(u, prev, pos0, pool_w, pool_scale):
    B, T, _ = u.shape
    L = prev.shape[1]
    ext = jnp.concatenate([prev.astype(jnp.float32), u.astype(jnp.float32)], axis=1)
    cs = jnp.concatenate([jnp.zeros((B, 1, POOL_W), jnp.float32), jnp.cumsum(ext, axis=1)], axis=1)
    end = cs[:, L + 1:L + 1 + T]
    pos = pos0 + jnp.arange(T, dtype=jnp.int32)
    means = []
    for g, w in enumerate(POOL_WINDOWS):
        sl = slice(g * POOL_GW, (g + 1) * POOL_GW)
        start = cs[:, L + 1 - w:L + 1 - w + T, sl]
        cnt = jnp.minimum(pos + 1, w).astype(jnp.float32)[None, :, None]
        means.append((end[..., sl] - start) / cnt)
    d = (jnp.concatenate(means, axis=-1) - ext[:, L:]).reshape(B, T, POOL_GROUPS, POOL_GW)
    y = jnp.einsum('btgc,gcd->btgd', d, pool_w.astype(jnp.float32)).reshape(B, T, POOL_W)
    y = y * pool_scale.astype(jnp.float32)
    return y.astype(u.dtype), ext[:, T + L - L:][:, -L:].astype(u.dtype)


def _moe(x, w_router, router_bias, w_e_gate, w_e_up, w_e_down):
    B, T, D = x.shape
    xf = x.reshape(B * T, D)
    N = xf.shape[0]
    scores = jax.nn.sigmoid((xf @ w_router).astype(jnp.float32))
    sel = (scores + router_bias.astype(jnp.float32)).reshape(N, N_GROUPS, EXPERTS_PER_GROUP)
    group_score = jnp.sum(lax.top_k(sel, TOP_K)[0], axis=-1)
    gi = jnp.argmax(group_score, axis=-1)
    in_group = jnp.take_along_axis(sel, gi[:, None, None], axis=1)[:, 0]
    _, li = lax.top_k(in_group, TOP_K)
    eid = gi[:, None] * EXPERTS_PER_GROUP + li
    wsel = jnp.take_along_axis(scores, eid, axis=1)
    wsel = wsel / jnp.sum(wsel, axis=-1, keepdims=True)
    gate = jnp.einsum('nk,nke->en', wsel, jax.nn.one_hot(eid, N_EXPERTS, dtype=jnp.float32))

    def expert(acc, inp):
        wg, wu, wd, ge = inp
        h = jax.nn.silu(xf @ wg) * (xf @ wu)
        return acc + ge[:, None] * (h @ wd).astype(jnp.float32), None

    acc, _ = lax.scan(expert, jnp.zeros((N, D), jnp.float32), (w_e_gate, w_e_up, w_e_down, gate))
    return acc.astype(x.dtype).reshape(B, T, D)


def _layer(x, p, w_router, router_bias, k_buf, v_buf, gla_s0, pool_prev, pos0):
    B, T, _ = x.shape
    proj = x @ p['w_in']
    qa, ka, va, qg, kg, vg, rg, ag, up = jnp.split(proj, IN_OFFSETS, axis=-1)
    ka = ka.reshape(B, T, N_KV_HEADS, HEAD_DIM)
    va = va.reshape(B, T, N_KV_HEADS, HEAD_DIM)
    if k_buf is None:
        oa = _attn_prompt(qa, ka, va, p['attn_sink'])
        L = min(WINDOW, T)
        new_k, new_v = ka[:, T - L:], va[:, T - L:]
    else:
        oa, new_k, new_v = _attn_sample(qa, ka, va, k_buf, v_buf, pos0, p['attn_sink'])
    og, new_s = _gla_branch(qg, kg, vg, rg, ag, gla_s0, p['gla_w_a2'], p['gla_b_a'], p['gla_norm_g'])
    op, new_pool = _pool_branch(up, pool_prev, pos0, p['pool_w'], p['pool_scale'])
    gates = jax.nn.sigmoid((x @ p['w_gate']).astype(jnp.float32)).reshape(B, T, N_BRANCH, D_MODEL)
    branches = jnp.stack([oa, og, op], axis=2)
    proj_br = jnp.einsum('btnc,ncd->btnd', branches, p['w_branch']).astype(jnp.float32)
    merged = jnp.sum(gates * proj_br, axis=2).astype(x.dtype)
    mix = merged @ p['w_o']
    x = _layer_norm(DN_ALPHA * x + mix, p['ln1_g'], p['ln1_b'])
    ffn = _moe(x, w_router, router_bias, p['w_e_gate'], p['w_e_up'], p['w_e_down'])
    x = _layer_norm(DN_ALPHA * x + ffn, p['ln2_g'], p['ln2_b'])
    return x, new_k, new_v, new_s, new_pool


def setup_inputs(seed: int = 0) -> dict:
    key = jax.random.key(seed)
    ks = jax.random.split(key, 32)
    f32 = jnp.float32

    def nrm(k, shape, scale):
        return jax.random.normal(k, shape, f32) * scale

    wbuf = min(WINDOW, PAST_LEN)
    return {
        'x_prompt': nrm(ks[0], (BATCH, SEQ, D_MODEL), 1.0),
        'x_sample': nrm(ks[1], (DEC_BATCH, DEC_SEQ, D_MODEL), 1.0),
        'cache_attn_k': nrm(ks[2], (DEPTH, DEC_BATCH, wbuf, N_KV_HEADS, HEAD_DIM), 1.0),
        'cache_attn_v': nrm(ks[3], (DEPTH, DEC_BATCH, wbuf, N_KV_HEADS, HEAD_DIM), 1.0),
        'state_gla': nrm(ks[4], (DEPTH, DEC_BATCH, GLA_HEADS, GLA_DK, GLA_DV), 1.0),
        'state_pool': nrm(ks[5], (DEPTH, DEC_BATCH, POOL_STATE, POOL_W), 1.0),
        'w_in': nrm(ks[6], (DEPTH, D_MODEL, IN_W), D_MODEL ** -0.5),
        'w_gate': nrm(ks[7], (DEPTH, D_MODEL, N_BRANCH * D_MODEL), D_MODEL ** -0.5),
        'attn_sink': nrm(ks[8], (DEPTH, N_HEADS), 0.5),
        'gla_w_a2': nrm(ks[9], (DEPTH, GLA_RANK, GLA_KW), GLA_RANK ** -0.5),
        'gla_b_a': nrm(ks[10], (DEPTH, GLA_KW), 0.5),
        'gla_norm_g': 1.0 + nrm(ks[11], (DEPTH, GLA_VW), 0.1),
        'pool_w': nrm(ks[12], (DEPTH, POOL_GROUPS, POOL_GW, POOL_GW), POOL_GW ** -0.5),
        'pool_scale': 1.0 + nrm(ks[13], (DEPTH, POOL_W), 0.1),
        'w_branch': nrm(ks[14], (DEPTH, N_BRANCH, BRANCH_W, D_MODEL), BRANCH_W ** -0.5 * DN_BETA),
        'w_o': nrm(ks[15], (DEPTH, D_MODEL, D_MODEL), D_MODEL ** -0.5 * DN_BETA),
        'ln1_g': 1.0 + nrm(ks[16], (DEPTH, D_MODEL), 0.1),
        'ln1_b': nrm(ks[17], (DEPTH, D_MODEL), 0.02),
        'ln2_g': 1.0 + nrm(ks[18], (DEPTH, D_MODEL), 0.1),
        'ln2_b': nrm(ks[19], (DEPTH, D_MODEL), 0.02),
        'w_router': nrm(ks[20], (D_MODEL, N_EXPERTS), D_MODEL ** -0.5),
        'router_bias': nrm(ks[21], (N_EXPERTS,), 0.01),
        'w_e_gate': nrm(ks[22], (DEPTH, N_EXPERTS, D_MODEL, D_EXPERT), D_MODEL ** -0.5),
        'w_e_up': nrm(ks[23], (DEPTH, N_EXPERTS, D_MODEL, D_EXPERT), D_MODEL ** -0.5),
        'w_e_down': nrm(ks[24], (DEPTH, N_EXPERTS, D_EXPERT, D_MODEL), D_EXPERT ** -0.5 * DN_BETA),
    }


def reference(x_prompt, x_sample, cache_attn_k, cache_attn_v, state_gla, state_pool,
              w_in, w_gate, attn_sink, gla_w_a2, gla_b_a, gla_norm_g, pool_w, pool_scale,
              w_branch, w_o, ln1_g, ln1_b, ln2_g, ln2_b, w_router, router_bias,
              w_e_gate, w_e_up, w_e_down):
    yp, ys = x_prompt, x_sample
    bp = x_prompt.shape[0]
    pk, pv, ps, pp = [], [], [], []
    sk, sv, ss, sp = [], [], [], []
    for l in range(DEPTH):
        p = {'w_in': w_in[l], 'w_gate': w_gate[l], 'attn_sink': attn_sink[l],
             'gla_w_a2': gla_w_a2[l], 'gla_b_a': gla_b_a[l], 'gla_norm_g': gla_norm_g[l],
             'pool_w': pool_w[l], 'pool_scale': pool_scale[l], 'w_branch': w_branch[l], 'w_o': w_o[l],
             'ln1_g': ln1_g[l], 'ln1_b': ln1_b[l], 'ln2_g': ln2_g[l], 'ln2_b': ln2_b[l],
             'w_e_gate': w_e_gate[l], 'w_e_up': w_e_up[l], 'w_e_down': w_e_down[l]}
        gla0 = jnp.zeros((bp, GLA_HEADS, GLA_DK, GLA_DV), jnp.float32)
        pool0 = jnp.zeros((bp, POOL_STATE, POOL_W), x_prompt.dtype)
        yp, k_, v_, s_, q_ = _layer(yp, p, w_router, router_bias, None, None, gla0, pool0, 0)
        pk.append(k_); pv.append(v_); ps.append(s_); pp.append(q_)
        ys, k_, v_, s_, q_ = _layer(ys, p, w_router, router_bias, cache_attn_k[l], cache_attn_v[l],
                                    state_gla[l], state_pool[l], PAST_LEN)
        sk.append(k_); sv.append(v_); ss.append(s_); sp.append(q_)
    return (yp, ys, jnp.stack(pk), jnp.stack(pv), jnp.stack(ps), jnp.stack(pp),
            jnp.stack(sk), jnp.stack(sv), jnp.stack(ss), jnp.stack(sp))
```

```python
import functools

import jax
import jax.numpy as jnp
from jax import lax
from jax.experimental import pallas as pl
from jax.experimental.pallas import tpu as pltpu

F32, BF16, I32 = jnp.float32, jnp.bfloat16, jnp.int32

D_MODEL = 1024
BATCH = 8
SEQ = 2048
DEPTH = 4
DEC_BATCH = 128
DEC_SEQ = 8
N_HEADS = 8
N_KV_HEADS = 2
HEAD_DIM = 64
WINDOW = 128
ATTN_W = N_HEADS * HEAD_DIM
KV_W = N_KV_HEADS * HEAD_DIM
Q_PER_KV = N_HEADS // N_KV_HEADS
GLA_HEADS = 4
GLA_DK = 64
GLA_DV = 128
GLA_KW = GLA_HEADS * GLA_DK
GLA_VW = GLA_HEADS * GLA_DV
GLA_RANK = 16
GLA_TAU = 16.0
GLA_CHUNK = 64
POOL_WINDOWS = (2, 4, 8, 16)
POOL_GROUPS = 4
POOL_GW = 128
POOL_W = POOL_GROUPS * POOL_GW
POOL_STATE = max(POOL_WINDOWS) - 1
N_BRANCH = 3
BRANCH_W = 512
N_EXPERTS = 16
N_GROUPS = 4
EXPERTS_PER_GROUP = N_EXPERTS // N_GROUPS
D_EXPERT = 512
DN_ALPHA = (2.0 * DEPTH) ** 0.25
LN_EPS = 1e-5
RMS_EPS = 1e-6
NEG_INF = -1e30
ALIBI_SLOPES = tuple(2.0 ** (-8.0 * h / N_HEADS) for h in range(1, N_HEADS + 1))

LANES = 128
SUBLANES = 8

NP = BATCH * SEQ
NS = DEC_BATCH * DEC_SEQ
N_TOK = NP + NS
TM = 512
N_TILES = N_TOK // TM

PA_W = ATTN_W + 2 * KV_W
PG_W = 2 * GLA_KW + 2 * GLA_VW
IN_W_PAD = PA_W + PG_W + POOL_W + LANES

PAIRS = ((0, 1), (0, 2), (0, 3), (1, 2), (1, 3), (2, 3))
N_CLASSES = N_GROUPS * len(PAIRS)
TM_E = 256
NT_E = -(-(N_TOK + N_CLASSES * (TM_E - 1)) // TM_E)
N_SORTED = NT_E * TM_E
XE_W = D_MODEL + LANES
PLAN_ROWS = 256

VMEM_LIMIT = 56 * 1024 * 1024


def _params(*sem):
    return pltpu.CompilerParams(dimension_semantics=sem, vmem_limit_bytes=VMEM_LIMIT)


def _dot(a, b):
    return jnp.dot(a, b, preferred_element_type=F32)


def _dot_nt(a, b):
    return lax.dot_general(a, b, (((1,), (1,)), ((), ())), preferred_element_type=F32)


def _dot_tn(a, b):
    return lax.dot_general(a, b, (((0,), (0,)), ((), ())), preferred_element_type=F32)


def _layer_norm(h, g, b):
    mu = jnp.mean(h, axis=-1, keepdims=True)
    hc = h - mu
    var = jnp.mean(hc * hc, axis=-1, keepdims=True)
    return hc * lax.rsqrt(var + LN_EPS) * g + b


def _sigmoid(x):
    return 1.0 / (1.0 + jnp.exp(-x))


def _log_sigmoid(x):
    return jnp.minimum(x, 0.0) - jnp.log1p(jnp.exp(-jnp.abs(x)))


def _in_proj_kernel(x_ref, w_ref, pa_ref, pg_ref, pu_ref, pag_ref):
    xb = x_ref[...].astype(BF16)
    pa_ref[...] = _dot(xb, w_ref[:, 0:PA_W])
    pg_ref[...] = _dot(xb, w_ref[:, PA_W:PA_W + PG_W])
    pu_ref[...] = _dot(xb, w_ref[:, PA_W + PG_W:PA_W + PG_W + POOL_W])
    pag_ref[...] = _dot(xb, w_ref[:, PA_W + PG_W + POOL_W:IN_W_PAD])


def _in_proj(x, w_in_p, layer):
    row = lambda w: pl.BlockSpec((TM, w), lambda i: (i, 0))
    return pl.pallas_call(
        _in_proj_kernel,
        grid=(N_TILES,),
        in_specs=[row(D_MODEL), pl.BlockSpec((None, D_MODEL, IN_W_PAD), lambda i: (layer, 0, 0))],
        out_specs=[row(PA_W), row(PG_W), row(POOL_W), row(LANES)],
        out_shape=[jax.ShapeDtypeStruct((N_TOK, w), F32) for w in (PA_W, PG_W, POOL_W, LANES)],
        compiler_params=_params("arbitrary"),
        name="in_proj",
    )(x, w_in_p)


def _softmax_sink_pv(parts, sink):
    m = sink
    for s, _ in parts:
        m = jnp.maximum(m, jnp.max(s, axis=1, keepdims=True))
    den = jnp.exp(sink - m)
    es = []
    for s, _ in parts:
        e = jnp.exp(s - m)
        den = den + jnp.sum(e, axis=1, keepdims=True)
        es.append(e)
    out = None
    for e, (_, v) in zip(es, parts):
        o = _dot((e / den).astype(BF16), v)
        out = o if out is None else out + o
    return out


def _attn_prompt_kernel(sink_ref, cur_ref, prev_ref, o_ref):
    i = pl.program_id(1)
    k = jnp.concatenate([prev_ref[:, 0:KV_W], cur_ref[:, ATTN_W:ATTN_W + KV_W]], axis=0).astype(BF16)
    v = jnp.concatenate([prev_ref[:, KV_W:2 * KV_W], cur_ref[:, ATTN_W + KV_W:PA_W]], axis=0).astype(BF16)
    r = lax.broadcasted_iota(I32, (WINDOW, 2 * WINDOW), 0)
    c = lax.broadcasted_iota(I32, (WINDOW, 2 * WINDOW), 1)
    dist = (WINDOW + r - c).astype(F32)
    mask = (c > r) & (c <= WINDOW + r) & ((i > 0) | (c >= WINDOW))
    for h in range(N_HEADS):
        kv = h // Q_PER_KV
        qh = cur_ref[:, h * HEAD_DIM:(h + 1) * HEAD_DIM].astype(BF16)
        kh = k[:, kv * HEAD_DIM:(kv + 1) * HEAD_DIM]
        vh = v[:, kv * HEAD_DIM:(kv + 1) * HEAD_DIM]
        s = _dot_nt(qh, kh) * (HEAD_DIM ** -0.5) - ALIBI_SLOPES[h] * dist
        s = jnp.where(mask, s, NEG_INF)
        o = _softmax_sink_pv([(s, vh)], sink_ref[h])
        o_ref[:, h * HEAD_DIM:(h + 1) * HEAD_DIM] = o.astype(BF16)


def _attn_prompt(pa, sink):
    nb = SEQ // WINDOW
    return pl.pallas_call(
        _attn_prompt_kernel,
        grid_spec=pltpu.PrefetchScalarGridSpec(
            num_scalar_prefetch=1,
            grid=(BATCH, nb),
            in_specs=[
                pl.BlockSpec((WINDOW, PA_W), lambda b, i, s: (b * nb + i, 0)),
                pl.BlockSpec((WINDOW, 2 * KV_W), lambda b, i, s: (b * nb + jnp.maximum(i - 1, 0), ATTN_W // (2 * KV_W))),
            ],
            out_specs=pl.BlockSpec((WINDOW, ATTN_W), lambda b, i, s: (b * nb + i, 0)),
        ),
        out_shape=jax.ShapeDtypeStruct((N_TOK, ATTN_W), BF16),
        compiler_params=_params("arbitrary", "arbitrary"),
        name="attn_prompt",
    )(sink, pa, pa)


ATTN_SB = 8


def _attn_sample_kernel(sink_ref, cur_ref, kc_ref, vc_ref, oa_in_ref, o_ref, nk_ref, nv_ref):
    del oa_in_ref
    rows = Q_PER_KV * DEC_SEQ
    r = lax.broadcasted_iota(I32, (rows, WINDOW), 0)
    t = r % DEC_SEQ
    j = lax.broadcasted_iota(I32, (rows, WINDOW), 1)
    dist_c = (WINDOW + t - j).astype(F32)
    mask_c = j > t
    rn = lax.broadcasted_iota(I32, (rows, DEC_SEQ), 0)
    tn = rn % DEC_SEQ
    jn = lax.broadcasted_iota(I32, (rows, DEC_SEQ), 1)
    dist_n = (tn - jn).astype(F32)
    mask_n = jn <= tn
    gcol = lax.broadcasted_iota(I32, (rows, 1), 0) // DEC_SEQ
    for kv in range(N_KV_HEADS):
        slope = jnp.zeros((rows, 1), F32)
        sink = jnp.zeros((rows, 1), F32)
        for g in range(Q_PER_KV):
            slope = jnp.where(gcol == g, ALIBI_SLOPES[kv * Q_PER_KV + g], slope)
            sink = jnp.where(gcol == g, sink_ref[kv * Q_PER_KV + g], sink)
        ks = slice(kv * HEAD_DIM, (kv + 1) * HEAD_DIM)
        for s in range(ATTN_SB):
            rs = slice(s * DEC_SEQ, (s + 1) * DEC_SEQ)
            q = jnp.concatenate(
                [cur_ref[rs, (kv * Q_PER_KV + g) * HEAD_DIM:(kv * Q_PER_KV + g + 1) * HEAD_DIM] for g in range(Q_PER_KV)],
                axis=0).astype(BF16)
            kc = kc_ref[s, :, ks].astype(BF16)
            vc = vc_ref[s, :, ks].astype(BF16)
            kn = cur_ref[rs, ATTN_W + kv * HEAD_DIM:ATTN_W + (kv + 1) * HEAD_DIM].astype(BF16)
            vn = cur_ref[rs, ATTN_W + KV_W + kv * HEAD_DIM:ATTN_W + KV_W + (kv + 1) * HEAD_DIM].astype(BF16)
            sc = _dot_nt(q, kc) * (HEAD_DIM ** -0.5) - slope * dist_c
            sc = jnp.where(mask_c, sc, NEG_INF)
            sn = _dot_nt(q, kn) * (HEAD_DIM ** -0.5) - slope * dist_n
            sn = jnp.where(mask_n, sn, NEG_INF)
            o = _softmax_sink_pv([(sc, vc), (sn, vn)], sink)
            for g in range(Q_PER_KV):
                h = kv * Q_PER_KV + g
                o_ref[rs, h * HEAD_DIM:(h + 1) * HEAD_DIM] = o[g * DEC_SEQ:(g + 1) * DEC_SEQ].astype(BF16)
    keep = WINDOW - DEC_SEQ
    for s in range(ATTN_SB):
        rs = slice(s * DEC_SEQ, (s + 1) * DEC_SEQ)
        nk_ref[s, 0:keep, :] = kc_ref[s, DEC_SEQ:WINDOW, :]
        nk_ref[s, keep:WINDOW, :] = cur_ref[rs, ATTN_W:ATTN_W + KV_W]
        nv_ref[s, 0:keep, :] = vc_ref[s, DEC_SEQ:WINDOW, :]
        nv_ref[s, keep:WINDOW, :] = cur_ref[rs, ATTN_W + KV_W:PA_W]


def _attn_sample(pa, sink, cache_k, cache_v, oa, layer):
    rows = ATTN_SB * DEC_SEQ
    first = NP // rows
    cache = pl.BlockSpec((None, ATTN_SB, WINDOW, KV_W), lambda j, s: (layer, j, 0, 0))
    new = pl.BlockSpec((ATTN_SB, WINDOW, KV_W), lambda j, s: (j, 0, 0))
    return pl.pallas_call(
        _attn_sample_kernel,
        grid_spec=pltpu.PrefetchScalarGridSpec(
            num_scalar_prefetch=1,
            grid=(DEC_BATCH // ATTN_SB,),
            in_specs=[
                pl.BlockSpec((rows, PA_W), lambda j, s: (first + j, 0)),
                cache, cache,
                pl.BlockSpec(memory_space=pl.ANY),
            ],
            out_specs=[pl.BlockSpec((rows, ATTN_W), lambda j, s: (first + j, 0)), new, new],
        ),
        out_shape=[jax.ShapeDtypeStruct((N_TOK, ATTN_W), BF16),
                   jax.ShapeDtypeStruct((DEC_BATCH, WINDOW, KV_W), F32),
                   jax.ShapeDtypeStruct((DEC_BATCH, WINDOW, KV_W), F32)],
        input_output_aliases={4: 0},
        compiler_params=_params("arbitrary"),
        name="attn_sample",
    )(sink, pa, cache_k, cache_v, oa)


def _chunk_cumsum(x, chunk):
    pos = lax.broadcasted_iota(I32, x.shape, 0) % chunk
    sh = 1
    while sh < chunk:
        x = x + jnp.where(pos >= sh, pltpu.roll(x, sh, 0), 0.0)
        sh *= 2
    return x


def _gla_log_decay(pag_ref, wa2_ref, ba_ref):
    z = _dot(pag_ref[...].astype(BF16), wa2_ref[...]) + ba_ref[...]
    return _log_sigmoid(z) / GLA_TAU


def _gla_finish(o, rg, g):
    o = o * lax.rsqrt(jnp.mean(o * o, axis=-1, keepdims=True) + RMS_EPS)
    return (o * g * (rg * _sigmoid(rg))).astype(BF16)


GLA_TT = 512


def _gla_prompt_kernel(pg_ref, pag_ref, wa2_ref, ba_ref, g_ref, o_ref, s_ref, cum_ref, st_ref):
    j = pl.program_id(1)

    @pl.when(j == 0)
    def _():
        st_ref[...] = jnp.zeros_like(st_ref)

    cum_ref[...] = _chunk_cumsum(_gla_log_decay(pag_ref, wa2_ref, ba_ref), GLA_CHUNK)
    tri = (lax.broadcasted_iota(I32, (GLA_CHUNK, GLA_CHUNK), 0)
           >= lax.broadcasted_iota(I32, (GLA_CHUNK, GLA_CHUNK), 1))

    def chunk(c, carry):
        rows = pl.ds(pl.multiple_of(c * GLA_CHUNK, GLA_CHUNK), GLA_CHUNK)
        cum = cum_ref[rows, :]
        tot = cum[GLA_CHUNK - 1:GLA_CHUNK, :]
        q = pg_ref[rows, 0:GLA_KW] * (GLA_DK ** -0.5)
        k = pg_ref[rows, GLA_KW:2 * GLA_KW]
        qd = (q * jnp.exp(cum)).astype(BF16)
        kd = (k * jnp.exp(-cum)).astype(BF16)
        kdec = (k * jnp.exp(tot - cum)).astype(BF16)
        etot = jnp.exp(tot)
        for h in range(GLA_HEADS):
            ks = slice(h * GLA_DK, (h + 1) * GLA_DK)
            vs = slice(2 * GLA_KW + h * GLA_DV, 2 * GLA_KW + (h + 1) * GLA_DV)
            rs = slice(2 * GLA_KW + GLA_VW + h * GLA_DV, 2 * GLA_KW + GLA_VW + (h + 1) * GLA_DV)
            v = pg_ref[rows, vs].astype(BF16)
            st = st_ref[:, ks]
            att = jnp.where(tri, _dot_nt(qd[:, ks], kd[:, ks]), 0.0)
            o = _dot_nt(qd[:, ks], st.astype(BF16)) + _dot(att.astype(BF16), v)
            st_ref[:, ks] = etot[:, ks] * st + _dot_tn(v, kdec[:, ks])
            o_ref[rows, h * GLA_DV:(h + 1) * GLA_DV] = _gla_finish(o, pg_ref[rows, rs], g_ref[:, h * GLA_DV:(h + 1) * GLA_DV])
        return carry

    lax.fori_loop(0, GLA_TT // GLA_CHUNK, chunk, 0)

    @pl.when(j == pl.num_programs(1) - 1)
    def _():
        for h in range(GLA_HEADS):
            s_ref[h] = st_ref[:, h * GLA_DK:(h + 1) * GLA_DK].T


def _gla_prompt(pg, pag, wa2, ba, g):
    nt = SEQ // GLA_TT
    const = lambda shape: pl.BlockSpec(shape, lambda b, j: (0,) * len(shape))
    return pl.pallas_call(
        _gla_prompt_kernel,
        grid=(BATCH, nt),
        in_specs=[
            pl.BlockSpec((GLA_TT, PG_W), lambda b, j: (b * nt + j, 0)),
            pl.BlockSpec((GLA_TT, LANES), lambda b, j: (b * nt + j, 0)),
            const((LANES, GLA_KW)), const((1, GLA_KW)), const((1, GLA_VW)),
        ],
        out_specs=[
            pl.BlockSpec((GLA_TT, GLA_VW), lambda b, j: (b * nt + j, 0)),
            pl.BlockSpec((None, GLA_HEADS, GLA_DK, GLA_DV), lambda b, j: (b, 0, 0, 0)),
        ],
        out_shape=[jax.ShapeDtypeStruct((N_TOK, GLA_VW), BF16),
                   jax.ShapeDtypeStruct((BATCH, GLA_HEADS, GLA_DK, GLA_DV), F32)],
        scratch_shapes=[pltpu.VMEM((GLA_TT, GLA_KW), F32), pltpu.VMEM((GLA_DV, GLA_KW), F32)],
        compiler_params=_params("arbitrary", "arbitrary"),
        name="gla_prompt",
    )(pg, pag, wa2, ba, g)


GLA_SB = 8


def _gla_sample_kernel(pg_ref, pag_ref, wa2_ref, ba_ref, g_ref, s0_ref, og_in_ref, o_ref, s_ref):
    del og_in_ref
    cum_all = _chunk_cumsum(_gla_log_decay(pag_ref, wa2_ref, ba_ref), DEC_SEQ)
    tri = (lax.broadcasted_iota(I32, (DEC_SEQ, DEC_SEQ), 0) >= lax.broadcasted_iota(I32, (DEC_SEQ, DEC_SEQ), 1))
    for s in range(GLA_SB):
        rows = slice(s * DEC_SEQ, (s + 1) * DEC_SEQ)
        cum = cum_all[rows, :]
        tot = cum[DEC_SEQ - 1:DEC_SEQ, :]
        q = pg_ref[rows, 0:GLA_KW] * (GLA_DK ** -0.5)
        k = pg_ref[rows, GLA_KW:2 * GLA_KW]
        qd = (q * jnp.exp(cum)).astype(BF16)
        kd = (k * jnp.exp(-cum)).astype(BF16)
        kdec = (k * jnp.exp(tot - cum)).astype(BF16)
        etot = jnp.exp(tot)
        etot_col = [jnp.broadcast_to(etot[:, p * LANES:(p + 1) * LANES], (SUBLANES, LANES)).T[:, 0:1]
                    for p in range(GLA_KW // LANES)]
        for h in range(GLA_HEADS):
            ks = slice(h * GLA_DK, (h + 1) * GLA_DK)
            vs = slice(2 * GLA_KW + h * GLA_DV, 2 * GLA_KW + (h + 1) * GLA_DV)
            rs = slice(2 * GLA_KW + GLA_VW + h * GLA_DV, 2 * GLA_KW + GLA_VW + (h + 1) * GLA_DV)
            v = pg_ref[rows, vs].astype(BF16)
            st = s0_ref[s, h]
            att = jnp.where(tri, _dot_nt(qd[:, ks], kd[:, ks]), 0.0)
            o = _dot(qd[:, ks], st.astype(BF16)) + _dot(att.astype(BF16), v)
            per = LANES // GLA_DK
            col = etot_col[h // per][(h % per) * GLA_DK:(h % per + 1) * GLA_DK, :]
            s_ref[s, h] = col * st + _dot_tn(kdec[:, ks], v)
            o_ref[rows, h * GLA_DV:(h + 1) * GLA_DV] = _gla_finish(o, pg_ref[rows, rs], g_ref[:, h * GLA_DV:(h + 1) * GLA_DV])


def _gla_sample(pg, pag, wa2, ba, g, state, og, layer):
    rows = GLA_SB * DEC_SEQ
    first = NP // rows
    const = lambda shape: pl.BlockSpec(shape, lambda j: (0,) * len(shape))
    return pl.pallas_call(
        _gla_sample_kernel,
        grid=(DEC_BATCH // GLA_SB,),
        in_specs=[
            pl.BlockSpec((rows, PG_W), lambda j: (first + j, 0)),
            pl.BlockSpec((rows, LANES), lambda j: (first + j, 0)),
            const((LANES, GLA_KW)), const((1, GLA_KW)), const((1, GLA_VW)),
            pl.BlockSpec((None, GLA_SB, GLA_HEADS, GLA_DK, GLA_DV), lambda j: (layer, j, 0, 0, 0)),
            pl.BlockSpec(memory_space=pl.ANY),
        ],
        out_specs=[
            pl.BlockSpec((rows, GLA_VW), lambda j: (first + j, 0)),
            pl.BlockSpec((GLA_SB, GLA_HEADS, GLA_DK, GLA_DV), lambda j: (j, 0, 0, 0)),
        ],
        out_shape=[jax.ShapeDtypeStruct((N_TOK, GLA_VW), BF16),
                   jax.ShapeDtypeStruct((DEC_BATCH, GLA_HEADS, GLA_DK, GLA_DV), F32)],
        input_output_aliases={6: 0},
        compiler_params=_params("arbitrary"),
        name="gla_sample",
    )(pg, pag, wa2, ba, g, state, og)


POOL_HIST = 16


def _pool_groups(ext, cnt, pw_ref, ps_ref, out_rows):
    ax = ext.ndim - 2
    outs = []
    for g, w in enumerate(POOL_WINDOWS):
        x = ext[..., g * POOL_GW:(g + 1) * POOL_GW]
        s, sh = x, 1
        while sh < w:
            s = s + pltpu.roll(s, sh, ax)
            sh *= 2
        if ext.ndim == 3:
            d = (s[:, POOL_HIST:, :] / cnt[g] - x[:, POOL_HIST:, :]).reshape(out_rows, POOL_GW)
        else:
            d = s[POOL_HIST:, :] / cnt[g] - x[POOL_HIST:, :]
        y = _dot(d.astype(BF16), pw_ref[g]) * ps_ref[:, g * POOL_GW:(g + 1) * POOL_GW]
        outs.append(y.astype(BF16))
    return outs


POOL_TT = 512


def _pool_prompt_kernel(pu_ref, pw_ref, ps_ref, o_ref, hist_ref):
    j = pl.program_id(1)

    @pl.when(j == 0)
    def _():
        hist_ref[...] = jnp.zeros_like(hist_ref)

    u = pu_ref[...]
    ext = jnp.concatenate([hist_ref[...], u], axis=0)
    pos = j * POOL_TT + lax.broadcasted_iota(I32, (POOL_TT, 1), 0)
    cnt = [jnp.minimum(pos + 1, w).astype(F32) for w in POOL_WINDOWS]
    for g, y in enumerate(_pool_groups(ext, cnt, pw_ref, ps_ref, POOL_TT)):
        o_ref[:, g * POOL_GW:(g + 1) * POOL_GW] = y
    hist_ref[...] = u[POOL_TT - POOL_HIST:, :]


def _pool_prompt(pu, pw, ps):
    nt = SEQ // POOL_TT
    return pl.pallas_call(
        _pool_prompt_kernel,
        grid=(BATCH, nt),
        in_specs=[
            pl.BlockSpec((POOL_TT, POOL_W), lambda b, j: (b * nt + j, 0)),
            pl.BlockSpec((POOL_GROUPS, POOL_GW, POOL_GW), lambda b, j: (0, 0, 0)),
            pl.BlockSpec((1, POOL_W), lambda b, j: (0, 0)),
        ],
        out_specs=pl.BlockSpec((POOL_TT, POOL_W), lambda b, j: (b * nt + j, 0)),
        out_shape=jax.ShapeDtypeStruct((N_TOK, POOL_W), BF16),
        scratch_shapes=[pltpu.VMEM((POOL_HIST, POOL_W), F32)],
        compiler_params=_params("arbitrary", "arbitrary"),
        name="pool_prompt",
    )(pu, pw, ps)


POOL_SB = 16


def _pool_sample_kernel(ext_ref, pw_ref, ps_ref, op_in_ref, o_ref):
    del op_in_ref
    cnt = [float(w) for w in POOL_WINDOWS]
    for g, y in enumerate(_pool_groups(ext_ref[...], cnt, pw_ref, ps_ref, POOL_SB * DEC_SEQ)):
        o_ref[:, g * POOL_GW:(g + 1) * POOL_GW] = y


def _pool_sample(ext, pw, ps, op):
    rows = POOL_SB * DEC_SEQ
    first = NP // rows
    return pl.pallas_call(
        _pool_sample_kernel,
        grid=(DEC_BATCH // POOL_SB,),
        in_specs=[
            pl.BlockSpec((POOL_SB, POOL_HIST + DEC_SEQ, POOL_W), lambda j: (j, 0, 0)),
            pl.BlockSpec((POOL_GROUPS, POOL_GW, POOL_GW), lambda j: (0, 0, 0)),
            pl.BlockSpec((1, POOL_W), lambda j: (0, 0)),
            pl.BlockSpec(memory_space=pl.ANY),
        ],
        out_specs=pl.BlockSpec((rows, POOL_W), lambda j: (first + j, 0)),
        out_shape=jax.ShapeDtypeStruct((N_TOK, POOL_W), BF16),
        input_output_aliases={3: 0},
        compiler_params=_params("arbitrary"),
        name="pool_sample",
    )(ext, pw, ps, op)


def _route(sc, sel):
    gscore = []
    for g in range(N_GROUPS):
        v = sel[EXPERTS_PER_GROUP * g:EXPERTS_PER_GROUP * (g + 1)]
        best = None
        for a, b in PAIRS:
            pair = v[a] + v[b]
            best = pair if best is None else jnp.maximum(best, pair)
        gscore.append(best)
    gi = jnp.zeros_like(gscore[0], dtype=I32)
    best = gscore[0]
    for g in range(1, N_GROUPS):
        upd = gscore[g] > best
        gi = jnp.where(upd, g, gi)
        best = jnp.where(upd, gscore[g], best)

    def in_group(rows, j):
        out = rows[(N_GROUPS - 1) * EXPERTS_PER_GROUP + j]
        for g in range(N_GROUPS - 2, -1, -1):
            out = jnp.where(gi == g, rows[g * EXPERTS_PER_GROUP + j], out)
        return out

    u = [in_group(sel, j) for j in range(EXPERTS_PER_GROUP)]
    s_in = [in_group(sc, j) for j in range(EXPERTS_PER_GROUP)]

    def argmax4(vals):
        idx = jnp.zeros_like(gi)
        m = vals[0]
        for j in range(1, EXPERTS_PER_GROUP):
            upd = vals[j] > m
            idx = jnp.where(upd, j, idx)
            m = jnp.where(upd, vals[j], m)
        return idx

    def pick(vals, idx):
        out = vals[EXPERTS_PER_GROUP - 1]
        for j in range(EXPERTS_PER_GROUP - 2, -1, -1):
            out = jnp.where(idx == j, vals[j], out)
        return out

    i1 = argmax4(u)
    i2 = argmax4([jnp.where(i1 == j, NEG_INF, u[j]) for j in range(EXPERTS_PER_GROUP)])
    w1, w2 = pick(s_in, i1), pick(s_in, i2)
    tot = w1 + w2
    w1, w2 = w1 / tot, w2 / tot
    lo, hi = jnp.minimum(i1, i2), jnp.maximum(i1, i2)
    first_lo = i1 < i2
    w_lo, w_hi = jnp.where(first_lo, w1, w2), jnp.where(first_lo, w2, w1)
    pair = jnp.where(lo == 0, hi - 1, jnp.where(lo == 1, hi + 1, 5))
    return gi * len(PAIRS) + pair, w_lo, w_hi


def _merge_kernel(x_ref, oa_ref, og_ref, op_ref, wg_ref, wb_ref, wo_ref, g1_ref, b1_ref, wr_ref, rb_ref,
                  xe_ref, meta_ref):
    x = x_ref[...]
    xb = x.astype(BF16)
    merged = None
    for n, br in enumerate((oa_ref, og_ref, op_ref)):
        gate = _sigmoid(_dot(xb, wg_ref[:, n * D_MODEL:(n + 1) * D_MODEL]))
        term = gate * _dot(br[...], wb_ref[n])
        merged = term if merged is None else merged + term
    mix = _dot(merged.astype(BF16), wo_ref[...])
    x1 = _layer_norm(DN_ALPHA * x + mix, g1_ref[...], b1_ref[...])
    xe_ref[:, 0:D_MODEL] = x1
    sc_t = _sigmoid(_dot_nt(wr_ref[...], x1.astype(BF16)))
    sel_t = sc_t + rb_ref[...]
    sc = [sc_t[e:e + 1, :] for e in range(N_EXPERTS)]
    sel = [sel_t[e:e + 1, :] for e in range(N_EXPERTS)]
    cls, w_lo, w_hi = _route(sc, sel)
    cls = cls.astype(F32)

    def rows(n):
        rid = lax.broadcasted_iota(I32, (n, TM), 0)
        return jnp.where(rid == 0, cls, jnp.where(rid == 1, w_lo, jnp.where(rid == 2, w_hi, 0.0)))

    meta_ref[...] = rows(SUBLANES)
    xe_ref[:, D_MODEL:XE_W] = rows(LANES).T


def _merge(x, oa, og, op, wg, wb, wo, g1, b1, wr_t, rb, layer):
    row = lambda w: pl.BlockSpec((TM, w), lambda i: (i, 0))
    lay = lambda *shape: pl.BlockSpec((None,) + shape, lambda i: (layer,) + (0,) * len(shape))
    const = lambda *shape: pl.BlockSpec(shape, lambda i: (0,) * len(shape))
    return pl.pallas_call(
        _merge_kernel,
        grid=(N_TILES,),
        in_specs=[row(D_MODEL), row(BRANCH_W), row(BRANCH_W), row(BRANCH_W),
                  lay(D_MODEL, N_BRANCH * D_MODEL), lay(N_BRANCH, BRANCH_W, D_MODEL), lay(D_MODEL, D_MODEL),
                  lay(1, D_MODEL), lay(1, D_MODEL), const(N_EXPERTS, D_MODEL), const(N_EXPERTS, 1)],
        out_specs=[row(XE_W), pl.BlockSpec((None, SUBLANES, TM), lambda i: (i, 0, 0))],
        out_shape=[jax.ShapeDtypeStruct((N_TOK, XE_W), F32), jax.ShapeDtypeStruct((N_TILES, SUBLANES, TM), F32)],
        compiler_params=_params("arbitrary"),
        name="merge",
    )(x, oa, og, op, wg, wb, wo, g1, b1, wr_t, rb)


def _plan_kernel(cls_ref, pos_ref, tcls_ref, nused_ref):
    cls = cls_ref[...]
    lane_r = lax.broadcasted_iota(I32, (LANES, 2 * LANES), 0)
    lane_c = lax.broadcasted_iota(I32, (LANES, 2 * LANES), 1)
    lane_mat = ((lane_c >= LANES) | (lane_r < lane_c)).astype(BF16)
    row_r = lax.broadcasted_iota(I32, (2 * PLAN_ROWS, PLAN_ROWS), 0)
    row_c = lax.broadcasted_iota(I32, (2 * PLAN_ROWS, PLAN_ROWS), 1)
    row_mat = ((row_r >= PLAN_ROWS) | (row_c < row_r)).astype(BF16)
    tile_start = (lax.broadcasted_iota(I32, (SUBLANES, LANES), 1) * TM_E).astype(F32)
    pos = jnp.zeros((PLAN_ROWS, LANES), F32)
    off = jnp.zeros((PLAN_ROWS, LANES), F32)
    tcls = jnp.zeros((SUBLANES, LANES), I32)
    for c in range(N_CLASSES):
        m = cls == c
        lanes = _dot(m.astype(BF16), lane_mat)
        rows = _dot(row_mat, lanes[:, LANES:].astype(BF16))
        rank = lanes[:, 0:LANES] + rows[0:PLAN_ROWS]
        count = rows[PLAN_ROWS:]
        pos = jnp.where(m, off + rank, pos)
        off = off + jnp.ceil(count * (1.0 / TM_E)) * TM_E
        tcls = tcls + (off[0:SUBLANES] <= tile_start).astype(I32)
    pos_ref[...] = pos.astype(I32)
    tcls_ref[...] = tcls
    nused_ref[...] = (off[0:SUBLANES] * (1.0 / TM_E)).astype(I32)


def _plan(cls2d):
    return pl.pallas_call(
        _plan_kernel,
        out_shape=[jax.ShapeDtypeStruct((PLAN_ROWS, LANES), I32),
                   jax.ShapeDtypeStruct((SUBLANES, LANES), I32),
                   jax.ShapeDtypeStruct((SUBLANES, LANES), I32)],
        compiler_params=pltpu.CompilerParams(vmem_limit_bytes=VMEM_LIMIT),
        name="plan",
    )(cls2d)


def _row_copies(pos_ref, src_hbm, dst_hbm, sem, scatter):
    base = pl.program_id(0) * TM

    def copy(r):
        p = pos_ref[0, r]
        if scatter:
            return pltpu.make_async_copy(src_hbm.at[pl.ds(base + r, 1)], dst_hbm.at[pl.ds(p, 1)], sem)
        return pltpu.make_async_copy(src_hbm.at[pl.ds(p, 1)], dst_hbm.at[pl.ds(base + r, 1)], sem)

    def start(r, carry):
        copy(r).start()
        return carry

    def wait(r, carry):
        copy(r).wait()
        return carry

    lax.fori_loop(0, TM, start, 0, unroll=8)
    lax.fori_loop(0, TM, wait, 0, unroll=8)


def _dispatch_kernel(pos_ref, x_hbm, xs_in_hbm, xs_hbm, sem):
    del xs_in_hbm
    _row_copies(pos_ref, x_hbm, xs_hbm, sem, scatter=True)


def _combine_kernel(pos_ref, ys_hbm, x_hbm, sem):
    _row_copies(pos_ref, ys_hbm, x_hbm, sem, scatter=False)


_POS_SPEC = pl.BlockSpec((None, 1, TM), lambda i: (i, 0, 0), memory_space=pltpu.SMEM)
_ANY = pl.BlockSpec(memory_space=pl.ANY)


def _dispatch(pos, xe, xs_prev):
    return pl.pallas_call(
        _dispatch_kernel,
        grid=(N_TILES,),
        in_specs=[_POS_SPEC, _ANY, _ANY],
        out_specs=_ANY,
        out_shape=jax.ShapeDtypeStruct((N_SORTED, XE_W), F32),
        scratch_shapes=[pltpu.SemaphoreType.DMA(())],
        input_output_aliases={2: 0},
        compiler_params=_params("arbitrary"),
        name="dispatch",
    )(pos, xe, xs_prev)


def _combine(pos, ys):
    return pl.pallas_call(
        _combine_kernel,
        grid=(N_TILES,),
        in_specs=[_POS_SPEC, _ANY],
        out_specs=_ANY,
        out_shape=jax.ShapeDtypeStruct((N_TOK, D_MODEL), F32),
        scratch_shapes=[pltpu.SemaphoreType.DMA(())],
        compiler_params=_params("arbitrary"),
        name="combine",
    )(pos, ys)


def _experts_kernel(ea_ref, eb_ref, nused_ref, xs_ref, wga_ref, wua_ref, wda_ref, wgb_ref, wub_ref, wdb_ref,
                    g2_ref, b2_ref, ys_ref):
    del ea_ref, eb_ref

    @pl.when(pl.program_id(0) < nused_ref[0])
    def _():
        x1 = xs_ref[:, 0:D_MODEL]
        xb = x1.astype(BF16)

        def expert(wg, wu, wd):
            a = _dot(xb, wg[...])
            h = a * _sigmoid(a) * _dot(xb, wu[...])
            return _dot(h.astype(BF16), wd[...])

        ffn = xs_ref[:, D_MODEL + 1:D_MODEL + 2] * expert(wga_ref, wua_ref, wda_ref)
        ffn = ffn + xs_ref[:, D_MODEL + 2:D_MODEL + 3] * expert(wgb_ref, wub_ref, wdb_ref)
        ys_ref[...] = _layer_norm(DN_ALPHA * x1 + ffn, g2_ref[...], b2_ref[...])


def _experts(tile_a, tile_b, nused, xs, w_gate, w_up, w_down, g2, b2, layer):
    tile = lambda w: pl.BlockSpec((TM_E, w), lambda i, ea, eb, nu: (jnp.minimum(i, nu[0] - 1), 0))
    wa = lambda *shape: pl.BlockSpec((None, None) + shape, lambda i, ea, eb, nu: (layer, ea[i], 0, 0))
    wb = lambda *shape: pl.BlockSpec((None, None) + shape, lambda i, ea, eb, nu: (layer, eb[i], 0, 0))
    lay = pl.BlockSpec((None, 1, D_MODEL), lambda i, ea, eb, nu: (layer, 0, 0))
    return pl.pallas_call(
        _experts_kernel,
        grid_spec=pltpu.PrefetchScalarGridSpec(
            num_scalar_prefetch=3,
            grid=(NT_E,),
            in_specs=[tile(XE_W),
                      wa(D_MODEL, D_EXPERT), wa(D_MODEL, D_EXPERT), wa(D_EXPERT, D_MODEL),
                      wb(D_MODEL, D_EXPERT), wb(D_MODEL, D_EXPERT), wb(D_EXPERT, D_MODEL),
                      lay, lay],
            out_specs=tile(D_MODEL),
        ),
        out_shape=jax.ShapeDtypeStruct((N_SORTED, D_MODEL), F32),
        compiler_params=_params("arbitrary"),
        name="experts",
    )(tile_a, tile_b, nused, xs, w_gate, w_up, w_down, w_gate, w_up, w_down, g2, b2)


_PAIR_LO = tuple(a for a, _ in PAIRS)
_PAIR_HI = tuple(b for _, b in PAIRS)


def kernel(x_prompt, x_sample, cache_attn_k, cache_attn_v, state_gla, state_pool, w_in, w_gate, attn_sink,
           gla_w_a2, gla_b_a, gla_norm_g, pool_w, pool_scale, w_branch, w_o, ln1_g, ln1_b, ln2_g, ln2_b,
           w_router, router_bias, w_e_gate, w_e_up, w_e_down):
    ag0 = PA_W + PG_W
    w_in_p = jnp.concatenate(
        [w_in[:, :, :ag0], w_in[:, :, ag0 + GLA_RANK:], w_in[:, :, ag0:ag0 + GLA_RANK],
         jnp.zeros((DEPTH, D_MODEL, LANES - GLA_RANK), w_in.dtype)], axis=-1).astype(BF16)
    w_gate_b, w_branch_b, w_o_b = w_gate.astype(BF16), w_branch.astype(BF16), w_o.astype(BF16)
    wa2_p = jnp.pad(gla_w_a2, ((0, 0), (0, LANES - GLA_RANK), (0, 0))).astype(BF16)
    pool_w_b = pool_w.astype(BF16)
    wr_t = w_router.T.astype(BF16)
    rb = router_bias.reshape(N_EXPERTS, 1).astype(F32)
    weg, weu, wed = w_e_gate.astype(BF16), w_e_up.astype(BF16), w_e_down.astype(BF16)
    cache_k = cache_attn_k.reshape(DEPTH, DEC_BATCH, WINDOW, KV_W)
    cache_v = cache_attn_v.reshape(DEPTH, DEC_BATCH, WINDOW, KV_W)
    pair_lo, pair_hi = jnp.array(_PAIR_LO, I32), jnp.array(_PAIR_HI, I32)

    x = jnp.concatenate([x_prompt.reshape(NP, D_MODEL), x_sample.reshape(NS, D_MODEL)], axis=0)
    xs = jnp.zeros((N_SORTED, XE_W), F32)
    pk, pv, ps, pp, sk, sv, ss, sp = ([] for _ in range(8))
    for l in range(DEPTH):
        pa, pg, pu, pag = _in_proj(x, w_in_p, l)
        row2 = lambda a: a[l].reshape(1, -1)

        oa = _attn_prompt(pa, attn_sink[l])
        oa, nk, nv = _attn_sample(pa, attn_sink[l], cache_k, cache_v, oa, l)
        og, s_p = _gla_prompt(pg, pag, wa2_p[l], row2(gla_b_a), row2(gla_norm_g))
        og, s_s = _gla_sample(pg, pag, wa2_p[l], row2(gla_b_a), row2(gla_norm_g), state_gla, og, l)
        op = _pool_prompt(pu, pool_w_b[l], row2(pool_scale))
        u_s = pu[NP:].reshape(DEC_BATCH, DEC_SEQ, POOL_W)
        ext_s = jnp.concatenate(
            [jnp.zeros((DEC_BATCH, POOL_HIST - POOL_STATE, POOL_W), F32), state_pool[l], u_s], axis=1)
        op = _pool_sample(ext_s, pool_w_b[l], row2(pool_scale), op)

        xe, meta = _merge(x, oa, og, op, w_gate_b, w_branch_b, w_o_b, ln1_g.reshape(DEPTH, 1, D_MODEL),
                          ln1_b.reshape(DEPTH, 1, D_MODEL), wr_t, rb, l)
        cls = meta[:, 0, :].reshape(N_TOK // LANES, LANES).astype(I32)
        cls2d = jnp.pad(cls, ((0, PLAN_ROWS - N_TOK // LANES), (0, 0)), constant_values=-1)
        pos2d, tcls, nused = _plan(cls2d)
        pos = pos2d.reshape(-1)[:N_TOK].reshape(N_TILES, 1, TM)
        tile_cls = jnp.minimum(tcls[0, :NT_E], N_CLASSES - 1)
        tile_a = (tile_cls // len(PAIRS)) * EXPERTS_PER_GROUP + pair_lo[tile_cls % len(PAIRS)]
        tile_b = (tile_cls // len(PAIRS)) * EXPERTS_PER_GROUP + pair_hi[tile_cls % len(PAIRS)]
        xs = _dispatch(pos, xe, xs)
        ys = _experts(tile_a, tile_b, nused[0, :1], xs, weg, weu, wed, ln2_g.reshape(DEPTH, 1, D_MODEL),
                      ln2_b.reshape(DEPTH, 1, D_MODEL), l)
        x = _combine(pos, ys)

        kp = pa[:NP].reshape(BATCH, SEQ, PA_W)[:, SEQ - WINDOW:, :]
        pk.append(kp[:, :, ATTN_W:ATTN_W + KV_W].reshape(BATCH, WINDOW, N_KV_HEADS, HEAD_DIM))
        pv.append(kp[:, :, ATTN_W + KV_W:].reshape(BATCH, WINDOW, N_KV_HEADS, HEAD_DIM))
        ps.append(s_p)
        pp.append(pu[:NP].reshape(BATCH, SEQ, POOL_W)[:, SEQ - POOL_STATE:, :])
        sk.append(nk.reshape(DEC_BATCH, WINDOW, N_KV_HEADS, HEAD_DIM))
        sv.append(nv.reshape(DEC_BATCH, WINDOW, N_KV_HEADS, HEAD_DIM))
        ss.append(s_s)
        sp.append(ext_s[:, POOL_HIST + DEC_SEQ - POOL_STATE:, :])
    return (x[:NP].reshape(BATCH, SEQ, D_MODEL), x[NP:].reshape(DEC_BATCH, DEC_SEQ, D_MODEL),
            jnp.stack(pk), jnp.stack(pv), jnp.stack(ps), jnp.stack(pp),
            jnp.stack(sk), jnp.stack(sv), jnp.stack(ss), jnp.stack(sp))
```

```python
import functools

import jax
import jax.numpy as jnp
from jax import lax
from jax.experimental import pallas as pl
from jax.experimental.pallas import tpu as pltpu

F32, BF16, I32 = jnp.float32, jnp.bfloat16, jnp.int32

D_MODEL = 1024
BATCH = 8
SEQ = 2048
DEPTH = 4
DEC_BATCH = 128
DEC_SEQ = 8
N_HEADS = 8
N_KV_HEADS = 2
HEAD_DIM = 64
WINDOW = 128
ATTN_W = N_HEADS * HEAD_DIM
KV_W = N_KV_HEADS * HEAD_DIM
Q_PER_KV = N_HEADS // N_KV_HEADS
GLA_HEADS = 4
GLA_DK = 64
GLA_DV = 128
GLA_KW = GLA_HEADS * GLA_DK
GLA_VW = GLA_HEADS * GLA_DV
GLA_RANK = 16
GLA_TAU = 16.0
GLA_CHUNK = 64
POOL_WINDOWS = (2, 4, 8, 16)
POOL_GROUPS = 4
POOL_GW = 128
POOL_W = POOL_GROUPS * POOL_GW
POOL_STATE = max(POOL_WINDOWS) - 1
N_BRANCH = 3
BRANCH_W = 512
N_EXPERTS = 16
N_GROUPS = 4
EXPERTS_PER_GROUP = N_EXPERTS // N_GROUPS
D_EXPERT = 512
DN_ALPHA = (2.0 * DEPTH) ** 0.25
LN_EPS = 1e-5
RMS_EPS = 1e-6
NEG_INF = -1e30
ALIBI_SLOPES = tuple(2.0 ** (-8.0 * h / N_HEADS) for h in range(1, N_HEADS + 1))

LANES = 128
SUBLANES = 8

NP = BATCH * SEQ
NS = DEC_BATCH * DEC_SEQ
N_TOK = NP + NS
TM = 512
N_TILES = N_TOK // TM

PA_W = ATTN_W + 2 * KV_W
PG_W = 2 * GLA_KW + 2 * GLA_VW
IN_W_PAD = PA_W + PG_W + POOL_W + LANES

PAIRS = ((0, 1), (0, 2), (0, 3), (1, 2), (1, 3), (2, 3))
N_CLASSES = N_GROUPS * len(PAIRS)
TM_E = 256
NT_E = -(-(N_TOK + N_CLASSES * (TM_E - 1)) // TM_E)
N_SORTED = NT_E * TM_E
XE_W = D_MODEL + LANES
PLAN_ROWS = 256

VMEM_LIMIT = 56 * 1024 * 1024


def _params(*sem):
    return pltpu.CompilerParams(dimension_semantics=sem, vmem_limit_bytes=VMEM_LIMIT)


def _dot(a, b):
    return jnp.dot(a, b, preferred_element_type=F32)


def _dot_nt(a, b):
    return lax.dot_general(a, b, (((1,), (1,)), ((), ())), preferred_element_type=F32)


def _dot_tn(a, b):
    return lax.dot_general(a, b, (((0,), (0,)), ((), ())), preferred_element_type=F32)


def _layer_norm(h, g, b):
    mu = jnp.mean(h, axis=-1, keepdims=True)
    hc = h - mu
    var = jnp.mean(hc * hc, axis=-1, keepdims=True)
    return hc * lax.rsqrt(var + LN_EPS) * g + b


def _sigmoid(x):
    return 1.0 / (1.0 + jnp.exp(-x))


def _log_sigmoid(x):
    return jnp.minimum(x, 0.0) - jnp.log1p(jnp.exp(-jnp.abs(x)))


def _in_proj_kernel(x_ref, w_ref, pa_ref, pg_ref, pu_ref, pag_ref):
    xb = x_ref[...].astype(BF16)
    pa_ref[...] = _dot(xb, w_ref[:, 0:PA_W])
    pg_ref[...] = _dot(xb, w_ref[:, PA_W:PA_W + PG_W])
    pu_ref[...] = _dot(xb, w_ref[:, PA_W + PG_W:PA_W + PG_W + POOL_W])
    pag_ref[...] = _dot(xb, w_ref[:, PA_W + PG_W + POOL_W:IN_W_PAD])


def _in_proj(x, w_in_p, layer):
    row = lambda w: pl.BlockSpec((TM, w), lambda i: (i, 0))
    return pl.pallas_call(
        _in_proj_kernel,
        grid=(N_TILES,),
        in_specs=[row(D_MODEL), pl.BlockSpec((None, D_MODEL, IN_W_PAD), lambda i: (layer, 0, 0))],
        out_specs=[row(PA_W), row(PG_W), row(POOL_W), row(LANES)],
        out_shape=[jax.ShapeDtypeStruct((N_TOK, w), F32) for w in (PA_W, PG_W, POOL_W, LANES)],
        compiler_params=_params("arbitrary"),
        name="in_proj",
    )(x, w_in_p)


def _softmax_sink_pv(parts, sink):
    m = sink
    for s, _ in parts:
        m = jnp.maximum(m, jnp.max(s, axis=1, keepdims=True))
    den = jnp.exp(sink - m)
    es = []
    for s, _ in parts:
        e = jnp.exp(s - m)
        den = den + jnp.sum(e, axis=1, keepdims=True)
        es.append(e)
    out = None
    for e, (_, v) in zip(es, parts):
        o = _dot((e / den).astype(BF16), v)
        out = o if out is None else out + o
    return out


def _attn_prompt_kernel(sink_ref, cur_ref, prev_ref, o_ref):
    i = pl.program_id(1)
    k = jnp.concatenate([prev_ref[:, 0:KV_W], cur_ref[:, ATTN_W:ATTN_W + KV_W]], axis=0).astype(BF16)
    v = jnp.concatenate([prev_ref[:, KV_W:2 * KV_W], cur_ref[:, ATTN_W + KV_W:PA_W]], axis=0).astype(BF16)
    r = lax.broadcasted_iota(I32, (WINDOW, 2 * WINDOW), 0)
    c = lax.broadcasted_iota(I32, (WINDOW, 2 * WINDOW), 1)
    dist = (WINDOW + r - c).astype(F32)
    mask = (c > r) & (c <= WINDOW + r) & ((i > 0) | (c >= WINDOW))
    for h in range(N_HEADS):
        kv = h // Q_PER_KV
        qh = cur_ref[:, h * HEAD_DIM:(h + 1) * HEAD_DIM].astype(BF16)
        kh = k[:, kv * HEAD_DIM:(kv + 1) * HEAD_DIM]
        vh = v[:, kv * HEAD_DIM:(kv + 1) * HEAD_DIM]
        s = _dot_nt(qh, kh) * (HEAD_DIM ** -0.5) - ALIBI_SLOPES[h] * dist
        s = jnp.where(mask, s, NEG_INF)
        o = _softmax_sink_pv([(s, vh)], sink_ref[h])
        o_ref[:, h * HEAD_DIM:(h + 1) * HEAD_DIM] = o.astype(BF16)


def _attn_prompt(pa, sink):
    nb = SEQ // WINDOW
    return pl.pallas_call(
        _attn_prompt_kernel,
        grid_spec=pltpu.PrefetchScalarGridSpec(
            num_scalar_prefetch=1,
            grid=(BATCH, nb),
            in_specs=[
                pl.BlockSpec((WINDOW, PA_W), lambda b, i, s: (b * nb + i, 0)),
                pl.BlockSpec((WINDOW, 2 * KV_W), lambda b, i, s: (b * nb + jnp.maximum(i - 1, 0), ATTN_W // (2 * KV_W))),
            ],
            out_specs=pl.BlockSpec((WINDOW, ATTN_W), lambda b, i, s: (b * nb + i, 0)),
        ),
        out_shape=jax.ShapeDtypeStruct((NP, ATTN_W), BF16),
        compiler_params=_params("arbitrary", "arbitrary"),
        name="attn_prompt",
    )(sink, pa, pa)


ATTN_SB = 8


def _attn_sample_kernel(sink_ref, cur_ref, kc_ref, vc_ref, o_ref, nk_ref, nv_ref):
    rows = Q_PER_KV * DEC_SEQ
    r = lax.broadcasted_iota(I32, (rows, WINDOW), 0)
    t = r % DEC_SEQ
    j = lax.broadcasted_iota(I32, (rows, WINDOW), 1)
    dist_c = (WINDOW + t - j).astype(F32)
    mask_c = j > t
    rn = lax.broadcasted_iota(I32, (rows, DEC_SEQ), 0)
    tn = rn % DEC_SEQ
    jn = lax.broadcasted_iota(I32, (rows, DEC_SEQ), 1)
    dist_n = (tn - jn).astype(F32)
    mask_n = jn <= tn
    gcol = lax.broadcasted_iota(I32, (rows, 1), 0) // DEC_SEQ
    for kv in range(N_KV_HEADS):
        slope = jnp.zeros((rows, 1), F32)
        sink = jnp.zeros((rows, 1), F32)
        for g in range(Q_PER_KV):
            slope = jnp.where(gcol == g, ALIBI_SLOPES[kv * Q_PER_KV + g], slope)
            sink = jnp.where(gcol == g, sink_ref[kv * Q_PER_KV + g], sink)
        ks = slice(kv * HEAD_DIM, (kv + 1) * HEAD_DIM)
        for s in range(ATTN_SB):
            rs = slice(s * DEC_SEQ, (s + 1) * DEC_SEQ)
            q = jnp.concatenate(
                [cur_ref[rs, (kv * Q_PER_KV + g) * HEAD_DIM:(kv * Q_PER_KV + g + 1) * HEAD_DIM] for g in range(Q_PER_KV)],
                axis=0).astype(BF16)
            kc = kc_ref[s, :, ks].astype(BF16)
            vc = vc_ref[s, :, ks].astype(BF16)
            kn = cur_ref[rs, ATTN_W + kv * HEAD_DIM:ATTN_W + (kv + 1) * HEAD_DIM].astype(BF16)
            vn = cur_ref[rs, ATTN_W + KV_W + kv * HEAD_DIM:ATTN_W + KV_W + (kv + 1) * HEAD_DIM].astype(BF16)
            sc = _dot_nt(q, kc) * (HEAD_DIM ** -0.5) - slope * dist_c
            sc = jnp.where(mask_c, sc, NEG_INF)
            sn = _dot_nt(q, kn) * (HEAD_DIM ** -0.5) - slope * dist_n
            sn = jnp.where(mask_n, sn, NEG_INF)
            o = _softmax_sink_pv([(sc, vc), (sn, vn)], sink)
            for g in range(Q_PER_KV):
                h = kv * Q_PER_KV + g
                o_ref[rs, h * HEAD_DIM:(h + 1) * HEAD_DIM] = o[g * DEC_SEQ:(g + 1) * DEC_SEQ].astype(BF16)
    keep = WINDOW - DEC_SEQ
    for s in range(ATTN_SB):
        rs = slice(s * DEC_SEQ, (s + 1) * DEC_SEQ)
        nk_ref[s, 0:keep, :] = kc_ref[s, DEC_SEQ:WINDOW, :]
        nk_ref[s, keep:WINDOW, :] = cur_ref[rs, ATTN_W:ATTN_W + KV_W]
        nv_ref[s, 0:keep, :] = vc_ref[s, DEC_SEQ:WINDOW, :]
        nv_ref[s, keep:WINDOW, :] = cur_ref[rs, ATTN_W + KV_W:PA_W]


def _attn_sample(pa, sink, cache_k, cache_v, layer):
    rows = ATTN_SB * DEC_SEQ
    first = NP // rows
    cache = pl.BlockSpec((None, ATTN_SB, WINDOW, KV_W), lambda j, s: (layer, j, 0, 0))
    new = pl.BlockSpec((ATTN_SB, WINDOW, KV_W), lambda j, s: (j, 0, 0))
    return pl.pallas_call(
        _attn_sample_kernel,
        grid_spec=pltpu.PrefetchScalarGridSpec(
            num_scalar_prefetch=1,
            grid=(DEC_BATCH // ATTN_SB,),
            in_specs=[pl.BlockSpec((rows, PA_W), lambda j, s: (first + j, 0)), cache, cache],
            out_specs=[pl.BlockSpec((rows, ATTN_W), lambda j, s: (j, 0)), new, new],
        ),
        out_shape=[jax.ShapeDtypeStruct((NS, ATTN_W), BF16),
                   jax.ShapeDtypeStruct((DEC_BATCH, WINDOW, KV_W), F32),
                   jax.ShapeDtypeStruct((DEC_BATCH, WINDOW, KV_W), F32)],
        compiler_params=_params("arbitrary"),
        name="attn_sample",
    )(sink, pa, cache_k, cache_v)


def _chunk_cumsum(x, chunk):
    pos = lax.broadcasted_iota(I32, x.shape, 0) % chunk
    sh = 1
    while sh < chunk:
        x = x + jnp.where(pos >= sh, pltpu.roll(x, sh, 0), 0.0)
        sh *= 2
    return x


def _gla_log_decay(pag_ref, wa2_ref, ba_ref):
    z = _dot(pag_ref[...].astype(BF16), wa2_ref[...]) + ba_ref[...]
    return _log_sigmoid(z) / GLA_TAU


def _gla_finish(o, rg, g):
    o = o * lax.rsqrt(jnp.mean(o * o, axis=-1, keepdims=True) + RMS_EPS)
    return (o * g * (rg * _sigmoid(rg))).astype(BF16)


GLA_TT = 512


def _gla_prompt_kernel(pg_ref, pag_ref, wa2_ref, ba_ref, g_ref, o_ref, s_ref, cum_ref, st_ref):
    j = pl.program_id(1)

    @pl.when(j == 0)
    def _():
        st_ref[...] = jnp.zeros_like(st_ref)

    cum_ref[...] = _chunk_cumsum(_gla_log_decay(pag_ref, wa2_ref, ba_ref), GLA_CHUNK)
    tri = (lax.broadcasted_iota(I32, (GLA_CHUNK, GLA_CHUNK), 0)
           >= lax.broadcasted_iota(I32, (GLA_CHUNK, GLA_CHUNK), 1))

    def chunk(c, carry):
        rows = pl.ds(pl.multiple_of(c * GLA_CHUNK, GLA_CHUNK), GLA_CHUNK)
        cum = cum_ref[rows, :]
        tot = cum[GLA_CHUNK - 1:GLA_CHUNK, :]
        q = pg_ref[rows, 0:GLA_KW] * (GLA_DK ** -0.5)
        k = pg_ref[rows, GLA_KW:2 * GLA_KW]
        qd = (q * jnp.exp(cum)).astype(BF16)
        kd = (k * jnp.exp(-cum)).astype(BF16)
        kdec = (k * jnp.exp(tot - cum)).astype(BF16)
        etot = jnp.exp(tot)
        for h in range(GLA_HEADS):
            ks = slice(h * GLA_DK, (h + 1) * GLA_DK)
            vs = slice(2 * GLA_KW + h * GLA_DV, 2 * GLA_KW + (h + 1) * GLA_DV)
            rs = slice(2 * GLA_KW + GLA_VW + h * GLA_DV, 2 * GLA_KW + GLA_VW + (h + 1) * GLA_DV)
            v = pg_ref[rows, vs].astype(BF16)
            st = st_ref[:, ks]
            att = jnp.where(tri, _dot_nt(qd[:, ks], kd[:, ks]), 0.0)
            o = _dot_nt(qd[:, ks], st.astype(BF16)) + _dot(att.astype(BF16), v)
            st_ref[:, ks] = etot[:, ks] * st + _dot_tn(v, kdec[:, ks])
            o_ref[rows, h * GLA_DV:(h + 1) * GLA_DV] = _gla_finish(o, pg_ref[rows, rs], g_ref[:, h * GLA_DV:(h + 1) * GLA_DV])
        return carry

    lax.fori_loop(0, GLA_TT // GLA_CHUNK, chunk, 0)

    @pl.when(j == pl.num_programs(1) - 1)
    def _():
        for h in range(GLA_HEADS):
            s_ref[h] = st_ref[:, h * GLA_DK:(h + 1) * GLA_DK].T


def _gla_prompt(pg, pag, wa2, ba, g):
    nt = SEQ // GLA_TT
    const = lambda shape: pl.BlockSpec(shape, lambda b, j: (0,) * len(shape))
    return pl.pallas_call(
        _gla_prompt_kernel,
        grid=(BATCH, nt),
        in_specs=[
            pl.BlockSpec((GLA_TT, PG_W), lambda b, j: (b * nt + j, 0)),
            pl.BlockSpec((GLA_TT, LANES), lambda b, j: (b * nt + j, 0)),
            const((LANES, GLA_KW)), const((1, GLA_KW)), const((1, GLA_VW)),
        ],
        out_specs=[
            pl.BlockSpec((GLA_TT, GLA_VW), lambda b, j: (b * nt + j, 0)),
            pl.BlockSpec((None, GLA_HEADS, GLA_DK, GLA_DV), lambda b, j: (b, 0, 0, 0)),
        ],
        out_shape=[jax.ShapeDtypeStruct((NP, GLA_VW), BF16),
                   jax.ShapeDtypeStruct((BATCH, GLA_HEADS, GLA_DK, GLA_DV), F32)],
        scratch_shapes=[pltpu.VMEM((GLA_TT, GLA_KW), F32), pltpu.VMEM((GLA_DV, GLA_KW), F32)],
        compiler_params=_params("arbitrary", "arbitrary"),
        name="gla_prompt",
    )(pg, pag, wa2, ba, g)


GLA_SB = 8


def _gla_sample_kernel(pg_ref, pag_ref, wa2_ref, ba_ref, g_ref, s0_ref, o_ref, s_ref):
    cum_all = _chunk_cumsum(_gla_log_decay(pag_ref, wa2_ref, ba_ref), DEC_SEQ)
    tri = (lax.broadcasted_iota(I32, (DEC_SEQ, DEC_SEQ), 0) >= lax.broadcasted_iota(I32, (DEC_SEQ, DEC_SEQ), 1))
    for s in range(GLA_SB):
        rows = slice(s * DEC_SEQ, (s + 1) * DEC_SEQ)
        cum = cum_all[rows, :]
        tot = cum[DEC_SEQ - 1:DEC_SEQ, :]
        q = pg_ref[rows, 0:GLA_KW] * (GLA_DK ** -0.5)
        k = pg_ref[rows, GLA_KW:2 * GLA_KW]
        qd = (q * jnp.exp(cum)).astype(BF16)
        kd = (k * jnp.exp(-cum)).astype(BF16)
        kdec = (k * jnp.exp(tot - cum)).astype(BF16)
        etot = jnp.exp(tot)
        etot_col = [jnp.broadcast_to(etot[:, p * LANES:(p + 1) * LANES], (SUBLANES, LANES)).T[:, 0:1]
                    for p in range(GLA_KW // LANES)]
        for h in range(GLA_HEADS):
            ks = slice(h * GLA_DK, (h + 1) * GLA_DK)
            vs = slice(2 * GLA_KW + h * GLA_DV, 2 * GLA_KW + (h + 1) * GLA_DV)
            rs = slice(2 * GLA_KW + GLA_VW + h * GLA_DV, 2 * GLA_KW + GLA_VW + (h + 1) * GLA_DV)
            v = pg_ref[rows, vs].astype(BF16)
            st = s0_ref[s, h]
            att = jnp.where(tri, _dot_nt(qd[:, ks], kd[:, ks]), 0.0)
            o = _dot(qd[:, ks], st.astype(BF16)) + _dot(att.astype(BF16), v)
            per = LANES // GLA_DK
            col = etot_col[h // per][(h % per) * GLA_DK:(h % per + 1) * GLA_DK, :]
            s_ref[s, h] = col * st + _dot_tn(kdec[:, ks], v)
            o_ref[rows, h * GLA_DV:(h + 1) * GLA_DV] = _gla_finish(o, pg_ref[rows, rs], g_ref[:, h * GLA_DV:(h + 1) * GLA_DV])


def _gla_sample(pg, pag, wa2, ba, g, state, layer):
    rows = GLA_SB * DEC_SEQ
    first = NP // rows
    const = lambda shape: pl.BlockSpec(shape, lambda j: (0,) * len(shape))
    return pl.pallas_call(
        _gla_sample_kernel,
        grid=(DEC_BATCH // GLA_SB,),
        in_specs=[
            pl.BlockSpec((rows, PG_W), lambda j: (first + j, 0)),
            pl.BlockSpec((rows, LANES), lambda j: (first + j, 0)),
            const((LANES, GLA_KW)), const((1, GLA_KW)), const((1, GLA_VW)),
            pl.BlockSpec((None, GLA_SB, GLA_HEADS, GLA_DK, GLA_DV), lambda j: (layer, j, 0, 0, 0)),
        ],
        out_specs=[
            pl.BlockSpec((rows, GLA_VW), lambda j: (j, 0)),
            pl.BlockSpec((GLA_SB, GLA_HEADS, GLA_DK, GLA_DV), lambda j: (j, 0, 0, 0)),
        ],
        out_shape=[jax.ShapeDtypeStruct((NS, GLA_VW), BF16),
                   jax.ShapeDtypeStruct((DEC_BATCH, GLA_HEADS, GLA_DK, GLA_DV), F32)],
        compiler_params=_params("arbitrary"),
        name="gla_sample",
    )(pg, pag, wa2, ba, g, state)


POOL_HIST = 16


def _pool_groups(ext, cnt, pw_ref, ps_ref, out_rows):
    ax = ext.ndim - 2
    outs = []
    for g, w in enumerate(POOL_WINDOWS):
        x = ext[..., g * POOL_GW:(g + 1) * POOL_GW]
        s, sh = x, 1
        while sh < w:
            s = s + pltpu.roll(s, sh, ax)
            sh *= 2
        if ext.ndim == 3:
            d = (s[:, POOL_HIST:, :] / cnt[g] - x[:, POOL_HIST:, :]).reshape(out_rows, POOL_GW)
        else:
            d = s[POOL_HIST:, :] / cnt[g] - x[POOL_HIST:, :]
        y = _dot(d.astype(BF16), pw_ref[g]) * ps_ref[:, g * POOL_GW:(g + 1) * POOL_GW]
        outs.append(y.astype(BF16))
    return outs


POOL_TT = 512


def _pool_prompt_kernel(pu_ref, pw_ref, ps_ref, o_ref, hist_ref):
    j = pl.program_id(1)

    @pl.when(j == 0)
    def _():
        hist_ref[...] = jnp.zeros_like(hist_ref)

    u = pu_ref[...]
    ext = jnp.concatenate([hist_ref[...], u], axis=0)
    pos = j * POOL_TT + lax.broadcasted_iota(I32, (POOL_TT, 1), 0)
    cnt = [jnp.minimum(pos + 1, w).astype(F32) for w in POOL_WINDOWS]
    for g, y in enumerate(_pool_groups(ext, cnt, pw_ref, ps_ref, POOL_TT)):
        o_ref[:, g * POOL_GW:(g + 1) * POOL_GW] = y
    hist_ref[...] = u[POOL_TT - POOL_HIST:, :]


def _pool_prompt(pu, pw, ps):
    nt = SEQ // POOL_TT
    return pl.pallas_call(
        _pool_prompt_kernel,
        grid=(BATCH, nt),
        in_specs=[
            pl.BlockSpec((POOL_TT, POOL_W), lambda b, j: (b * nt + j, 0)),
            pl.BlockSpec((POOL_GROUPS, POOL_GW, POOL_GW), lambda b, j: (0, 0, 0)),
            pl.BlockSpec((1, POOL_W), lambda b, j: (0, 0)),
        ],
        out_specs=pl.BlockSpec((POOL_TT, POOL_W), lambda b, j: (b * nt + j, 0)),
        out_shape=jax.ShapeDtypeStruct((NP, POOL_W), BF16),
        scratch_shapes=[pltpu.VMEM((POOL_HIST, POOL_W), F32)],
        compiler_params=_params("arbitrary", "arbitrary"),
        name="pool_prompt",
    )(pu, pw, ps)


POOL_SB = 16


def _pool_sample_kernel(ext_ref, pw_ref, ps_ref, o_ref):
    cnt = [float(w) for w in POOL_WINDOWS]
    for g, y in enumerate(_pool_groups(ext_ref[...], cnt, pw_ref, ps_ref, POOL_SB * DEC_SEQ)):
        o_ref[:, g * POOL_GW:(g + 1) * POOL_GW] = y


def _pool_sample(ext, pw, ps):
    rows = POOL_SB * DEC_SEQ
    return pl.pallas_call(
        _pool_sample_kernel,
        grid=(DEC_BATCH // POOL_SB,),
        in_specs=[
            pl.BlockSpec((POOL_SB, POOL_HIST + DEC_SEQ, POOL_W), lambda j: (j, 0, 0)),
            pl.BlockSpec((POOL_GROUPS, POOL_GW, POOL_GW), lambda j: (0, 0, 0)),
            pl.BlockSpec((1, POOL_W), lambda j: (0, 0)),
        ],
        out_specs=pl.BlockSpec((rows, POOL_W), lambda j: (j, 0)),
        out_shape=jax.ShapeDtypeStruct((NS, POOL_W), BF16),
        compiler_params=_params("arbitrary"),
        name="pool_sample",
    )(ext, pw, ps)


def _route(sc, sel):
    gscore = []
    for g in range(N_GROUPS):
        v = sel[EXPERTS_PER_GROUP * g:EXPERTS_PER_GROUP * (g + 1)]
        best = None
        for a, b in PAIRS:
            pair = v[a] + v[b]
            best = pair if best is None else jnp.maximum(best, pair)
        gscore.append(best)
    gi = jnp.zeros_like(gscore[0], dtype=I32)
    best = gscore[0]
    for g in range(1, N_GROUPS):
        upd = gscore[g] > best
        gi = jnp.where(upd, g, gi)
        best = jnp.where(upd, gscore[g], best)

    def in_group(rows, j):
        out = rows[(N_GROUPS - 1) * EXPERTS_PER_GROUP + j]
        for g in range(N_GROUPS - 2, -1, -1):
            out = jnp.where(gi == g, rows[g * EXPERTS_PER_GROUP + j], out)
        return out

    u = [in_group(sel, j) for j in range(EXPERTS_PER_GROUP)]
    s_in = [in_group(sc, j) for j in range(EXPERTS_PER_GROUP)]

    def argmax4(vals):
        idx = jnp.zeros_like(gi)
        m = vals[0]
        for j in range(1, EXPERTS_PER_GROUP):
            upd = vals[j] > m
            idx = jnp.where(upd, j, idx)
            m = jnp.where(upd, vals[j], m)
        return idx

    def pick(vals, idx):
        out = vals[EXPERTS_PER_GROUP - 1]
        for j in range(EXPERTS_PER_GROUP - 2, -1, -1):
            out = jnp.where(idx == j, vals[j], out)
        return out

    i1 = argmax4(u)
    i2 = argmax4([jnp.where(i1 == j, NEG_INF, u[j]) for j in range(EXPERTS_PER_GROUP)])
    w1, w2 = pick(s_in, i1), pick(s_in, i2)
    tot = w1 + w2
    w1, w2 = w1 / tot, w2 / tot
    lo, hi = jnp.minimum(i1, i2), jnp.maximum(i1, i2)
    first_lo = i1 < i2
    w_lo, w_hi = jnp.where(first_lo, w1, w2), jnp.where(first_lo, w2, w1)
    pair = jnp.where(lo == 0, hi - 1, jnp.where(lo == 1, hi + 1, 5))
    return gi * len(PAIRS) + pair, w_lo, w_hi


def _merge_kernel(x_ref, oap_ref, oas_ref, ogp_ref, ogs_ref, opp_ref, ops_ref, wg_ref, wb_ref, wo_ref,
                  g1_ref, b1_ref, wr_ref, rb_ref, xe_ref, meta_ref):
    x = x_ref[...]
    xb = x.astype(BF16)
    is_prompt = pl.program_id(0) < NP // TM
    merged = None
    for n, (brp, brs) in enumerate(((oap_ref, oas_ref), (ogp_ref, ogs_ref), (opp_ref, ops_ref))):
        gate = _sigmoid(_dot(xb, wg_ref[:, n * D_MODEL:(n + 1) * D_MODEL]))
        term = gate * _dot(jnp.where(is_prompt, brp[...], brs[...]), wb_ref[n])
        merged = term if merged is None else merged + term
    mix = _dot(merged.astype(BF16), wo_ref[...])
    x1 = _layer_norm(DN_ALPHA * x + mix, g1_ref[...], b1_ref[...])
    xe_ref[:, 0:D_MODEL] = x1
    sc_t = _sigmoid(_dot_nt(wr_ref[...], x1.astype(BF16)))
    sel_t = sc_t + rb_ref[...]
    sc = [sc_t[e:e + 1, :] for e in range(N_EXPERTS)]
    sel = [sel_t[e:e + 1, :] for e in range(N_EXPERTS)]
    cls, w_lo, w_hi = _route(sc, sel)
    cls = cls.astype(F32)

    def rows(n):
        rid = lax.broadcasted_iota(I32, (n, TM), 0)
        return jnp.where(rid == 0, cls, jnp.where(rid == 1, w_lo, jnp.where(rid == 2, w_hi, 0.0)))

    meta_ref[...] = rows(SUBLANES)
    xe_ref[:, D_MODEL:XE_W] = rows(LANES).T


def _merge(x, branches, wg, wb, wo, g1, b1, wr_t, rb, layer):
    row = lambda w: pl.BlockSpec((TM, w), lambda i: (i, 0))
    lay = lambda *shape: pl.BlockSpec((None,) + shape, lambda i: (layer,) + (0,) * len(shape))
    const = lambda *shape: pl.BlockSpec(shape, lambda i: (0,) * len(shape))
    pt = NP // TM
    br_p = pl.BlockSpec((TM, BRANCH_W), lambda i: (jnp.minimum(i, pt - 1), 0))
    br_s = pl.BlockSpec((TM, BRANCH_W), lambda i: (jnp.maximum(i - pt, 0), 0))
    return pl.pallas_call(
        _merge_kernel,
        grid=(N_TILES,),
        in_specs=[row(D_MODEL), br_p, br_s, br_p, br_s, br_p, br_s,
                  lay(D_MODEL, N_BRANCH * D_MODEL), lay(N_BRANCH, BRANCH_W, D_MODEL), lay(D_MODEL, D_MODEL),
                  lay(1, D_MODEL), lay(1, D_MODEL), const(N_EXPERTS, D_MODEL), const(N_EXPERTS, 1)],
        out_specs=[row(XE_W), pl.BlockSpec((None, SUBLANES, TM), lambda i: (i, 0, 0))],
        out_shape=[jax.ShapeDtypeStruct((N_TOK, XE_W), F32), jax.ShapeDtypeStruct((N_TILES, SUBLANES, TM), F32)],
        compiler_params=_params("arbitrary"),
        name="merge",
    )(x, *branches, wg, wb, wo, g1, b1, wr_t, rb)


def _plan_kernel(cls_ref, pos_ref, tcls_ref, nused_ref):
    cls = cls_ref[...]
    lane_r = lax.broadcasted_iota(I32, (LANES, 2 * LANES), 0)
    lane_c = lax.broadcasted_iota(I32, (LANES, 2 * LANES), 1)
    lane_mat = ((lane_c >= LANES) | (lane_r < lane_c)).astype(BF16)
    row_r = lax.broadcasted_iota(I32, (2 * PLAN_ROWS, PLAN_ROWS), 0)
    row_c = lax.broadcasted_iota(I32, (2 * PLAN_ROWS, PLAN_ROWS), 1)
    row_mat = ((row_r >= PLAN_ROWS) | (row_c < row_r)).astype(BF16)
    tile_start = (lax.broadcasted_iota(I32, (SUBLANES, LANES), 1) * TM_E).astype(F32)
    pos = jnp.zeros((PLAN_ROWS, LANES), F32)
    off = jnp.zeros((PLAN_ROWS, LANES), F32)
    tcls = jnp.zeros((SUBLANES, LANES), I32)
    for c in range(N_CLASSES):
        m = cls == c
        lanes = _dot(m.astype(BF16), lane_mat)
        rows = _dot(row_mat, lanes[:, LANES:].astype(BF16))
        rank = lanes[:, 0:LANES] + rows[0:PLAN_ROWS]
        count = rows[PLAN_ROWS:]
        pos = jnp.where(m, off + rank, pos)
        off = off + jnp.ceil(count * (1.0 / TM_E)) * TM_E
        tcls = tcls + (off[0:SUBLANES] <= tile_start).astype(I32)
    pos_ref[...] = pos.astype(I32)
    tcls_ref[...] = tcls
    nused_ref[...] = (off[0:SUBLANES] * (1.0 / TM_E)).astype(I32)


def _plan(cls2d):
    return pl.pallas_call(
        _plan_kernel,
        out_shape=[jax.ShapeDtypeStruct((PLAN_ROWS, LANES), I32),
                   jax.ShapeDtypeStruct((SUBLANES, LANES), I32),
                   jax.ShapeDtypeStruct((SUBLANES, LANES), I32)],
        compiler_params=pltpu.CompilerParams(vmem_limit_bytes=VMEM_LIMIT),
        name="plan",
    )(cls2d)


def _row_copies(pos_ref, tile_ref, sorted_hbm, sem, scatter):
    def copy(r):
        row, srt = tile_ref.at[pl.ds(r, 1)], sorted_hbm.at[pl.ds(pos_ref[0, r], 1)]
        return pltpu.make_async_copy(row, srt, sem) if scatter else pltpu.make_async_copy(srt, row, sem)

    def start(r, carry):
        copy(r).start()
        return carry

    def wait(r, carry):
        copy(r).wait()
        return carry

    lax.fori_loop(0, TM, start, 0, unroll=8)
    lax.fori_loop(0, TM, wait, 0, unroll=8)


def _dispatch_kernel(pos_ref, x_ref, xs_in_hbm, xs_hbm, sem):
    del xs_in_hbm
    _row_copies(pos_ref, x_ref, xs_hbm, sem, scatter=True)


def _combine_kernel(pos_ref, ys_hbm, x_ref, sem):
    _row_copies(pos_ref, x_ref, ys_hbm, sem, scatter=False)


_POS_SPEC = pl.BlockSpec((None, 1, TM), lambda i: (i, 0, 0), memory_space=pltpu.SMEM)
_ANY = pl.BlockSpec(memory_space=pl.ANY)


def _dispatch(pos, xe, xs_prev):
    return pl.pallas_call(
        _dispatch_kernel,
        grid=(N_TILES,),
        in_specs=[_POS_SPEC, pl.BlockSpec((TM, XE_W), lambda i: (i, 0)), _ANY],
        out_specs=_ANY,
        out_shape=jax.ShapeDtypeStruct((N_SORTED, XE_W), F32),
        scratch_shapes=[pltpu.SemaphoreType.DMA(())],
        input_output_aliases={2: 0},
        compiler_params=_params("arbitrary"),
        name="dispatch",
    )(pos, xe, xs_prev)


def _combine(pos, ys):
    return pl.pallas_call(
        _combine_kernel,
        grid=(N_TILES,),
        in_specs=[_POS_SPEC, _ANY],
        out_specs=pl.BlockSpec((TM, D_MODEL), lambda i: (i, 0)),
        out_shape=jax.ShapeDtypeStruct((N_TOK, D_MODEL), F32),
        scratch_shapes=[pltpu.SemaphoreType.DMA(())],
        compiler_params=_params("arbitrary"),
        name="combine",
    )(pos, ys)


def _experts_kernel(ea_ref, eb_ref, nused_ref, xs_ref, wga_ref, wua_ref, wda_ref, wgb_ref, wub_ref, wdb_ref,
                    g2_ref, b2_ref, ys_ref):
    del ea_ref, eb_ref
    used = pl.program_id(0) < nused_ref[0]

    @pl.when(jnp.logical_not(used))
    def _():
        ys_ref[...] = jnp.zeros_like(ys_ref)

    @pl.when(used)
    def _():
        x1 = xs_ref[:, 0:D_MODEL]
        xb = x1.astype(BF16)

        def expert(wg, wu, wd):
            a = _dot(xb, wg[...])
            h = a * _sigmoid(a) * _dot(xb, wu[...])
            return _dot(h.astype(BF16), wd[...])

        ffn = xs_ref[:, D_MODEL + 1:D_MODEL + 2] * expert(wga_ref, wua_ref, wda_ref)
        ffn = ffn + xs_ref[:, D_MODEL + 2:D_MODEL + 3] * expert(wgb_ref, wub_ref, wdb_ref)
        ys_ref[...] = _layer_norm(DN_ALPHA * x1 + ffn, g2_ref[...], b2_ref[...])


def _experts(tile_a, tile_b, nused, xs, w_gate, w_up, w_down, g2, b2, layer):
    tile = lambda w: pl.BlockSpec((TM_E, w), lambda i, ea, eb, nu: (jnp.minimum(i, nu[0] - 1), 0))
    wa = lambda *shape: pl.BlockSpec((None, None) + shape, lambda i, ea, eb, nu: (layer, ea[i], 0, 0))
    wb = lambda *shape: pl.BlockSpec((None, None) + shape, lambda i, ea, eb, nu: (layer, eb[i], 0, 0))
    lay = pl.BlockSpec((None, 1, D_MODEL), lambda i, ea, eb, nu: (layer, 0, 0))
    return pl.pallas_call(
        _experts_kernel,
        grid_spec=pltpu.PrefetchScalarGridSpec(
            num_scalar_prefetch=3,
            grid=(NT_E,),
            in_specs=[tile(XE_W),
                      wa(D_MODEL, D_EXPERT), wa(D_MODEL, D_EXPERT), wa(D_EXPERT, D_MODEL),
                      wb(D_MODEL, D_EXPERT), wb(D_MODEL, D_EXPERT), wb(D_EXPERT, D_MODEL),
                      lay, lay],
            out_specs=pl.BlockSpec((TM_E, D_MODEL), lambda i, ea, eb, nu: (i, 0)),
        ),
        out_shape=jax.ShapeDtypeStruct((N_SORTED, D_MODEL), F32),
        compiler_params=_params("arbitrary"),
        name="experts",
    )(tile_a, tile_b, nused, xs, w_gate, w_up, w_down, w_gate, w_up, w_down, g2, b2)


_PAIR_LO = tuple(a for a, _ in PAIRS)
_PAIR_HI = tuple(b for _, b in PAIRS)


def kernel(x_prompt, x_sample, cache_attn_k, cache_attn_v, state_gla, state_pool, w_in, w_gate, attn_sink,
           gla_w_a2, gla_b_a, gla_norm_g, pool_w, pool_scale, w_branch, w_o, ln1_g, ln1_b, ln2_g, ln2_b,
           w_router, router_bias, w_e_gate, w_e_up, w_e_down):
    ag0 = PA_W + PG_W
    w_in_p = jnp.concatenate(
        [w_in[:, :, :ag0], w_in[:, :, ag0 + GLA_RANK:], w_in[:, :, ag0:ag0 + GLA_RANK],
         jnp.zeros((DEPTH, D_MODEL, LANES - GLA_RANK), w_in.dtype)], axis=-1).astype(BF16)
    w_gate_b, w_branch_b, w_o_b = w_gate.astype(BF16), w_branch.astype(BF16), w_o.astype(BF16)
    wa2_p = jnp.pad(gla_w_a2, ((0, 0), (0, LANES - GLA_RANK), (0, 0))).astype(BF16)
    pool_w_b = pool_w.astype(BF16)
    wr_t = w_router.T.astype(BF16)
    rb = router_bias.reshape(N_EXPERTS, 1).astype(F32)
    weg, weu, wed = w_e_gate.astype(BF16), w_e_up.astype(BF16), w_e_down.astype(BF16)
    cache_k = cache_attn_k.reshape(DEPTH, DEC_BATCH, WINDOW, KV_W)
    cache_v = cache_attn_v.reshape(DEPTH, DEC_BATCH, WINDOW, KV_W)
    pair_lo, pair_hi = jnp.array(_PAIR_LO, I32), jnp.array(_PAIR_HI, I32)

    x = jnp.concatenate([x_prompt.reshape(NP, D_MODEL), x_sample.reshape(NS, D_MODEL)], axis=0)
    xs = jnp.zeros((N_SORTED, XE_W), F32)
    pk, pv, ps, pp, sk, sv, ss, sp = ([] for _ in range(8))
    for l in range(DEPTH):
        pa, pg, pu, pag = _in_proj(x, w_in_p, l)
        row2 = lambda a: a[l].reshape(1, -1)

        oa_p = _attn_prompt(pa, attn_sink[l])
        oa_s, nk, nv = _attn_sample(pa, attn_sink[l], cache_k, cache_v, l)
        og_p, s_p = _gla_prompt(pg, pag, wa2_p[l], row2(gla_b_a), row2(gla_norm_g))
        og_s, s_s = _gla_sample(pg, pag, wa2_p[l], row2(gla_b_a), row2(gla_norm_g), state_gla, l)
        op_p = _pool_prompt(pu, pool_w_b[l], row2(pool_scale))
        u_s = pu[NP:].reshape(DEC_BATCH, DEC_SEQ, POOL_W)
        ext_s = jnp.concatenate(
            [jnp.zeros((DEC_BATCH, POOL_HIST - POOL_STATE, POOL_W), F32), state_pool[l], u_s], axis=1)
        op_s = _pool_sample(ext_s, pool_w_b[l], row2(pool_scale))

        xe, meta = _merge(x, (oa_p, oa_s, og_p, og_s, op_p, op_s), w_gate_b, w_branch_b, w_o_b,
                          ln1_g.reshape(DEPTH, 1, D_MODEL), ln1_b.reshape(DEPTH, 1, D_MODEL), wr_t, rb, l)
        cls = meta[:, 0, :].reshape(N_TOK // LANES, LANES).astype(I32)
        cls2d = jnp.pad(cls, ((0, PLAN_ROWS - N_TOK // LANES), (0, 0)), constant_values=-1)
        pos2d, tcls, nused = _plan(cls2d)
        pos = pos2d.reshape(-1)[:N_TOK].reshape(N_TILES, 1, TM)
        tile_cls = jnp.minimum(tcls[0, :NT_E], N_CLASSES - 1)
        tile_a = (tile_cls // len(PAIRS)) * EXPERTS_PER_GROUP + pair_lo[tile_cls % len(PAIRS)]
        tile_b = (tile_cls // len(PAIRS)) * EXPERTS_PER_GROUP + pair_hi[tile_cls % len(PAIRS)]
        xs = _dispatch(pos, xe, xs)
        ys = _experts(tile_a, tile_b, nused[0, :1], xs, weg, weu, wed, ln2_g.reshape(DEPTH, 1, D_MODEL),
                      ln2_b.reshape(DEPTH, 1, D_MODEL), l)
        x = _combine(pos, ys)

        kp = pa[:NP].reshape(BATCH, SEQ, PA_W)[:, SEQ - WINDOW:, :]
        pk.append(kp[:, :, ATTN_W:ATTN_W + KV_W].reshape(BATCH, WINDOW, N_KV_HEADS, HEAD_DIM))
        pv.append(kp[:, :, ATTN_W + KV_W:].reshape(BATCH, WINDOW, N_KV_HEADS, HEAD_DIM))
        ps.append(s_p)
        pp.append(pu[:NP].reshape(BATCH, SEQ, POOL_W)[:, SEQ - POOL_STATE:, :])
        sk.append(nk.reshape(DEC_BATCH, WINDOW, N_KV_HEADS, HEAD_DIM))
        sv.append(nv.reshape(DEC_BATCH, WINDOW, N_KV_HEADS, HEAD_DIM))
        ss.append(s_s)
        sp.append(ext_s[:, POOL_HIST + DEC_SEQ - POOL_STATE:, :])
    return (x[:NP].reshape(BATCH, SEQ, D_MODEL), x[NP:].reshape(DEC_BATCH, DEC_SEQ, D_MODEL),
            jnp.stack(pk), jnp.stack(pv), jnp.stack(ps), jnp.stack(pp),
            jnp.stack(sk), jnp.stack(sv), jnp.stack(ss), jnp.stack(sp))
```

```python
import jax
import jax.numpy as jnp
import numpy as np
from jax import lax
from jax.experimental import pallas as pl
from jax.experimental.pallas import tpu as pltpu

F32, BF16, I32 = jnp.float32, jnp.bfloat16, jnp.int32

D_MODEL = 1024
BATCH = 8
SEQ = 2048
DEPTH = 4
DEC_BATCH = 128
DEC_SEQ = 8
N_HEADS = 8
N_KV_HEADS = 2
HEAD_DIM = 64
WINDOW = 128
ATTN_W = N_HEADS * HEAD_DIM
KV_W = N_KV_HEADS * HEAD_DIM
Q_PER_KV = N_HEADS // N_KV_HEADS
GLA_HEADS = 4
GLA_DK = 64
GLA_DV = 128
GLA_KW = GLA_HEADS * GLA_DK
GLA_VW = GLA_HEADS * GLA_DV
GLA_RANK = 16
GLA_TAU = 16.0
GLA_CHUNK = 64
POOL_WINDOWS = (2, 4, 8, 16)
POOL_GROUPS = 4
POOL_GW = 128
POOL_W = POOL_GROUPS * POOL_GW
POOL_STATE = max(POOL_WINDOWS) - 1
N_BRANCH = 3
BRANCH_W = 512
N_EXPERTS = 16
N_GROUPS = 4
EXPERTS_PER_GROUP = N_EXPERTS // N_GROUPS
D_EXPERT = 512
DN_ALPHA = (2.0 * DEPTH) ** 0.25
LN_EPS = 1e-5
RMS_EPS = 1e-6
NEG_INF = -1e30
ALIBI_SLOPES = tuple(2.0 ** (-8.0 * h / N_HEADS) for h in range(1, N_HEADS + 1))

LANES = 128
SUBLANES = 8

NP = BATCH * SEQ
NS = DEC_BATCH * DEC_SEQ
N_TOK = NP + NS
TM = 512
N_TILES = N_TOK // TM
PT = NP // TM

PA_W = ATTN_W + 2 * KV_W
PG_W = 2 * GLA_KW + 2 * GLA_VW
IN_W_PAD = PA_W + PG_W + POOL_W + LANES

PAIRS = ((0, 1), (0, 2), (0, 3), (1, 2), (1, 3), (2, 3))
N_CLASSES = N_GROUPS * len(PAIRS)
TM_E = 256
NT_E = -(-(N_TOK + N_CLASSES * (TM_E - 1)) // TM_E)
N_SORTED = NT_E * TM_E
XE_W = D_MODEL + LANES
PLAN_ROWS = 256

VMEM_LIMIT = 56 * 1024 * 1024


def _params(*sem):
    return pltpu.CompilerParams(dimension_semantics=sem, vmem_limit_bytes=VMEM_LIMIT)


def _dot(a, b):
    return jnp.dot(a, b, preferred_element_type=F32)


def _dot_nt(a, b):
    return lax.dot_general(a, b, (((1,), (1,)), ((), ())), preferred_element_type=F32)


def _dot_tn(a, b):
    return lax.dot_general(a, b, (((0,), (0,)), ((), ())), preferred_element_type=F32)


def _bdot_nt(a, b):
    return lax.dot_general(a, b, (((2,), (2,)), ((0,), (0,))), preferred_element_type=F32)


def _bdot(a, b):
    return lax.dot_general(a, b, (((2,), (1,)), ((0,), (0,))), preferred_element_type=F32)


def _layer_norm(h, g, b):
    mu = jnp.mean(h, axis=-1, keepdims=True)
    hc = h - mu
    var = jnp.mean(hc * hc, axis=-1, keepdims=True)
    return hc * lax.rsqrt(var + LN_EPS) * g + b


def _sigmoid(x):
    return 1.0 / (1.0 + jnp.exp(-x))


def _log_sigmoid(x):
    return jnp.minimum(x, 0.0) - jnp.log1p(jnp.exp(-jnp.abs(x)))


def _pair_specs(width):
    return (pl.BlockSpec((TM, width), lambda i, *_: (jnp.minimum(i, PT - 1), 0)),
            pl.BlockSpec((TM, width), lambda i, *_: (jnp.maximum(i - PT, 0), 0)))


def _pair_tile(p_ref, s_ref):
    return jnp.where(pl.program_id(0) < PT, p_ref[...], s_ref[...])


def _in_proj_kernel(xp_ref, xs_ref, w_ref, pa_ref, pg_ref, pu_ref, pag_ref):
    xb = _pair_tile(xp_ref, xs_ref).astype(BF16)
    pa_ref[...] = _dot(xb, w_ref[:, 0:PA_W])
    pg_ref[...] = _dot(xb, w_ref[:, PA_W:PA_W + PG_W])
    pu_ref[...] = _dot(xb, w_ref[:, PA_W + PG_W:PA_W + PG_W + POOL_W])
    pag_ref[...] = _dot(xb, w_ref[:, PA_W + PG_W + POOL_W:IN_W_PAD])


def _in_proj(x_pair, w_in_p, layer):
    row = lambda w: pl.BlockSpec((TM, w), lambda i: (i, 0))
    return pl.pallas_call(
        _in_proj_kernel,
        grid=(N_TILES,),
        in_specs=[*_pair_specs(D_MODEL), pl.BlockSpec((None, D_MODEL, IN_W_PAD), lambda i: (layer, 0, 0))],
        out_specs=[row(PA_W), row(PG_W), row(POOL_W), row(LANES)],
        out_shape=[jax.ShapeDtypeStruct((N_TOK, w), F32) for w in (PA_W, PG_W, POOL_W, LANES)],
        compiler_params=_params("arbitrary"),
        name="in_proj",
    )(*x_pair, w_in_p)


def _softmax_sink_pv(parts, sink, pv):
    m = sink
    for s, _ in parts:
        m = jnp.maximum(m, jnp.max(s, axis=-1, keepdims=True))
    den = jnp.exp(sink - m)
    es = []
    for s, _ in parts:
        e = jnp.exp(s - m)
        den = den + jnp.sum(e, axis=-1, keepdims=True)
        es.append(e)
    inv = 1.0 / den
    out = None
    for e, (_, v) in zip(es, parts):
        o = pv((e * inv).astype(BF16), v)
        out = o if out is None else out + o
    return out


def _head_column(sink_ref, kv, rows_per_head):
    g = lax.broadcasted_iota(I32, (Q_PER_KV * rows_per_head, 1), 0) // rows_per_head
    col = jnp.zeros(g.shape, F32)
    for i in range(Q_PER_KV):
        col = jnp.where(g == i, sink_ref[kv * Q_PER_KV + i], col)
    return col


def _alibi_bias(dist, visible):
    out = np.empty((N_KV_HEADS, Q_PER_KV * dist.shape[0], dist.shape[1]), np.float32)
    for h in range(N_HEADS):
        kv, g = divmod(h, Q_PER_KV)
        out[kv, g * dist.shape[0]:(g + 1) * dist.shape[0]] = np.where(visible, -ALIBI_SLOPES[h] * dist, NEG_INF)
    return out


def _attn_prompt_kernel(sink_ref, cur_ref, prev_ref, bias_ref, o_ref, nk_ref, nv_ref):
    k = jnp.concatenate([prev_ref[:, 0:KV_W], cur_ref[:, ATTN_W:ATTN_W + KV_W]], axis=0).astype(BF16)
    v = jnp.concatenate([prev_ref[:, KV_W:2 * KV_W], cur_ref[:, ATTN_W + KV_W:PA_W]], axis=0).astype(BF16)
    for kv in range(N_KV_HEADS):
        ks = slice(kv * HEAD_DIM, (kv + 1) * HEAD_DIM)
        heads = range(kv * Q_PER_KV, (kv + 1) * Q_PER_KV)
        q = jnp.concatenate([cur_ref[:, h * HEAD_DIM:(h + 1) * HEAD_DIM] for h in heads], axis=0)
        s = _dot_nt((q * (HEAD_DIM ** -0.5)).astype(BF16), k[:, ks]) + bias_ref[kv]
        o = _softmax_sink_pv([(s, v[:, ks])], _head_column(sink_ref, kv, WINDOW), _dot)
        for g, h in enumerate(heads):
            o_ref[:, h * HEAD_DIM:(h + 1) * HEAD_DIM] = o[g * WINDOW:(g + 1) * WINDOW].astype(BF16)

    @pl.when(pl.program_id(1) == pl.num_programs(1) - 1)
    def _():
        nk_ref[...] = cur_ref[:, ATTN_W:ATTN_W + KV_W]
        nv_ref[...] = cur_ref[:, ATTN_W + KV_W:PA_W]


def _attn_prompt(pa, sink):
    nb = SEQ // WINDOW
    r = np.arange(WINDOW)[:, None]
    c = np.arange(2 * WINDOW)[None, :]
    band = (c > r) & (c <= WINDOW + r)
    bias = jnp.asarray(np.stack([_alibi_bias(WINDOW + r - c, band & (c >= WINDOW)), _alibi_bias(WINDOW + r - c, band)]))
    state = pl.BlockSpec((None, WINDOW, KV_W), lambda b, i, s: (b, 0, 0))
    return pl.pallas_call(
        _attn_prompt_kernel,
        grid_spec=pltpu.PrefetchScalarGridSpec(
            num_scalar_prefetch=1,
            grid=(BATCH, nb),
            in_specs=[
                pl.BlockSpec((WINDOW, PA_W), lambda b, i, s: (b * nb + i, 0)),
                pl.BlockSpec((WINDOW, 2 * KV_W), lambda b, i, s: (b * nb + jnp.maximum(i - 1, 0), ATTN_W // (2 * KV_W))),
                pl.BlockSpec((None, N_KV_HEADS, Q_PER_KV * WINDOW, 2 * WINDOW), lambda b, i, s: (jnp.minimum(i, 1), 0, 0, 0)),
            ],
            out_specs=[pl.BlockSpec((WINDOW, ATTN_W), lambda b, i, s: (b * nb + i, 0)), state, state],
        ),
        out_shape=[jax.ShapeDtypeStruct((NP, ATTN_W), BF16),
                   jax.ShapeDtypeStruct((BATCH, WINDOW, KV_W), F32),
                   jax.ShapeDtypeStruct((BATCH, WINDOW, KV_W), F32)],
        compiler_params=_params("arbitrary", "arbitrary"),
        name="attn_prompt",
    )(sink, pa, pa, bias)


ATTN_SB = 16


def _attn_sample_kernel(sink_ref, cur_ref, kc_ref, vc_ref, bias_c_ref, bias_n_ref, o_ref, nk_ref, nv_ref):
    cur = cur_ref[...].reshape(ATTN_SB, DEC_SEQ, PA_W)
    for kv in range(N_KV_HEADS):
        ks = slice(kv * HEAD_DIM, (kv + 1) * HEAD_DIM)
        heads = range(kv * Q_PER_KV, (kv + 1) * Q_PER_KV)
        q = jnp.concatenate([cur[:, :, h * HEAD_DIM:(h + 1) * HEAD_DIM] for h in heads], axis=1)
        q = (q * (HEAD_DIM ** -0.5)).astype(BF16)
        kn = cur[:, :, ATTN_W + kv * HEAD_DIM:ATTN_W + (kv + 1) * HEAD_DIM].astype(BF16)
        vn = cur[:, :, ATTN_W + KV_W + kv * HEAD_DIM:ATTN_W + KV_W + (kv + 1) * HEAD_DIM].astype(BF16)
        sc = _bdot_nt(q, kc_ref[:, :, ks].astype(BF16)) + bias_c_ref[kv]
        sn = _bdot_nt(q, kn) + bias_n_ref[kv]
        sink = _head_column(sink_ref, kv, DEC_SEQ)
        o = _softmax_sink_pv([(sc, vc_ref[:, :, ks].astype(BF16)), (sn, vn)], sink, _bdot)
        for g, h in enumerate(heads):
            o_ref[:, h * HEAD_DIM:(h + 1) * HEAD_DIM] = (
                o[:, g * DEC_SEQ:(g + 1) * DEC_SEQ, :].reshape(ATTN_SB * DEC_SEQ, HEAD_DIM).astype(BF16))
    keep = WINDOW - DEC_SEQ
    nk_ref[:, 0:keep, :] = kc_ref[:, DEC_SEQ:WINDOW, :]
    nk_ref[:, keep:WINDOW, :] = cur[:, :, ATTN_W:ATTN_W + KV_W]
    nv_ref[:, 0:keep, :] = vc_ref[:, DEC_SEQ:WINDOW, :]
    nv_ref[:, keep:WINDOW, :] = cur[:, :, ATTN_W + KV_W:PA_W]


def _attn_sample(pa, sink, cache_k, cache_v, layer):
    rows = ATTN_SB * DEC_SEQ
    first = NP // rows
    t = np.arange(DEC_SEQ)[:, None]
    jc = np.arange(WINDOW)[None, :]
    jn = np.arange(DEC_SEQ)[None, :]
    bias_c = jnp.asarray(_alibi_bias(WINDOW + t - jc, jc > t))
    bias_n = jnp.asarray(_alibi_bias(t - jn, jn <= t))
    cache = pl.BlockSpec((None, ATTN_SB, WINDOW, KV_W), lambda j, s: (layer, j, 0, 0))
    new = pl.BlockSpec((ATTN_SB, WINDOW, KV_W), lambda j, s: (j, 0, 0))
    const = lambda a: pl.BlockSpec(a.shape, lambda j, s: (0,) * a.ndim)
    return pl.pallas_call(
        _attn_sample_kernel,
        grid_spec=pltpu.PrefetchScalarGridSpec(
            num_scalar_prefetch=1,
            grid=(DEC_BATCH // ATTN_SB,),
            in_specs=[pl.BlockSpec((rows, PA_W), lambda j, s: (first + j, 0)), cache, cache,
                      const(bias_c), const(bias_n)],
            out_specs=[pl.BlockSpec((rows, ATTN_W), lambda j, s: (j, 0)), new, new],
        ),
        out_shape=[jax.ShapeDtypeStruct((NS, ATTN_W), BF16),
                   jax.ShapeDtypeStruct((DEC_BATCH, WINDOW, KV_W), F32),
                   jax.ShapeDtypeStruct((DEC_BATCH, WINDOW, KV_W), F32)],
        compiler_params=_params("arbitrary"),
        name="attn_sample",
    )(sink, pa, cache_k, cache_v, bias_c, bias_n)


def _chunk_cumsum(x, chunk):
    pos = lax.broadcasted_iota(I32, x.shape, 0) % chunk
    sh = 1
    while sh < chunk:
        x = x + jnp.where(pos >= sh, pltpu.roll(x, sh, 0), 0.0)
        sh *= 2
    return x


def _gla_log_decay(pag_ref, wa2_ref, ba_ref):
    z = _dot(pag_ref[...].astype(BF16), wa2_ref[...]) + ba_ref[...]
    return _log_sigmoid(z) / GLA_TAU


def _gla_finish(o, rg, g):
    o = o * lax.rsqrt(jnp.mean(o * o, axis=-1, keepdims=True) + RMS_EPS)
    return (o * g * (rg * _sigmoid(rg))).astype(BF16)


GLA_TT = 512
GLA_AG = 128


def _gla_prompt_kernel(pg_ref, pag_ref, wa2_ref, ba_ref, g_ref, o_ref, s_ref,
                       qd_ref, kd_ref, kdec_ref, a_ref, sall_ref, st_ref, acc_ref):
    nc = GLA_TT // GLA_CHUNK

    @pl.when(pl.program_id(1) == 0)
    def _():
        st_ref[...] = jnp.zeros_like(st_ref)

    cum = _chunk_cumsum(_gla_log_decay(pag_ref, wa2_ref, ba_ref), GLA_CHUNK)
    cum3 = cum.reshape(nc, GLA_CHUNK, GLA_KW)
    tot3 = cum3[:, GLA_CHUNK - 1:GLA_CHUNK, :]
    k = pg_ref[:, GLA_KW:2 * GLA_KW]
    qd_ref[...] = (pg_ref[:, 0:GLA_KW] * (GLA_DK ** -0.5) * jnp.exp(cum)).astype(BF16)
    kd_ref[...] = (k * jnp.exp(-cum)).astype(BF16)
    kdec_ref[...] = (k * jnp.exp(tot3 - cum3).reshape(GLA_TT, GLA_KW)).astype(BF16)
    etot = jnp.exp(tot3)

    def hs(h):
        return slice(h * GLA_DK, (h + 1) * GLA_DK)

    def vs(h):
        return slice(2 * GLA_KW + h * GLA_DV, 2 * GLA_KW + (h + 1) * GLA_DV)

    for c in range(nc):
        rows = slice(c * GLA_CHUNK, (c + 1) * GLA_CHUNK)
        for h in range(GLA_HEADS):
            a_ref[c, :, hs(h)] = _dot_tn(pg_ref[rows, vs(h)].astype(BF16), kdec_ref[rows, hs(h)])

    st = st_ref[...]
    for c in range(nc):
        sall_ref[c] = st.astype(BF16)
        st = etot[c] * st + a_ref[c]
    st_ref[...] = st

    r = lax.broadcasted_iota(I32, (GLA_AG, GLA_AG), 0)
    col = lax.broadcasted_iota(I32, (GLA_AG, GLA_AG), 1)
    causal = (r // GLA_CHUNK == col // GLA_CHUNK) & (col <= r)
    for h in range(GLA_HEADS):
        out = slice(h * GLA_DV, (h + 1) * GLA_DV)
        for c in range(nc):
            rows = slice(c * GLA_CHUNK, (c + 1) * GLA_CHUNK)
            acc_ref[rows, out] = _dot_nt(qd_ref[rows, hs(h)], sall_ref[c, :, hs(h)])
        for a in range(GLA_TT // GLA_AG):
            rows = slice(a * GLA_AG, (a + 1) * GLA_AG)
            att = jnp.where(causal, _dot_nt(qd_ref[rows, hs(h)], kd_ref[rows, hs(h)]), 0.0)
            acc_ref[rows, out] += _dot(att.astype(BF16), pg_ref[rows, vs(h)].astype(BF16))
        rg = pg_ref[:, 2 * GLA_KW + GLA_VW + h * GLA_DV:2 * GLA_KW + GLA_VW + (h + 1) * GLA_DV]
        o_ref[:, out] = _gla_finish(acc_ref[:, out], rg, g_ref[:, out])

    @pl.when(pl.program_id(1) == pl.num_programs(1) - 1)
    def _():
        for h in range(GLA_HEADS):
            s_ref[h] = st_ref[:, hs(h)].T


def _gla_prompt(pg, pag, wa2, ba, g):
    nt = SEQ // GLA_TT
    nc = GLA_TT // GLA_CHUNK
    const = lambda shape: pl.BlockSpec(shape, lambda b, j: (0,) * len(shape))
    return pl.pallas_call(
        _gla_prompt_kernel,
        grid=(BATCH, nt),
        in_specs=[
            pl.BlockSpec((GLA_TT, PG_W), lambda b, j: (b * nt + j, 0)),
            pl.BlockSpec((GLA_TT, LANES), lambda b, j: (b * nt + j, 0)),
            const((LANES, GLA_KW)), const((1, GLA_KW)), const((1, GLA_VW)),
        ],
        out_specs=[
            pl.BlockSpec((GLA_TT, GLA_VW), lambda b, j: (b * nt + j, 0)),
            pl.BlockSpec((None, GLA_HEADS, GLA_DK, GLA_DV), lambda b, j: (b, 0, 0, 0)),
        ],
        out_shape=[jax.ShapeDtypeStruct((NP, GLA_VW), BF16),
                   jax.ShapeDtypeStruct((BATCH, GLA_HEADS, GLA_DK, GLA_DV), F32)],
        scratch_shapes=[pltpu.VMEM((GLA_TT, GLA_KW), BF16), pltpu.VMEM((GLA_TT, GLA_KW), BF16),
                        pltpu.VMEM((GLA_TT, GLA_KW), BF16),
                        pltpu.VMEM((nc, GLA_DV, GLA_KW), F32), pltpu.VMEM((nc, GLA_DV, GLA_KW), BF16),
                        pltpu.VMEM((GLA_DV, GLA_KW), F32), pltpu.VMEM((GLA_TT, GLA_VW), F32)],
        compiler_params=_params("arbitrary", "arbitrary"),
        name="gla_prompt",
    )(pg, pag, wa2, ba, g)


GLA_SB = 8


def _gla_sample_kernel(pg_ref, pag_ref, wa2_ref, ba_ref, g_ref, s0_ref, o_ref, s_ref):
    cum_all = _chunk_cumsum(_gla_log_decay(pag_ref, wa2_ref, ba_ref), DEC_SEQ)
    tri = (lax.broadcasted_iota(I32, (DEC_SEQ, DEC_SEQ), 0) >= lax.broadcasted_iota(I32, (DEC_SEQ, DEC_SEQ), 1))
    for s in range(GLA_SB):
        rows = slice(s * DEC_SEQ, (s + 1) * DEC_SEQ)
        cum = cum_all[rows, :]
        tot = cum[DEC_SEQ - 1:DEC_SEQ, :]
        q = pg_ref[rows, 0:GLA_KW] * (GLA_DK ** -0.5)
        k = pg_ref[rows, GLA_KW:2 * GLA_KW]
        qd = (q * jnp.exp(cum)).astype(BF16)
        kd = (k * jnp.exp(-cum)).astype(BF16)
        kdec = (k * jnp.exp(tot - cum)).astype(BF16)
        etot = jnp.exp(tot)
        etot_col = [jnp.broadcast_to(etot[:, p * LANES:(p + 1) * LANES], (SUBLANES, LANES)).T[:, 0:1]
                    for p in range(GLA_KW // LANES)]
        for h in range(GLA_HEADS):
            ks = slice(h * GLA_DK, (h + 1) * GLA_DK)
            vs = slice(2 * GLA_KW + h * GLA_DV, 2 * GLA_KW + (h + 1) * GLA_DV)
            rs = slice(2 * GLA_KW + GLA_VW + h * GLA_DV, 2 * GLA_KW + GLA_VW + (h + 1) * GLA_DV)
            v = pg_ref[rows, vs].astype(BF16)
            st = s0_ref[s, h]
            att = jnp.where(tri, _dot_nt(qd[:, ks], kd[:, ks]), 0.0)
            o = _dot(qd[:, ks], st.astype(BF16)) + _dot(att.astype(BF16), v)
            per = LANES // GLA_DK
            col = etot_col[h // per][(h % per) * GLA_DK:(h % per + 1) * GLA_DK, :]
            s_ref[s, h] = col * st + _dot_tn(kdec[:, ks], v)
            o_ref[rows, h * GLA_DV:(h + 1) * GLA_DV] = _gla_finish(o, pg_ref[rows, rs], g_ref[:, h * GLA_DV:(h + 1) * GLA_DV])


def _gla_sample(pg, pag, wa2, ba, g, state, layer):
    rows = GLA_SB * DEC_SEQ
    first = NP // rows
    const = lambda shape: pl.BlockSpec(shape, lambda j: (0,) * len(shape))
    return pl.pallas_call(
        _gla_sample_kernel,
        grid=(DEC_BATCH // GLA_SB,),
        in_specs=[
            pl.BlockSpec((rows, PG_W), lambda j: (first + j, 0)),
            pl.BlockSpec((rows, LANES), lambda j: (first + j, 0)),
            const((LANES, GLA_KW)), const((1, GLA_KW)), const((1, GLA_VW)),
            pl.BlockSpec((None, GLA_SB, GLA_HEADS, GLA_DK, GLA_DV), lambda j: (layer, j, 0, 0, 0)),
        ],
        out_specs=[
            pl.BlockSpec((rows, GLA_VW), lambda j: (j, 0)),
            pl.BlockSpec((GLA_SB, GLA_HEADS, GLA_DK, GLA_DV), lambda j: (j, 0, 0, 0)),
        ],
        out_shape=[jax.ShapeDtypeStruct((NS, GLA_VW), BF16),
                   jax.ShapeDtypeStruct((DEC_BATCH, GLA_HEADS, GLA_DK, GLA_DV), F32)],
        compiler_params=_params("arbitrary"),
        name="gla_sample",
    )(pg, pag, wa2, ba, g, state)


POOL_HIST = 16


def _pool_groups(ext, cnt, pw_ref, ps_ref, out_rows):
    ax = ext.ndim - 2
    outs = []
    for g, w in enumerate(POOL_WINDOWS):
        x = ext[..., g * POOL_GW:(g + 1) * POOL_GW]
        s, sh = x, 1
        while sh < w:
            s = s + pltpu.roll(s, sh, ax)
            sh *= 2
        if ext.ndim == 3:
            d = (s[:, POOL_HIST:, :] / cnt[g] - x[:, POOL_HIST:, :]).reshape(out_rows, POOL_GW)
        else:
            d = s[POOL_HIST:, :] / cnt[g] - x[POOL_HIST:, :]
        y = _dot(d.astype(BF16), pw_ref[g]) * ps_ref[:, g * POOL_GW:(g + 1) * POOL_GW]
        outs.append(y.astype(BF16))
    return outs


POOL_TT = 512


def _pool_prompt_kernel(pu_ref, pw_ref, ps_ref, o_ref, last_ref, hist_ref):
    j = pl.program_id(1)

    @pl.when(j == 0)
    def _():
        hist_ref[...] = jnp.zeros_like(hist_ref)

    u = pu_ref[...]
    ext = jnp.concatenate([hist_ref[...], u], axis=0)
    pos = j * POOL_TT + lax.broadcasted_iota(I32, (POOL_TT, 1), 0)
    cnt = [jnp.minimum(pos + 1, w).astype(F32) for w in POOL_WINDOWS]
    for g, y in enumerate(_pool_groups(ext, cnt, pw_ref, ps_ref, POOL_TT)):
        o_ref[:, g * POOL_GW:(g + 1) * POOL_GW] = y
    hist_ref[...] = u[POOL_TT - POOL_HIST:, :]

    @pl.when(j == pl.num_programs(1) - 1)
    def _():
        last_ref[...] = u[POOL_TT - POOL_HIST:, :]


def _pool_prompt(pu, pw, ps):
    nt = SEQ // POOL_TT
    return pl.pallas_call(
        _pool_prompt_kernel,
        grid=(BATCH, nt),
        in_specs=[
            pl.BlockSpec((POOL_TT, POOL_W), lambda b, j: (b * nt + j, 0)),
            pl.BlockSpec((POOL_GROUPS, POOL_GW, POOL_GW), lambda b, j: (0, 0, 0)),
            pl.BlockSpec((1, POOL_W), lambda b, j: (0, 0)),
        ],
        out_specs=[pl.BlockSpec((POOL_TT, POOL_W), lambda b, j: (b * nt + j, 0)),
                   pl.BlockSpec((None, POOL_HIST, POOL_W), lambda b, j: (b, 0, 0))],
        out_shape=[jax.ShapeDtypeStruct((NP, POOL_W), BF16),
                   jax.ShapeDtypeStruct((BATCH, POOL_HIST, POOL_W), F32)],
        scratch_shapes=[pltpu.VMEM((POOL_HIST, POOL_W), F32)],
        compiler_params=_params("arbitrary", "arbitrary"),
        name="pool_prompt",
    )(pu, pw, ps)


POOL_SB = 16


def _pool_sample_kernel(ext_ref, pw_ref, ps_ref, o_ref):
    cnt = [float(w) for w in POOL_WINDOWS]
    for g, y in enumerate(_pool_groups(ext_ref[...], cnt, pw_ref, ps_ref, POOL_SB * DEC_SEQ)):
        o_ref[:, g * POOL_GW:(g + 1) * POOL_GW] = y


def _pool_sample(ext, pw, ps):
    rows = POOL_SB * DEC_SEQ
    return pl.pallas_call(
        _pool_sample_kernel,
        grid=(DEC_BATCH // POOL_SB,),
        in_specs=[
            pl.BlockSpec((POOL_SB, POOL_HIST + DEC_SEQ, POOL_W), lambda j: (j, 0, 0)),
            pl.BlockSpec((POOL_GROUPS, POOL_GW, POOL_GW), lambda j: (0, 0, 0)),
            pl.BlockSpec((1, POOL_W), lambda j: (0, 0)),
        ],
        out_specs=pl.BlockSpec((rows, POOL_W), lambda j: (j, 0)),
        out_shape=jax.ShapeDtypeStruct((NS, POOL_W), BF16),
        compiler_params=_params("arbitrary"),
        name="pool_sample",
    )(ext, pw, ps)


def _route(sc, sel):
    gscore = []
    for g in range(N_GROUPS):
        v = sel[EXPERTS_PER_GROUP * g:EXPERTS_PER_GROUP * (g + 1)]
        best = None
        for a, b in PAIRS:
            pair = v[a] + v[b]
            best = pair if best is None else jnp.maximum(best, pair)
        gscore.append(best)
    gi = jnp.zeros_like(gscore[0], dtype=I32)
    best = gscore[0]
    for g in range(1, N_GROUPS):
        upd = gscore[g] > best
        gi = jnp.where(upd, g, gi)
        best = jnp.where(upd, gscore[g], best)

    def in_group(rows, j):
        out = rows[(N_GROUPS - 1) * EXPERTS_PER_GROUP + j]
        for g in range(N_GROUPS - 2, -1, -1):
            out = jnp.where(gi == g, rows[g * EXPERTS_PER_GROUP + j], out)
        return out

    u = [in_group(sel, j) for j in range(EXPERTS_PER_GROUP)]
    s_in = [in_group(sc, j) for j in range(EXPERTS_PER_GROUP)]

    def argmax4(vals):
        idx = jnp.zeros_like(gi)
        m = vals[0]
        for j in range(1, EXPERTS_PER_GROUP):
            upd = vals[j] > m
            idx = jnp.where(upd, j, idx)
            m = jnp.where(upd, vals[j], m)
        return idx

    def pick(vals, idx):
        out = vals[EXPERTS_PER_GROUP - 1]
        for j in range(EXPERTS_PER_GROUP - 2, -1, -1):
            out = jnp.where(idx == j, vals[j], out)
        return out

    i1 = argmax4(u)
    i2 = argmax4([jnp.where(i1 == j, NEG_INF, u[j]) for j in range(EXPERTS_PER_GROUP)])
    w1, w2 = pick(s_in, i1), pick(s_in, i2)
    tot = w1 + w2
    w1, w2 = w1 / tot, w2 / tot
    lo, hi = jnp.minimum(i1, i2), jnp.maximum(i1, i2)
    first_lo = i1 < i2
    w_lo, w_hi = jnp.where(first_lo, w1, w2), jnp.where(first_lo, w2, w1)
    pair = jnp.where(lo == 0, hi - 1, jnp.where(lo == 1, hi + 1, 5))
    return gi * len(PAIRS) + pair, w_lo, w_hi


def _merge_kernel(xp_ref, xs_ref, oap_ref, oas_ref, ogp_ref, ogs_ref, opp_ref, ops_ref, wg_ref, wb_ref, wo_ref,
                  g1_ref, b1_ref, wr_ref, rb_ref, xe_ref, meta_ref):
    x = _pair_tile(xp_ref, xs_ref)
    xb = x.astype(BF16)
    merged = None
    for n, (brp, brs) in enumerate(((oap_ref, oas_ref), (ogp_ref, ogs_ref), (opp_ref, ops_ref))):
        gate = _sigmoid(_dot(xb, wg_ref[:, n * D_MODEL:(n + 1) * D_MODEL]))
        term = gate * _dot(_pair_tile(brp, brs), wb_ref[n])
        merged = term if merged is None else merged + term
    mix = _dot(merged.astype(BF16), wo_ref[...])
    x1 = _layer_norm(DN_ALPHA * x + mix, g1_ref[...], b1_ref[...])
    xe_ref[:, 0:D_MODEL] = x1
    sc_t = _sigmoid(_dot_nt(wr_ref[...], x1.astype(BF16)))
    sel_t = sc_t + rb_ref[...]
    sc = [sc_t[e:e + 1, :] for e in range(N_EXPERTS)]
    sel = [sel_t[e:e + 1, :] for e in range(N_EXPERTS)]
    cls, w_lo, w_hi = _route(sc, sel)
    cls = cls.astype(F32)

    def rows(n):
        rid = lax.broadcasted_iota(I32, (n, TM), 0)
        return jnp.where(rid == 0, cls, jnp.where(rid == 1, w_lo, jnp.where(rid == 2, w_hi, 0.0)))

    meta_ref[...] = rows(SUBLANES)
    xe_ref[:, D_MODEL:XE_W] = rows(LANES).T


def _merge(x_pair, branches, wg, wb, wo, g1, b1, wr_t, rb, layer):
    row = lambda w: pl.BlockSpec((TM, w), lambda i: (i, 0))
    lay = lambda *shape: pl.BlockSpec((None,) + shape, lambda i: (layer,) + (0,) * len(shape))
    const = lambda *shape: pl.BlockSpec(shape, lambda i: (0,) * len(shape))
    return pl.pallas_call(
        _merge_kernel,
        grid=(N_TILES,),
        in_specs=[*_pair_specs(D_MODEL), *(_pair_specs(BRANCH_W) * N_BRANCH),
                  lay(D_MODEL, N_BRANCH * D_MODEL), lay(N_BRANCH, BRANCH_W, D_MODEL), lay(D_MODEL, D_MODEL),
                  lay(1, D_MODEL), lay(1, D_MODEL), const(N_EXPERTS, D_MODEL), const(N_EXPERTS, 1)],
        out_specs=[row(XE_W), pl.BlockSpec((None, SUBLANES, TM), lambda i: (i, 0, 0))],
        out_shape=[jax.ShapeDtypeStruct((N_TOK, XE_W), F32), jax.ShapeDtypeStruct((N_TILES, SUBLANES, TM), F32)],
        compiler_params=_params("arbitrary"),
        name="merge",
    )(*x_pair, *branches, wg, wb, wo, g1, b1, wr_t, rb)


def _plan_kernel(cls_ref, pos_ref, tcls_ref, nused_ref):
    cls = cls_ref[...]
    lane_r = lax.broadcasted_iota(I32, (LANES, 2 * LANES), 0)
    lane_c = lax.broadcasted_iota(I32, (LANES, 2 * LANES), 1)
    lane_mat = ((lane_c >= LANES) | (lane_r < lane_c)).astype(BF16)
    row_r = lax.broadcasted_iota(I32, (2 * PLAN_ROWS, PLAN_ROWS), 0)
    row_c = lax.broadcasted_iota(I32, (2 * PLAN_ROWS, PLAN_ROWS), 1)
    row_mat = ((row_r >= PLAN_ROWS) | (row_c < row_r)).astype(BF16)
    tile_start = (lax.broadcasted_iota(I32, (SUBLANES, LANES), 1) * TM_E).astype(F32)
    pos = jnp.zeros((PLAN_ROWS, LANES), F32)
    off = jnp.zeros((PLAN_ROWS, LANES), F32)
    tcls = jnp.zeros((SUBLANES, LANES), I32)
    for c in range(N_CLASSES):
        m = cls == c
        lanes = _dot(m.astype(BF16), lane_mat)
        rows = _dot(row_mat, lanes[:, LANES:].astype(BF16))
        rank = lanes[:, 0:LANES] + rows[0:PLAN_ROWS]
        count = rows[PLAN_ROWS:]
        pos = jnp.where(m, off + rank, pos)
        off = off + jnp.ceil(count * (1.0 / TM_E)) * TM_E
        tcls = tcls + (off[0:SUBLANES] <= tile_start).astype(I32)
    pos_ref[...] = pos.astype(I32)
    tcls_ref[...] = tcls
    nused_ref[...] = (off[0:SUBLANES] * (1.0 / TM_E)).astype(I32)


def _plan(cls2d):
    return pl.pallas_call(
        _plan_kernel,
        out_shape=[jax.ShapeDtypeStruct((PLAN_ROWS, LANES), I32),
                   jax.ShapeDtypeStruct((SUBLANES, LANES), I32),
                   jax.ShapeDtypeStruct((SUBLANES, LANES), I32)],
        compiler_params=pltpu.CompilerParams(vmem_limit_bytes=VMEM_LIMIT),
        name="plan",
    )(cls2d)


def _row_copies(pos_ref, tile_ref, sorted_hbm, sem, scatter):
    def copy(r):
        row, srt = tile_ref.at[pl.ds(r, 1)], sorted_hbm.at[pl.ds(pos_ref[0, r], 1)]
        return pltpu.make_async_copy(row, srt, sem) if scatter else pltpu.make_async_copy(srt, row, sem)

    def start(r, carry):
        copy(r).start()
        return carry

    def wait(r, carry):
        copy(r).wait()
        return carry

    lax.fori_loop(0, TM, start, 0, unroll=8)
    lax.fori_loop(0, TM, wait, 0, unroll=8)


def _dispatch_kernel(pos_ref, x_ref, xs_in_hbm, xs_hbm, sem):
    del xs_in_hbm
    _row_copies(pos_ref, x_ref, xs_hbm, sem, scatter=True)


def _combine_kernel(pos_ref, ys_hbm, xp_ref, xs_ref, sem):
    @pl.when(pl.program_id(0) < PT)
    def _():
        _row_copies(pos_ref, xp_ref, ys_hbm, sem, scatter=False)

    @pl.when(pl.program_id(0) >= PT)
    def _():
        _row_copies(pos_ref, xs_ref, ys_hbm, sem, scatter=False)


_POS_SPEC = pl.BlockSpec((None, 1, TM), lambda i: (i, 0, 0), memory_space=pltpu.SMEM)
_ANY = pl.BlockSpec(memory_space=pl.ANY)


def _dispatch(pos, xe, xs_prev):
    return pl.pallas_call(
        _dispatch_kernel,
        grid=(N_TILES,),
        in_specs=[_POS_SPEC, pl.BlockSpec((TM, XE_W), lambda i: (i, 0)), _ANY],
        out_specs=_ANY,
        out_shape=jax.ShapeDtypeStruct((N_SORTED, XE_W), F32),
        scratch_shapes=[pltpu.SemaphoreType.DMA(())],
        input_output_aliases={2: 0},
        compiler_params=_params("arbitrary"),
        name="dispatch",
    )(pos, xe, xs_prev)


def _combine(pos, ys):
    return pl.pallas_call(
        _combine_kernel,
        grid=(N_TILES,),
        in_specs=[_POS_SPEC, _ANY],
        out_specs=list(_pair_specs(D_MODEL)),
        out_shape=[jax.ShapeDtypeStruct((NP, D_MODEL), F32), jax.ShapeDtypeStruct((NS, D_MODEL), F32)],
        scratch_shapes=[pltpu.SemaphoreType.DMA(())],
        compiler_params=_params("arbitrary"),
        name="combine",
    )(pos, ys)


def _experts_kernel(ea_ref, eb_ref, nused_ref, xs_ref, wga_ref, wua_ref, wda_ref, wgb_ref, wub_ref, wdb_ref,
                    g2_ref, b2_ref, ys_ref):
    del ea_ref, eb_ref
    used = pl.program_id(0) < nused_ref[0]

    @pl.when(jnp.logical_not(used))
    def _():
        ys_ref[...] = jnp.zeros_like(ys_ref)

    @pl.when(used)
    def _():
        x1 = xs_ref[:, 0:D_MODEL]
        xb = x1.astype(BF16)

        def expert(wg, wu, wd):
            a = _dot(xb, wg[...])
            h = a * _sigmoid(a) * _dot(xb, wu[...])
            return _dot(h.astype(BF16), wd[...])

        ffn = xs_ref[:, D_MODEL + 1:D_MODEL + 2] * expert(wga_ref, wua_ref, wda_ref)
        ffn = ffn + xs_ref[:, D_MODEL + 2:D_MODEL + 3] * expert(wgb_ref, wub_ref, wdb_ref)
        ys_ref[...] = _layer_norm(DN_ALPHA * x1 + ffn, g2_ref[...], b2_ref[...])


def _experts(tile_a, tile_b, nused, xs, w_gate, w_up, w_down, g2, b2, layer):
    tile = lambda w: pl.BlockSpec((TM_E, w), lambda i, ea, eb, nu: (jnp.minimum(i, nu[0] - 1), 0))
    wa = lambda *shape: pl.BlockSpec((None, None) + shape, lambda i, ea, eb, nu: (layer, ea[i], 0, 0))
    wb = lambda *shape: pl.BlockSpec((None, None) + shape, lambda i, ea, eb, nu: (layer, eb[i], 0, 0))
    lay = pl.BlockSpec((None, 1, D_MODEL), lambda i, ea, eb, nu: (layer, 0, 0))
    return pl.pallas_call(
        _experts_kernel,
        grid_spec=pltpu.PrefetchScalarGridSpec(
            num_scalar_prefetch=3,
            grid=(NT_E,),
            in_specs=[tile(XE_W),
                      wa(D_MODEL, D_EXPERT), wa(D_MODEL, D_EXPERT), wa(D_EXPERT, D_MODEL),
                      wb(D_MODEL, D_EXPERT), wb(D_MODEL, D_EXPERT), wb(D_EXPERT, D_MODEL),
                      lay, lay],
            out_specs=pl.BlockSpec((TM_E, D_MODEL), lambda i, ea, eb, nu: (i, 0)),
        ),
        out_shape=jax.ShapeDtypeStruct((N_SORTED, D_MODEL), F32),
        compiler_params=_params("arbitrary"),
        name="experts",
    )(tile_a, tile_b, nused, xs, w_gate, w_up, w_down, w_gate, w_up, w_down, g2, b2)


_PAIR_LO = tuple(a for a, _ in PAIRS)
_PAIR_HI = tuple(b for _, b in PAIRS)


def kernel(x_prompt, x_sample, cache_attn_k, cache_attn_v, state_gla, state_pool, w_in, w_gate, attn_sink,
           gla_w_a2, gla_b_a, gla_norm_g, pool_w, pool_scale, w_branch, w_o, ln1_g, ln1_b, ln2_g, ln2_b,
           w_router, router_bias, w_e_gate, w_e_up, w_e_down):
    ag0 = PA_W + PG_W
    w_in_p = jnp.concatenate(
        [w_in[:, :, :ag0], w_in[:, :, ag0 + GLA_RANK:], w_in[:, :, ag0:ag0 + GLA_RANK],
         jnp.zeros((DEPTH, D_MODEL, LANES - GLA_RANK), w_in.dtype)], axis=-1).astype(BF16)
    w_gate_b, w_branch_b, w_o_b = w_gate.astype(BF16), w_branch.astype(BF16), w_o.astype(BF16)
    wa2_p = jnp.pad(gla_w_a2, ((0, 0), (0, LANES - GLA_RANK), (0, 0))).astype(BF16)
    pool_w_b = pool_w.astype(BF16)
    wr_t = w_router.T.astype(BF16)
    rb = router_bias.reshape(N_EXPERTS, 1).astype(F32)
    weg, weu, wed = w_e_gate.astype(BF16), w_e_up.astype(BF16), w_e_down.astype(BF16)
    cache_k = cache_attn_k.reshape(DEPTH, DEC_BATCH, WINDOW, KV_W)
    cache_v = cache_attn_v.reshape(DEPTH, DEC_BATCH, WINDOW, KV_W)
    pair_lo, pair_hi = jnp.array(_PAIR_LO, I32), jnp.array(_PAIR_HI, I32)

    x = (x_prompt.reshape(NP, D_MODEL), x_sample.reshape(NS, D_MODEL))
    xs = jnp.zeros((N_SORTED, XE_W), F32)
    pk, pv, ps, pp, sk, sv, ss, sp = ([] for _ in range(8))
    for l in range(DEPTH):
        pa, pg, pu, pag = _in_proj(x, w_in_p, l)
        row2 = lambda a: a[l].reshape(1, -1)

        oa_p, k_p, v_p = _attn_prompt(pa, attn_sink[l])
        oa_s, nk, nv = _attn_sample(pa, attn_sink[l], cache_k, cache_v, l)
        og_p, s_p = _gla_prompt(pg, pag, wa2_p[l], row2(gla_b_a), row2(gla_norm_g))
        og_s, s_s = _gla_sample(pg, pag, wa2_p[l], row2(gla_b_a), row2(gla_norm_g), state_gla, l)
        op_p, u_last = _pool_prompt(pu, pool_w_b[l], row2(pool_scale))
        u_s = pu[NP:].reshape(DEC_BATCH, DEC_SEQ, POOL_W)
        ext_s = jnp.concatenate(
            [jnp.zeros((DEC_BATCH, POOL_HIST - POOL_STATE, POOL_W), F32), state_pool[l], u_s], axis=1)
        op_s = _pool_sample(ext_s, pool_w_b[l], row2(pool_scale))

        xe, meta = _merge(x, (oa_p, oa_s, og_p, og_s, op_p, op_s), w_gate_b, w_branch_b, w_o_b,
                          ln1_g.reshape(DEPTH, 1, D_MODEL), ln1_b.reshape(DEPTH, 1, D_MODEL), wr_t, rb, l)
        cls = meta[:, 0, :].reshape(N_TOK // LANES, LANES).astype(I32)
        cls2d = jnp.pad(cls, ((0, PLAN_ROWS - N_TOK // LANES), (0, 0)), constant_values=-1)
        pos2d, tcls, nused = _plan(cls2d)
        pos = pos2d.reshape(-1)[:N_TOK].reshape(N_TILES, 1, TM)
        tile_cls = jnp.minimum(tcls[0, :NT_E], N_CLASSES - 1)
        tile_a = (tile_cls // len(PAIRS)) * EXPERTS_PER_GROUP + pair_lo[tile_cls % len(PAIRS)]
        tile_b = (tile_cls // len(PAIRS)) * EXPERTS_PER_GROUP + pair_hi[tile_cls % len(PAIRS)]
        xs = _dispatch(pos, xe, xs)
        ys = _experts(tile_a, tile_b, nused[0, :1], xs, weg, weu, wed, ln2_g.reshape(DEPTH, 1, D_MODEL),
                      ln2_b.reshape(DEPTH, 1, D_MODEL), l)
        x = _combine(pos, ys)

        pk.append(k_p.reshape(BATCH, WINDOW, N_KV_HEADS, HEAD_DIM))
        pv.append(v_p.reshape(BATCH, WINDOW, N_KV_HEADS, HEAD_DIM))
        ps.append(s_p)
        pp.append(u_last[:, POOL_HIST - POOL_STATE:, :])
        sk.append(nk.reshape(DEC_BATCH, WINDOW, N_KV_HEADS, HEAD_DIM))
        sv.append(nv.reshape(DEC_BATCH, WINDOW, N_KV_HEADS, HEAD_DIM))
        ss.append(s_s)
        sp.append(ext_s[:, POOL_HIST + DEC_SEQ - POOL_STATE:, :])
    return (x[0].reshape(BATCH, SEQ, D_MODEL), x[1].reshape(DEC_BATCH, DEC_SEQ, D_MODEL),
            jnp.stack(pk), jnp.stack(pv), jnp.stack(ps), jnp.stack(pp),
            jnp.stack(sk), jnp.stack(sv), jnp.stack(ss), jnp.stack(sp))
```

```python
import jax
import jax.numpy as jnp
import numpy as np
from jax import lax
from jax.experimental import pallas as pl
from jax.experimental.pallas import tpu as pltpu

F32, BF16, I32 = jnp.float32, jnp.bfloat16, jnp.int32

D_MODEL = 1024
BATCH = 8
SEQ = 2048
DEPTH = 4
DEC_BATCH = 128
DEC_SEQ = 8
N_HEADS = 8
N_KV_HEADS = 2
HEAD_DIM = 64
WINDOW = 128
ATTN_W = N_HEADS * HEAD_DIM
KV_W = N_KV_HEADS * HEAD_DIM
Q_PER_KV = N_HEADS // N_KV_HEADS
GLA_HEADS = 4
GLA_DK = 64
GLA_DV = 128
GLA_KW = GLA_HEADS * GLA_DK
GLA_VW = GLA_HEADS * GLA_DV
GLA_RANK = 16
GLA_TAU = 16.0
GLA_CHUNK = 64
POOL_WINDOWS = (2, 4, 8, 16)
POOL_GROUPS = 4
POOL_GW = 128
POOL_W = POOL_GROUPS * POOL_GW
POOL_STATE = max(POOL_WINDOWS) - 1
N_BRANCH = 3
BRANCH_W = 512
N_EXPERTS = 16
N_GROUPS = 4
EXPERTS_PER_GROUP = N_EXPERTS // N_GROUPS
D_EXPERT = 512
DN_ALPHA = (2.0 * DEPTH) ** 0.25
LN_EPS = 1e-5
RMS_EPS = 1e-6
NEG_INF = -1e30
ALIBI_SLOPES = tuple(2.0 ** (-8.0 * h / N_HEADS) for h in range(1, N_HEADS + 1))

LANES = 128
SUBLANES = 8

NP = BATCH * SEQ
NS = DEC_BATCH * DEC_SEQ
N_TOK = NP + NS
TM = 512
N_TILES = N_TOK // TM
PT = NP // TM

PA_W = ATTN_W + 2 * KV_W
PG_W = 2 * GLA_KW + 2 * GLA_VW
IN_W_PAD = PA_W + PG_W + POOL_W + LANES

PAIRS = ((0, 1), (0, 2), (0, 3), (1, 2), (1, 3), (2, 3))
N_CLASSES = N_GROUPS * len(PAIRS)
TM_E = 256
NT_E = -(-(N_TOK + N_CLASSES * (TM_E - 1)) // TM_E)
N_SORTED = NT_E * TM_E
XE_W = D_MODEL + LANES
PLAN_ROWS = 256

VMEM_LIMIT = 56 * 1024 * 1024


def _params(*sem):
    return pltpu.CompilerParams(dimension_semantics=sem, vmem_limit_bytes=VMEM_LIMIT)


def _dot(a, b):
    return jnp.dot(a, b, preferred_element_type=F32)


def _dot_nt(a, b):
    return lax.dot_general(a, b, (((1,), (1,)), ((), ())), preferred_element_type=F32)


def _dot_tn(a, b):
    return lax.dot_general(a, b, (((0,), (0,)), ((), ())), preferred_element_type=F32)


def _bdot_nt(a, b):
    return lax.dot_general(a, b, (((2,), (2,)), ((0,), (0,))), preferred_element_type=F32)


def _bdot(a, b):
    return lax.dot_general(a, b, (((2,), (1,)), ((0,), (0,))), preferred_element_type=F32)


def _layer_norm(h, g, b):
    mu = jnp.mean(h, axis=-1, keepdims=True)
    hc = h - mu
    var = jnp.mean(hc * hc, axis=-1, keepdims=True)
    return hc * lax.rsqrt(var + LN_EPS) * g + b


def _sigmoid(x):
    return 1.0 / (1.0 + jnp.exp(-x))


def _log_sigmoid(x):
    return jnp.minimum(x, 0.0) - jnp.log1p(jnp.exp(-jnp.abs(x)))


def _pair_specs(width):
    return (pl.BlockSpec((TM, width), lambda i, *_: (jnp.minimum(i, PT - 1), 0)),
            pl.BlockSpec((TM, width), lambda i, *_: (jnp.maximum(i - PT, 0), 0)))


def _pair_tile(p_ref, s_ref):
    return jnp.where(pl.program_id(0) < PT, p_ref[...], s_ref[...])


def _in_proj_kernel(xp_ref, xs_ref, w_ref, pa_ref, pg_ref, pu_ref, pag_ref):
    xb = _pair_tile(xp_ref, xs_ref).astype(BF16)
    pa_ref[...] = _dot(xb, w_ref[:, 0:PA_W])
    pg_ref[...] = _dot(xb, w_ref[:, PA_W:PA_W + PG_W])
    pu_ref[...] = _dot(xb, w_ref[:, PA_W + PG_W:PA_W + PG_W + POOL_W])
    pag_ref[...] = _dot(xb, w_ref[:, PA_W + PG_W + POOL_W:IN_W_PAD])


def _in_proj(x_pair, w_in_p, layer):
    row = lambda w: pl.BlockSpec((TM, w), lambda i: (i, 0))
    return pl.pallas_call(
        _in_proj_kernel,
        grid=(N_TILES,),
        in_specs=[*_pair_specs(D_MODEL), pl.BlockSpec((None, D_MODEL, IN_W_PAD), lambda i: (layer, 0, 0))],
        out_specs=[row(PA_W), row(PG_W), row(POOL_W), row(LANES)],
        out_shape=[jax.ShapeDtypeStruct((N_TOK, w), F32) for w in (PA_W, PG_W, POOL_W, LANES)],
        compiler_params=_params("arbitrary"),
        name="in_proj",
    )(*x_pair, w_in_p)


def _softmax_sink_pv(parts, sink, pv):
    m = sink
    for s, _ in parts:
        m = jnp.maximum(m, jnp.max(s, axis=-1, keepdims=True))
    den = jnp.exp(sink - m)
    es = []
    for s, _ in parts:
        e = jnp.exp(s - m)
        den = den + jnp.sum(e, axis=-1, keepdims=True)
        es.append(e)
    inv = 1.0 / den
    out = None
    for e, (_, v) in zip(es, parts):
        o = pv((e * inv).astype(BF16), v)
        out = o if out is None else out + o
    return out


def _head_column(sink_ref, kv, rows_per_head):
    g = lax.broadcasted_iota(I32, (Q_PER_KV * rows_per_head, 1), 0) // rows_per_head
    col = jnp.zeros(g.shape, F32)
    for i in range(Q_PER_KV):
        col = jnp.where(g == i, sink_ref[kv * Q_PER_KV + i], col)
    return col


def _alibi_bias(dist, visible):
    out = np.empty((N_KV_HEADS, Q_PER_KV * dist.shape[0], dist.shape[1]), np.float32)
    for h in range(N_HEADS):
        kv, g = divmod(h, Q_PER_KV)
        out[kv, g * dist.shape[0]:(g + 1) * dist.shape[0]] = np.where(visible, -ALIBI_SLOPES[h] * dist, NEG_INF)
    return out


ATTN_QB = 2


def _attn_prompt_kernel(sink_ref, cur_ref, prev_ref, bias0_ref, bias_ref, o_ref, nk_ref, nv_ref):
    k = jnp.concatenate([prev_ref[:, 0:KV_W], cur_ref[:, ATTN_W:ATTN_W + KV_W]], axis=0).astype(BF16)
    vt = jnp.concatenate([prev_ref[:, KV_W:2 * KV_W], cur_ref[:, ATTN_W + KV_W:PA_W]], axis=0).T.astype(BF16)
    g_of_col = lax.broadcasted_iota(I32, (1, Q_PER_KV * WINDOW), 1) // WINDOW
    for kv in range(N_KV_HEADS):
        ks = slice(kv * HEAD_DIM, (kv + 1) * HEAD_DIM)
        heads = range(kv * Q_PER_KV, (kv + 1) * Q_PER_KV)
        sink = jnp.zeros(g_of_col.shape, F32)
        for g, h in enumerate(heads):
            sink = jnp.where(g_of_col == g, sink_ref[h], sink)
        for j in range(ATTN_QB):
            rows = slice(j * WINDOW, (j + 1) * WINDOW)
            keys = slice(j * WINDOW, (j + 2) * WINDOW)
            q = jnp.concatenate([cur_ref[rows, h * HEAD_DIM:(h + 1) * HEAD_DIM] for h in heads], axis=0)
            bias = bias0_ref[kv] if j == 0 else bias_ref[kv]
            st = _dot_nt(k[keys, ks], (q * (HEAD_DIM ** -0.5)).astype(BF16)) + bias
            m = jnp.maximum(sink, jnp.max(st, axis=0, keepdims=True))
            e = jnp.exp(st - m)
            inv = 1.0 / (jnp.exp(sink - m) + jnp.sum(e, axis=0, keepdims=True))
            ot = _dot(vt[ks, keys], (e * inv).astype(BF16))
            for pair in range(Q_PER_KV // 2):
                two = jnp.concatenate(
                    [ot[:, (2 * pair + g) * WINDOW:(2 * pair + g + 1) * WINDOW] for g in range(2)], axis=0)
                lo = (kv * Q_PER_KV + 2 * pair) * HEAD_DIM
                o_ref[rows, lo:lo + 2 * HEAD_DIM] = two.T.astype(BF16)

    @pl.when(pl.program_id(1) == pl.num_programs(1) - 1)
    def _():
        last = slice((ATTN_QB - 1) * WINDOW, ATTN_QB * WINDOW)
        nk_ref[...] = cur_ref[last, ATTN_W:ATTN_W + KV_W]
        nv_ref[...] = cur_ref[last, ATTN_W + KV_W:PA_W]


def _attn_prompt(pa, sink):
    nb = SEQ // WINDOW
    ns = nb // ATTN_QB
    r = np.arange(WINDOW)[:, None]
    c = np.arange(2 * WINDOW)[None, :]
    band = (c > r) & (c <= WINDOW + r)
    bias = np.stack([_alibi_bias(WINDOW + r - c, band & (c >= WINDOW)), _alibi_bias(WINDOW + r - c, band)])
    bias = jnp.asarray(bias.transpose(0, 1, 3, 2))
    bias_spec = lambda variant: pl.BlockSpec((None, N_KV_HEADS, 2 * WINDOW, Q_PER_KV * WINDOW),
                                             lambda b, i, s: (variant(i), 0, 0, 0))
    state = pl.BlockSpec((None, WINDOW, KV_W), lambda b, i, s: (b, 0, 0))
    return pl.pallas_call(
        _attn_prompt_kernel,
        grid_spec=pltpu.PrefetchScalarGridSpec(
            num_scalar_prefetch=1,
            grid=(BATCH, ns),
            in_specs=[
                pl.BlockSpec((ATTN_QB * WINDOW, PA_W), lambda b, i, s: (b * ns + i, 0)),
                pl.BlockSpec((WINDOW, 2 * KV_W),
                             lambda b, i, s: (b * nb + jnp.maximum(ATTN_QB * i - 1, 0), ATTN_W // (2 * KV_W))),
                bias_spec(lambda i: jnp.minimum(i, 1)), bias_spec(lambda i: 1),
            ],
            out_specs=[pl.BlockSpec((ATTN_QB * WINDOW, ATTN_W), lambda b, i, s: (b * ns + i, 0)), state, state],
        ),
        out_shape=[jax.ShapeDtypeStruct((NP, ATTN_W), BF16),
                   jax.ShapeDtypeStruct((BATCH, WINDOW, KV_W), F32),
                   jax.ShapeDtypeStruct((BATCH, WINDOW, KV_W), F32)],
        compiler_params=_params("arbitrary", "arbitrary"),
        name="attn_prompt",
    )(sink, pa, pa, bias, bias)


ATTN_SB = 16


def _attn_sample_kernel(sink_ref, cur_ref, kc_ref, vc_ref, bias_c_ref, bias_n_ref, o_ref, nk_ref, nv_ref):
    cur = cur_ref[...].reshape(ATTN_SB, DEC_SEQ, PA_W)
    for kv in range(N_KV_HEADS):
        ks = slice(kv * HEAD_DIM, (kv + 1) * HEAD_DIM)
        heads = range(kv * Q_PER_KV, (kv + 1) * Q_PER_KV)
        q = jnp.concatenate([cur[:, :, h * HEAD_DIM:(h + 1) * HEAD_DIM] for h in heads], axis=1)
        q = (q * (HEAD_DIM ** -0.5)).astype(BF16)
        kn = cur[:, :, ATTN_W + kv * HEAD_DIM:ATTN_W + (kv + 1) * HEAD_DIM].astype(BF16)
        vn = cur[:, :, ATTN_W + KV_W + kv * HEAD_DIM:ATTN_W + KV_W + (kv + 1) * HEAD_DIM].astype(BF16)
        sc = _bdot_nt(q, kc_ref[:, :, ks].astype(BF16)) + bias_c_ref[kv]
        sn = _bdot_nt(q, kn) + bias_n_ref[kv]
        sink = _head_column(sink_ref, kv, DEC_SEQ)
        o = _softmax_sink_pv([(sc, vc_ref[:, :, ks].astype(BF16)), (sn, vn)], sink, _bdot)
        for g, h in enumerate(heads):
            o_ref[:, h * HEAD_DIM:(h + 1) * HEAD_DIM] = (
                o[:, g * DEC_SEQ:(g + 1) * DEC_SEQ, :].reshape(ATTN_SB * DEC_SEQ, HEAD_DIM).astype(BF16))
    keep = WINDOW - DEC_SEQ
    nk_ref[:, 0:keep, :] = kc_ref[:, DEC_SEQ:WINDOW, :]
    nk_ref[:, keep:WINDOW, :] = cur[:, :, ATTN_W:ATTN_W + KV_W]
    nv_ref[:, 0:keep, :] = vc_ref[:, DEC_SEQ:WINDOW, :]
    nv_ref[:, keep:WINDOW, :] = cur[:, :, ATTN_W + KV_W:PA_W]


def _attn_sample(pa, sink, cache_k, cache_v, layer):
    rows = ATTN_SB * DEC_SEQ
    first = NP // rows
    t = np.arange(DEC_SEQ)[:, None]
    jc = np.arange(WINDOW)[None, :]
    jn = np.arange(DEC_SEQ)[None, :]
    bias_c = jnp.asarray(_alibi_bias(WINDOW + t - jc, jc > t))
    bias_n = jnp.asarray(_alibi_bias(t - jn, jn <= t))
    cache = pl.BlockSpec((None, ATTN_SB, WINDOW, KV_W), lambda j, s: (layer, j, 0, 0))
    new = pl.BlockSpec((ATTN_SB, WINDOW, KV_W), lambda j, s: (j, 0, 0))
    const = lambda a: pl.BlockSpec(a.shape, lambda j, s: (0,) * a.ndim)
    return pl.pallas_call(
        _attn_sample_kernel,
        grid_spec=pltpu.PrefetchScalarGridSpec(
            num_scalar_prefetch=1,
            grid=(DEC_BATCH // ATTN_SB,),
            in_specs=[pl.BlockSpec((rows, PA_W), lambda j, s: (first + j, 0)), cache, cache,
                      const(bias_c), const(bias_n)],
            out_specs=[pl.BlockSpec((rows, ATTN_W), lambda j, s: (j, 0)), new, new],
        ),
        out_shape=[jax.ShapeDtypeStruct((NS, ATTN_W), BF16),
                   jax.ShapeDtypeStruct((DEC_BATCH, WINDOW, KV_W), F32),
                   jax.ShapeDtypeStruct((DEC_BATCH, WINDOW, KV_W), F32)],
        compiler_params=_params("arbitrary"),
        name="attn_sample",
    )(sink, pa, cache_k, cache_v, bias_c, bias_n)


def _chunk_cumsum(x, chunk):
    pos = lax.broadcasted_iota(I32, x.shape, 0) % chunk
    sh = 1
    while sh < chunk:
        x = x + jnp.where(pos >= sh, pltpu.roll(x, sh, 0), 0.0)
        sh *= 2
    return x


def _gla_log_decay(pag_ref, wa2_ref, ba_ref):
    z = _dot(pag_ref[...].astype(BF16), wa2_ref[...]) + ba_ref[...]
    return _log_sigmoid(z) / GLA_TAU


def _gla_finish(o, rg, g):
    o = o * lax.rsqrt(jnp.mean(o * o, axis=-1, keepdims=True) + RMS_EPS)
    return (o * g * (rg * _sigmoid(rg))).astype(BF16)


GLA_TT = 512
GLA_AG = 128


def _gla_prompt_kernel(pg_ref, pag_ref, wa2_ref, ba_ref, g_ref, o_ref, s_ref,
                       qd_ref, kd_ref, kdec_ref, a_ref, sall_ref, st_ref, acc_ref):
    nc = GLA_TT // GLA_CHUNK

    @pl.when(pl.program_id(1) == 0)
    def _():
        st_ref[...] = jnp.zeros_like(st_ref)

    cum = _chunk_cumsum(_gla_log_decay(pag_ref, wa2_ref, ba_ref), GLA_CHUNK)
    cum3 = cum.reshape(nc, GLA_CHUNK, GLA_KW)
    tot3 = cum3[:, GLA_CHUNK - 1:GLA_CHUNK, :]
    k = pg_ref[:, GLA_KW:2 * GLA_KW]
    qd_ref[...] = (pg_ref[:, 0:GLA_KW] * (GLA_DK ** -0.5) * jnp.exp(cum)).astype(BF16)
    kd_ref[...] = (k * jnp.exp(-cum)).astype(BF16)
    kdec_ref[...] = (k * jnp.exp(tot3 - cum3).reshape(GLA_TT, GLA_KW)).astype(BF16)
    etot = jnp.exp(tot3)

    def hs(h):
        return slice(h * GLA_DK, (h + 1) * GLA_DK)

    def vs(h):
        return slice(2 * GLA_KW + h * GLA_DV, 2 * GLA_KW + (h + 1) * GLA_DV)

    for c in range(nc):
        rows = slice(c * GLA_CHUNK, (c + 1) * GLA_CHUNK)
        for h in range(GLA_HEADS):
            a_ref[c, :, hs(h)] = _dot_tn(pg_ref[rows, vs(h)].astype(BF16), kdec_ref[rows, hs(h)])

    st = st_ref[...]
    for c in range(nc):
        sall_ref[c] = st.astype(BF16)
        st = etot[c] * st + a_ref[c]
    st_ref[...] = st

    r = lax.broadcasted_iota(I32, (GLA_AG, GLA_AG), 0)
    col = lax.broadcasted_iota(I32, (GLA_AG, GLA_AG), 1)
    causal = (r // GLA_CHUNK == col // GLA_CHUNK) & (col <= r)
    for h in range(GLA_HEADS):
        out = slice(h * GLA_DV, (h + 1) * GLA_DV)
        for c in range(nc):
            rows = slice(c * GLA_CHUNK, (c + 1) * GLA_CHUNK)
            acc_ref[rows, out] = _dot_nt(qd_ref[rows, hs(h)], sall_ref[c, :, hs(h)])
        for a in range(GLA_TT // GLA_AG):
            rows = slice(a * GLA_AG, (a + 1) * GLA_AG)
            att = jnp.where(causal, _dot_nt(qd_ref[rows, hs(h)], kd_ref[rows, hs(h)]), 0.0)
            acc_ref[rows, out] += _dot(att.astype(BF16), pg_ref[rows, vs(h)].astype(BF16))
        rg = pg_ref[:, 2 * GLA_KW + GLA_VW + h * GLA_DV:2 * GLA_KW + GLA_VW + (h + 1) * GLA_DV]
        o_ref[:, out] = _gla_finish(acc_ref[:, out], rg, g_ref[:, out])

    @pl.when(pl.program_id(1) == pl.num_programs(1) - 1)
    def _():
        for h in range(GLA_HEADS):
            s_ref[h] = st_ref[:, hs(h)].T


def _gla_prompt(pg, pag, wa2, ba, g):
    nt = SEQ // GLA_TT
    nc = GLA_TT // GLA_CHUNK
    const = lambda shape: pl.BlockSpec(shape, lambda b, j: (0,) * len(shape))
    return pl.pallas_call(
        _gla_prompt_kernel,
        grid=(BATCH, nt),
        in_specs=[
            pl.BlockSpec((GLA_TT, PG_W), lambda b, j: (b * nt + j, 0)),
            pl.BlockSpec((GLA_TT, LANES), lambda b, j: (b * nt + j, 0)),
            const((LANES, GLA_KW)), const((1, GLA_KW)), const((1, GLA_VW)),
        ],
        out_specs=[
            pl.BlockSpec((GLA_TT, GLA_VW), lambda b, j: (b * nt + j, 0)),
            pl.BlockSpec((None, GLA_HEADS, GLA_DK, GLA_DV), lambda b, j: (b, 0, 0, 0)),
        ],
        out_shape=[jax.ShapeDtypeStruct((NP, GLA_VW), BF16),
                   jax.ShapeDtypeStruct((BATCH, GLA_HEADS, GLA_DK, GLA_DV), F32)],
        scratch_shapes=[pltpu.VMEM((GLA_TT, GLA_KW), BF16), pltpu.VMEM((GLA_TT, GLA_KW), BF16),
                        pltpu.VMEM((GLA_TT, GLA_KW), BF16),
                        pltpu.VMEM((nc, GLA_DV, GLA_KW), F32), pltpu.VMEM((nc, GLA_DV, GLA_KW), BF16),
                        pltpu.VMEM((GLA_DV, GLA_KW), F32), pltpu.VMEM((GLA_TT, GLA_VW), F32)],
        compiler_params=_params("arbitrary", "arbitrary"),
        name="gla_prompt",
    )(pg, pag, wa2, ba, g)


GLA_SB = 8


def _gla_sample_kernel(pg_ref, pag_ref, wa2_ref, ba_ref, g_ref, s0_ref, o_ref, s_ref):
    cum_all = _chunk_cumsum(_gla_log_decay(pag_ref, wa2_ref, ba_ref), DEC_SEQ)
    tri = (lax.broadcasted_iota(I32, (DEC_SEQ, DEC_SEQ), 0) >= lax.broadcasted_iota(I32, (DEC_SEQ, DEC_SEQ), 1))
    for s in range(GLA_SB):
        rows = slice(s * DEC_SEQ, (s + 1) * DEC_SEQ)
        cum = cum_all[rows, :]
        tot = cum[DEC_SEQ - 1:DEC_SEQ, :]
        q = pg_ref[rows, 0:GLA_KW] * (GLA_DK ** -0.5)
        k = pg_ref[rows, GLA_KW:2 * GLA_KW]
        qd = (q * jnp.exp(cum)).astype(BF16)
        kd = (k * jnp.exp(-cum)).astype(BF16)
        kdec = (k * jnp.exp(tot - cum)).astype(BF16)
        etot = jnp.exp(tot)
        etot_col = [jnp.broadcast_to(etot[:, p * LANES:(p + 1) * LANES], (SUBLANES, LANES)).T[:, 0:1]
                    for p in range(GLA_KW // LANES)]
        for h in range(GLA_HEADS):
            ks = slice(h * GLA_DK, (h + 1) * GLA_DK)
            vs = slice(2 * GLA_KW + h * GLA_DV, 2 * GLA_KW + (h + 1) * GLA_DV)
            rs = slice(2 * GLA_KW + GLA_VW + h * GLA_DV, 2 * GLA_KW + GLA_VW + (h + 1) * GLA_DV)
            v = pg_ref[rows, vs].astype(BF16)
            st = s0_ref[s, h]
            att = jnp.where(tri, _dot_nt(qd[:, ks], kd[:, ks]), 0.0)
            o = _dot(qd[:, ks], st.astype(BF16)) + _dot(att.astype(BF16), v)
            per = LANES // GLA_DK
            col = etot_col[h // per][(h % per) * GLA_DK:(h % per + 1) * GLA_DK, :]
            s_ref[s, h] = col * st + _dot_tn(kdec[:, ks], v)
            o_ref[rows, h * GLA_DV:(h + 1) * GLA_DV] = _gla_finish(o, pg_ref[rows, rs], g_ref[:, h * GLA_DV:(h + 1) * GLA_DV])


def _gla_sample(pg, pag, wa2, ba, g, state, layer):
    rows = GLA_SB * DEC_SEQ
    first = NP // rows
    const = lambda shape: pl.BlockSpec(shape, lambda j: (0,) * len(shape))
    return pl.pallas_call(
        _gla_sample_kernel,
        grid=(DEC_BATCH // GLA_SB,),
        in_specs=[
            pl.BlockSpec((rows, PG_W), lambda j: (first + j, 0)),
            pl.BlockSpec((rows, LANES), lambda j: (first + j, 0)),
            const((LANES, GLA_KW)), const((1, GLA_KW)), const((1, GLA_VW)),
            pl.BlockSpec((None, GLA_SB, GLA_HEADS, GLA_DK, GLA_DV), lambda j: (layer, j, 0, 0, 0)),
        ],
        out_specs=[
            pl.BlockSpec((rows, GLA_VW), lambda j: (j, 0)),
            pl.BlockSpec((GLA_SB, GLA_HEADS, GLA_DK, GLA_DV), lambda j: (j, 0, 0, 0)),
        ],
        out_shape=[jax.ShapeDtypeStruct((NS, GLA_VW), BF16),
                   jax.ShapeDtypeStruct((DEC_BATCH, GLA_HEADS, GLA_DK, GLA_DV), F32)],
        compiler_params=_params("arbitrary"),
        name="gla_sample",
    )(pg, pag, wa2, ba, g, state)


POOL_HIST = 16


def _pool_groups(ext, cnt, pw_ref, ps_ref, out_rows):
    ax = ext.ndim - 2
    outs = []
    for g, w in enumerate(POOL_WINDOWS):
        x = ext[..., g * POOL_GW:(g + 1) * POOL_GW]
        s, sh = x, 1
        while sh < w:
            s = s + pltpu.roll(s, sh, ax)
            sh *= 2
        if ext.ndim == 3:
            d = (s[:, POOL_HIST:, :] / cnt[g] - x[:, POOL_HIST:, :]).reshape(out_rows, POOL_GW)
        else:
            d = s[POOL_HIST:, :] / cnt[g] - x[POOL_HIST:, :]
        y = _dot(d.astype(BF16), pw_ref[g]) * ps_ref[:, g * POOL_GW:(g + 1) * POOL_GW]
        outs.append(y.astype(BF16))
    return outs


POOL_TT = 512


def _pool_prompt_kernel(pu_ref, pw_ref, ps_ref, o_ref, last_ref, hist_ref):
    j = pl.program_id(1)

    @pl.when(j == 0)
    def _():
        hist_ref[...] = jnp.zeros_like(hist_ref)

    u = pu_ref[...]
    ext = jnp.concatenate([hist_ref[...], u], axis=0)
    pos = j * POOL_TT + lax.broadcasted_iota(I32, (POOL_TT, 1), 0)
    cnt = [jnp.minimum(pos + 1, w).astype(F32) for w in POOL_WINDOWS]
    for g, y in enumerate(_pool_groups(ext, cnt, pw_ref, ps_ref, POOL_TT)):
        o_ref[:, g * POOL_GW:(g + 1) * POOL_GW] = y
    hist_ref[...] = u[POOL_TT - POOL_HIST:, :]

    @pl.when(j == pl.num_programs(1) - 1)
    def _():
        last_ref[...] = u[POOL_TT - POOL_HIST:, :]


def _pool_prompt(pu, pw, ps):
    nt = SEQ // POOL_TT
    return pl.pallas_call(
        _pool_prompt_kernel,
        grid=(BATCH, nt),
        in_specs=[
            pl.BlockSpec((POOL_TT, POOL_W), lambda b, j: (b * nt + j, 0)),
            pl.BlockSpec((POOL_GROUPS, POOL_GW, POOL_GW), lambda b, j: (0, 0, 0)),
            pl.BlockSpec((1, POOL_W), lambda b, j: (0, 0)),
        ],
        out_specs=[pl.BlockSpec((POOL_TT, POOL_W), lambda b, j: (b * nt + j, 0)),
                   pl.BlockSpec((None, POOL_HIST, POOL_W), lambda b, j: (b, 0, 0))],
        out_shape=[jax.ShapeDtypeStruct((NP, POOL_W), BF16),
                   jax.ShapeDtypeStruct((BATCH, POOL_HIST, POOL_W), F32)],
        scratch_shapes=[pltpu.VMEM((POOL_HIST, POOL_W), F32)],
        compiler_params=_params("arbitrary", "arbitrary"),
        name="pool_prompt",
    )(pu, pw, ps)


POOL_SB = 16


def _pool_sample_kernel(ext_ref, pw_ref, ps_ref, o_ref):
    cnt = [float(w) for w in POOL_WINDOWS]
    for g, y in enumerate(_pool_groups(ext_ref[...], cnt, pw_ref, ps_ref, POOL_SB * DEC_SEQ)):
        o_ref[:, g * POOL_GW:(g + 1) * POOL_GW] = y


def _pool_sample(ext, pw, ps):
    rows = POOL_SB * DEC_SEQ
    return pl.pallas_call(
        _pool_sample_kernel,
        grid=(DEC_BATCH // POOL_SB,),
        in_specs=[
            pl.BlockSpec((POOL_SB, POOL_HIST + DEC_SEQ, POOL_W), lambda j: (j, 0, 0)),
            pl.BlockSpec((POOL_GROUPS, POOL_GW, POOL_GW), lambda j: (0, 0, 0)),
            pl.BlockSpec((1, POOL_W), lambda j: (0, 0)),
        ],
        out_specs=pl.BlockSpec((rows, POOL_W), lambda j: (j, 0)),
        out_shape=jax.ShapeDtypeStruct((NS, POOL_W), BF16),
        compiler_params=_params("arbitrary"),
        name="pool_sample",
    )(ext, pw, ps)


def _route(sc, sel):
    gscore = []
    for g in range(N_GROUPS):
        v = sel[EXPERTS_PER_GROUP * g:EXPERTS_PER_GROUP * (g + 1)]
        best = None
        for a, b in PAIRS:
            pair = v[a] + v[b]
            best = pair if best is None else jnp.maximum(best, pair)
        gscore.append(best)
    gi = jnp.zeros_like(gscore[0], dtype=I32)
    best = gscore[0]
    for g in range(1, N_GROUPS):
        upd = gscore[g] > best
        gi = jnp.where(upd, g, gi)
        best = jnp.where(upd, gscore[g], best)

    def in_group(rows, j):
        out = rows[(N_GROUPS - 1) * EXPERTS_PER_GROUP + j]
        for g in range(N_GROUPS - 2, -1, -1):
            out = jnp.where(gi == g, rows[g * EXPERTS_PER_GROUP + j], out)
        return out

    u = [in_group(sel, j) for j in range(EXPERTS_PER_GROUP)]
    s_in = [in_group(sc, j) for j in range(EXPERTS_PER_GROUP)]

    def argmax4(vals):
        idx = jnp.zeros_like(gi)
        m = vals[0]
        for j in range(1, EXPERTS_PER_GROUP):
            upd = vals[j] > m
            idx = jnp.where(upd, j, idx)
            m = jnp.where(upd, vals[j], m)
        return idx

    def pick(vals, idx):
        out = vals[EXPERTS_PER_GROUP - 1]
        for j in range(EXPERTS_PER_GROUP - 2, -1, -1):
            out = jnp.where(idx == j, vals[j], out)
        return out

    i1 = argmax4(u)
    i2 = argmax4([jnp.where(i1 == j, NEG_INF, u[j]) for j in range(EXPERTS_PER_GROUP)])
    w1, w2 = pick(s_in, i1), pick(s_in, i2)
    tot = w1 + w2
    w1, w2 = w1 / tot, w2 / tot
    lo, hi = jnp.minimum(i1, i2), jnp.maximum(i1, i2)
    first_lo = i1 < i2
    w_lo, w_hi = jnp.where(first_lo, w1, w2), jnp.where(first_lo, w2, w1)
    pair = jnp.where(lo == 0, hi - 1, jnp.where(lo == 1, hi + 1, 5))
    return gi * len(PAIRS) + pair, w_lo, w_hi


def _merge_kernel(xp_ref, xs_ref, oap_ref, oas_ref, ogp_ref, ogs_ref, opp_ref, ops_ref, wg_ref, wb_ref, wo_ref,
                  g1_ref, b1_ref, wr_ref, rb_ref, xe_ref, meta_ref):
    x = _pair_tile(xp_ref, xs_ref)
    xb = x.astype(BF16)
    merged = None
    for n, (brp, brs) in enumerate(((oap_ref, oas_ref), (ogp_ref, ogs_ref), (opp_ref, ops_ref))):
        gate = _sigmoid(_dot(xb, wg_ref[:, n * D_MODEL:(n + 1) * D_MODEL]))
        term = gate * _dot(_pair_tile(brp, brs), wb_ref[n])
        merged = term if merged is None else merged + term
    mix = _dot(merged.astype(BF16), wo_ref[...])
    x1 = _layer_norm(DN_ALPHA * x + mix, g1_ref[...], b1_ref[...])
    xe_ref[:, 0:D_MODEL] = x1
    sc_t = _sigmoid(_dot_nt(wr_ref[...], x1.astype(BF16)))
    sel_t = sc_t + rb_ref[...]
    sc = [sc_t[e:e + 1, :] for e in range(N_EXPERTS)]
    sel = [sel_t[e:e + 1, :] for e in range(N_EXPERTS)]
    cls, w_lo, w_hi = _route(sc, sel)
    cls = cls.astype(F32)

    def rows(n):
        rid = lax.broadcasted_iota(I32, (n, TM), 0)
        return jnp.where(rid == 0, cls, jnp.where(rid == 1, w_lo, jnp.where(rid == 2, w_hi, 0.0)))

    meta_ref[...] = rows(SUBLANES)
    xe_ref[:, D_MODEL:XE_W] = rows(LANES).T


def _merge(x_pair, branches, wg, wb, wo, g1, b1, wr_t, rb, layer):
    row = lambda w: pl.BlockSpec((TM, w), lambda i: (i, 0))
    lay = lambda *shape: pl.BlockSpec((None,) + shape, lambda i: (layer,) + (0,) * len(shape))
    const = lambda *shape: pl.BlockSpec(shape, lambda i: (0,) * len(shape))
    return pl.pallas_call(
        _merge_kernel,
        grid=(N_TILES,),
        in_specs=[*_pair_specs(D_MODEL), *(_pair_specs(BRANCH_W) * N_BRANCH),
                  lay(D_MODEL, N_BRANCH * D_MODEL), lay(N_BRANCH, BRANCH_W, D_MODEL), lay(D_MODEL, D_MODEL),
                  lay(1, D_MODEL), lay(1, D_MODEL), const(N_EXPERTS, D_MODEL), const(N_EXPERTS, 1)],
        out_specs=[row(XE_W), pl.BlockSpec((None, SUBLANES, TM), lambda i: (i, 0, 0))],
        out_shape=[jax.ShapeDtypeStruct((N_TOK, XE_W), F32), jax.ShapeDtypeStruct((N_TILES, SUBLANES, TM), F32)],
        compiler_params=_params("arbitrary"),
        name="merge",
    )(*x_pair, *branches, wg, wb, wo, g1, b1, wr_t, rb)


def _plan_kernel(cls_ref, pos_ref, tcls_ref, nused_ref):
    cls = cls_ref[...]
    lane_r = lax.broadcasted_iota(I32, (LANES, 2 * LANES), 0)
    lane_c = lax.broadcasted_iota(I32, (LANES, 2 * LANES), 1)
    lane_mat = ((lane_c >= LANES) | (lane_r < lane_c)).astype(BF16)
    row_r = lax.broadcasted_iota(I32, (2 * PLAN_ROWS, PLAN_ROWS), 0)
    row_c = lax.broadcasted_iota(I32, (2 * PLAN_ROWS, PLAN_ROWS), 1)
    row_mat = ((row_r >= PLAN_ROWS) | (row_c < row_r)).astype(BF16)
    tile_start = (lax.broadcasted_iota(I32, (SUBLANES, LANES), 1) * TM_E).astype(F32)
    pos = jnp.zeros((PLAN_ROWS, LANES), F32)
    off = jnp.zeros((PLAN_ROWS, LANES), F32)
    tcls = jnp.zeros((SUBLANES, LANES), I32)
    for c in range(N_CLASSES):
        m = cls == c
        lanes = _dot(m.astype(BF16), lane_mat)
        rows = _dot(row_mat, lanes[:, LANES:].astype(BF16))
        rank = lanes[:, 0:LANES] + rows[0:PLAN_ROWS]
        count = rows[PLAN_ROWS:]
        pos = jnp.where(m, off + rank, pos)
        off = off + jnp.ceil(count * (1.0 / TM_E)) * TM_E
        tcls = tcls + (off[0:SUBLANES] <= tile_start).astype(I32)
    pos_ref[...] = pos.astype(I32)
    tcls_ref[...] = tcls
    nused_ref[...] = (off[0:SUBLANES] * (1.0 / TM_E)).astype(I32)


def _plan(cls2d):
    return pl.pallas_call(
        _plan_kernel,
        out_shape=[jax.ShapeDtypeStruct((PLAN_ROWS, LANES), I32),
                   jax.ShapeDtypeStruct((SUBLANES, LANES), I32),
                   jax.ShapeDtypeStruct((SUBLANES, LANES), I32)],
        compiler_params=pltpu.CompilerParams(vmem_limit_bytes=VMEM_LIMIT),
        name="plan",
    )(cls2d)


def _row_copies(pos_ref, tile_ref, sorted_hbm, sem, scatter):
    def copy(r):
        row, srt = tile_ref.at[pl.ds(r, 1)], sorted_hbm.at[pl.ds(pos_ref[0, r], 1)]
        return pltpu.make_async_copy(row, srt, sem) if scatter else pltpu.make_async_copy(srt, row, sem)

    def wait(r, carry):
        copy(r).wait()
        return carry

    for r in range(TM):
        copy(r).start()
    lax.fori_loop(0, TM, wait, 0, unroll=8)


def _dispatch_kernel(pos_ref, x_ref, xs_in_hbm, xs_hbm, sem):
    del xs_in_hbm
    _row_copies(pos_ref, x_ref, xs_hbm, sem, scatter=True)


def _combine_kernel(pos_ref, ys_hbm, xp_ref, xs_ref, sem):
    @pl.when(pl.program_id(0) < PT)
    def _():
        _row_copies(pos_ref, xp_ref, ys_hbm, sem, scatter=False)

    @pl.when(pl.program_id(0) >= PT)
    def _():
        _row_copies(pos_ref, xs_ref, ys_hbm, sem, scatter=False)


_POS_SPEC = pl.BlockSpec((None, 1, TM), lambda i: (i, 0, 0), memory_space=pltpu.SMEM)
_ANY = pl.BlockSpec(memory_space=pl.ANY)


def _dispatch(pos, xe, xs_prev):
    return pl.pallas_call(
        _dispatch_kernel,
        grid=(N_TILES,),
        in_specs=[_POS_SPEC, pl.BlockSpec((TM, XE_W), lambda i: (i, 0)), _ANY],
        out_specs=_ANY,
        out_shape=jax.ShapeDtypeStruct((N_SORTED, XE_W), F32),
        scratch_shapes=[pltpu.SemaphoreType.DMA(())],
        input_output_aliases={2: 0},
        compiler_params=_params("arbitrary"),
        name="dispatch",
    )(pos, xe, xs_prev)


def _combine(pos, ys):
    return pl.pallas_call(
        _combine_kernel,
        grid=(N_TILES,),
        in_specs=[_POS_SPEC, _ANY],
        out_specs=list(_pair_specs(D_MODEL)),
        out_shape=[jax.ShapeDtypeStruct((NP, D_MODEL), F32), jax.ShapeDtypeStruct((NS, D_MODEL), F32)],
        scratch_shapes=[pltpu.SemaphoreType.DMA(())],
        compiler_params=_params("arbitrary"),
        name="combine",
    )(pos, ys)


def _experts_kernel(ea_ref, eb_ref, new_a_ref, new_b_ref, nused_ref, xs_ref,
                    wga_ref, wua_ref, wda_ref, wgb_ref, wub_ref, wdb_ref, g2_ref, b2_ref, ys_ref,
                    wga_s, wua_s, wda_s, wgb_s, wub_s, wdb_s):
    del ea_ref, eb_ref
    i = pl.program_id(0)
    used = i < nused_ref[0]

    @pl.when(jnp.logical_not(used))
    def _():
        ys_ref[...] = jnp.zeros_like(ys_ref)

    @pl.when(new_a_ref[i] == 1)
    def _():
        for src, dst in ((wga_ref, wga_s), (wua_ref, wua_s), (wda_ref, wda_s)):
            dst[...] = src[...].astype(BF16)

    @pl.when(new_b_ref[i] == 1)
    def _():
        for src, dst in ((wgb_ref, wgb_s), (wub_ref, wub_s), (wdb_ref, wdb_s)):
            dst[...] = src[...].astype(BF16)

    @pl.when(used)
    def _():
        x1 = xs_ref[:, 0:D_MODEL]
        xb = x1.astype(BF16)

        def expert(wg, wu, wd):
            a = _dot(xb, wg[...])
            h = a * _sigmoid(a) * _dot(xb, wu[...])
            return _dot(h.astype(BF16), wd[...])

        ffn = xs_ref[:, D_MODEL + 1:D_MODEL + 2] * expert(wga_s, wua_s, wda_s)
        ffn = ffn + xs_ref[:, D_MODEL + 2:D_MODEL + 3] * expert(wgb_s, wub_s, wdb_s)
        ys_ref[...] = _layer_norm(DN_ALPHA * x1 + ffn, g2_ref[...], b2_ref[...])


def _experts(tile_a, tile_b, new_a, new_b, nused, xs, w_gate, w_up, w_down, g2, b2, layer):
    tile = lambda w: pl.BlockSpec((TM_E, w), lambda i, ea, eb, na, nb, nu: (jnp.minimum(i, nu[0] - 1), 0))
    wa = lambda *shape: pl.BlockSpec((None, None) + shape, lambda i, ea, eb, na, nb, nu: (layer, ea[i], 0, 0))
    wb = lambda *shape: pl.BlockSpec((None, None) + shape, lambda i, ea, eb, na, nb, nu: (layer, eb[i], 0, 0))
    lay = pl.BlockSpec((None, 1, D_MODEL), lambda i, ea, eb, na, nb, nu: (layer, 0, 0))
    up, down = pltpu.VMEM((D_MODEL, D_EXPERT), BF16), pltpu.VMEM((D_EXPERT, D_MODEL), BF16)
    return pl.pallas_call(
        _experts_kernel,
        grid_spec=pltpu.PrefetchScalarGridSpec(
            num_scalar_prefetch=5,
            grid=(NT_E,),
            in_specs=[tile(XE_W),
                      wa(D_MODEL, D_EXPERT), wa(D_MODEL, D_EXPERT), wa(D_EXPERT, D_MODEL),
                      wb(D_MODEL, D_EXPERT), wb(D_MODEL, D_EXPERT), wb(D_EXPERT, D_MODEL),
                      lay, lay],
            out_specs=pl.BlockSpec((TM_E, D_MODEL), lambda i, ea, eb, na, nb, nu: (i, 0)),
            scratch_shapes=[up, up, down, up, up, down],
        ),
        out_shape=jax.ShapeDtypeStruct((N_SORTED, D_MODEL), F32),
        compiler_params=_params("arbitrary"),
        name="experts",
    )(tile_a, tile_b, new_a, new_b, nused, xs, w_gate, w_up, w_down, w_gate, w_up, w_down, g2, b2)


_PAIR_LO = tuple(a for a, _ in PAIRS)
_PAIR_HI = tuple(b for _, b in PAIRS)


def kernel(x_prompt, x_sample, cache_attn_k, cache_attn_v, state_gla, state_pool, w_in, w_gate, attn_sink,
           gla_w_a2, gla_b_a, gla_norm_g, pool_w, pool_scale, w_branch, w_o, ln1_g, ln1_b, ln2_g, ln2_b,
           w_router, router_bias, w_e_gate, w_e_up, w_e_down):
    ag0 = PA_W + PG_W
    w_in_p = jnp.concatenate(
        [w_in[:, :, :ag0], w_in[:, :, ag0 + GLA_RANK:], w_in[:, :, ag0:ag0 + GLA_RANK],
         jnp.zeros((DEPTH, D_MODEL, LANES - GLA_RANK), w_in.dtype)], axis=-1).astype(BF16)
    w_gate_b, w_branch_b, w_o_b = w_gate.astype(BF16), w_branch.astype(BF16), w_o.astype(BF16)
    wa2_p = jnp.pad(gla_w_a2, ((0, 0), (0, LANES - GLA_RANK), (0, 0))).astype(BF16)
    pool_w_b = pool_w.astype(BF16)
    wr_t = w_router.T.astype(BF16)
    rb = router_bias.reshape(N_EXPERTS, 1).astype(F32)
    cache_k = cache_attn_k.reshape(DEPTH, DEC_BATCH, WINDOW, KV_W)
    cache_v = cache_attn_v.reshape(DEPTH, DEC_BATCH, WINDOW, KV_W)
    pair_lo, pair_hi = jnp.array(_PAIR_LO, I32), jnp.array(_PAIR_HI, I32)

    x = (x_prompt.reshape(NP, D_MODEL), x_sample.reshape(NS, D_MODEL))
    xs = jnp.zeros((N_SORTED, XE_W), F32)
    pk, pv, ps, pp, sk, sv, ss, sp = ([] for _ in range(8))
    for l in range(DEPTH):
        pa, pg, pu, pag = _in_proj(x, w_in_p, l)
        row2 = lambda a: a[l].reshape(1, -1)

        oa_p, k_p, v_p = _attn_prompt(pa, attn_sink[l])
        oa_s, nk, nv = _attn_sample(pa, attn_sink[l], cache_k, cache_v, l)
        og_p, s_p = _gla_prompt(pg, pag, wa2_p[l], row2(gla_b_a), row2(gla_norm_g))
        og_s, s_s = _gla_sample(pg, pag, wa2_p[l], row2(gla_b_a), row2(gla_norm_g), state_gla, l)
        op_p, u_last = _pool_prompt(pu, pool_w_b[l], row2(pool_scale))
        u_s = pu[NP:].reshape(DEC_BATCH, DEC_SEQ, POOL_W)
        ext_s = jnp.concatenate(
            [jnp.zeros((DEC_BATCH, POOL_HIST - POOL_STATE, POOL_W), F32), state_pool[l], u_s], axis=1)
        op_s = _pool_sample(ext_s, pool_w_b[l], row2(pool_scale))

        xe, meta = _merge(x, (oa_p, oa_s, og_p, og_s, op_p, op_s), w_gate_b, w_branch_b, w_o_b,
                          ln1_g.reshape(DEPTH, 1, D_MODEL), ln1_b.reshape(DEPTH, 1, D_MODEL), wr_t, rb, l)
        cls = meta[:, 0, :].reshape(N_TOK // LANES, LANES).astype(I32)
        cls2d = jnp.pad(cls, ((0, PLAN_ROWS - N_TOK // LANES), (0, 0)), constant_values=-1)
        pos2d, tcls, nused = _plan(cls2d)
        pos = pos2d.reshape(-1)[:N_TOK].reshape(N_TILES, 1, TM)
        n_used = nused[0, :1]
        tile_cls = tcls[0, :NT_E]
        tile_cls = jnp.where(jnp.arange(NT_E) < n_used, tile_cls, tile_cls[n_used[0] - 1])
        tile_a = (tile_cls // len(PAIRS)) * EXPERTS_PER_GROUP + pair_lo[tile_cls % len(PAIRS)]
        tile_b = (tile_cls // len(PAIRS)) * EXPERTS_PER_GROUP + pair_hi[tile_cls % len(PAIRS)]
        first = jnp.ones((1,), I32)
        new_a = jnp.concatenate([first, (tile_a[1:] != tile_a[:-1]).astype(I32)])
        new_b = jnp.concatenate([first, (tile_b[1:] != tile_b[:-1]).astype(I32)])
        xs = _dispatch(pos, xe, xs)
        ys = _experts(tile_a, tile_b, new_a, new_b, n_used, xs, w_e_gate, w_e_up, w_e_down,
                      ln2_g.reshape(DEPTH, 1, D_MODEL), ln2_b.reshape(DEPTH, 1, D_MODEL), l)
        x = _combine(pos, ys)

        pk.append(k_p.reshape(BATCH, WINDOW, N_KV_HEADS, HEAD_DIM))
        pv.append(v_p.reshape(BATCH, WINDOW, N_KV_HEADS, HEAD_DIM))
        ps.append(s_p)
        pp.append(u_last[:, POOL_HIST - POOL_STATE:, :])
        sk.append(nk.reshape(DEC_BATCH, WINDOW, N_KV_HEADS, HEAD_DIM))
        sv.append(nv.reshape(DEC_BATCH, WINDOW, N_KV_HEADS, HEAD_DIM))
        ss.append(s_s)
        sp.append(ext_s[:, POOL_HIST + DEC_SEQ - POOL_STATE:, :])
    return (x[0].reshape(BATCH, SEQ, D_MODEL), x[1].reshape(DEC_BATCH, DEC_SEQ, D_MODEL),
            jnp.stack(pk), jnp.stack(pv), jnp.stack(ps), jnp.stack(pp),
            jnp.stack(sk), jnp.stack(sv), jnp.stack(ss), jnp.stack(sp))
```

```python
import jax
import jax.numpy as jnp
import numpy as np
from jax import lax
from jax.experimental import pallas as pl
from jax.experimental.pallas import tpu as pltpu

F32, BF16, I32 = jnp.float32, jnp.bfloat16, jnp.int32

D_MODEL = 1024
BATCH = 8
SEQ = 2048
DEPTH = 4
DEC_BATCH = 128
DEC_SEQ = 8
N_HEADS = 8
N_KV_HEADS = 2
HEAD_DIM = 64
WINDOW = 128
ATTN_W = N_HEADS * HEAD_DIM
KV_W = N_KV_HEADS * HEAD_DIM
Q_PER_KV = N_HEADS // N_KV_HEADS
GLA_HEADS = 4
GLA_DK = 64
GLA_DV = 128
GLA_KW = GLA_HEADS * GLA_DK
GLA_VW = GLA_HEADS * GLA_DV
GLA_RANK = 16
GLA_TAU = 16.0
GLA_CHUNK = 64
POOL_WINDOWS = (2, 4, 8, 16)
POOL_GROUPS = 4
POOL_GW = 128
POOL_W = POOL_GROUPS * POOL_GW
POOL_STATE = max(POOL_WINDOWS) - 1
N_BRANCH = 3
BRANCH_W = 512
N_EXPERTS = 16
N_GROUPS = 4
EXPERTS_PER_GROUP = N_EXPERTS // N_GROUPS
D_EXPERT = 512
DN_ALPHA = (2.0 * DEPTH) ** 0.25
LN_EPS = 1e-5
RMS_EPS = 1e-6
NEG_INF = -1e30
ALIBI_SLOPES = tuple(2.0 ** (-8.0 * h / N_HEADS) for h in range(1, N_HEADS + 1))

LANES = 128
SUBLANES = 8

NP = BATCH * SEQ
NS = DEC_BATCH * DEC_SEQ
N_TOK = NP + NS
TM = 512
N_TILES = N_TOK // TM
PT = NP // TM

PA_W = ATTN_W + 2 * KV_W
PG_W = 2 * GLA_KW + 2 * GLA_VW
IN_W_PAD = PA_W + PG_W + POOL_W + LANES

PAIRS = ((0, 1), (0, 2), (0, 3), (1, 2), (1, 3), (2, 3))
N_CLASSES = N_GROUPS * len(PAIRS)
TM_E = 256
NT_E = -(-(N_TOK + N_CLASSES * (TM_E - 1)) // TM_E)
N_SORTED = NT_E * TM_E
XE_W = D_MODEL + LANES
PLAN_ROWS = 256

VMEM_LIMIT = 56 * 1024 * 1024


def _params(*sem):
    return pltpu.CompilerParams(dimension_semantics=sem, vmem_limit_bytes=VMEM_LIMIT)


def _dot(a, b):
    return jnp.dot(a, b, preferred_element_type=F32)


def _dot_nt(a, b):
    return lax.dot_general(a, b, (((1,), (1,)), ((), ())), preferred_element_type=F32)


def _dot_tn(a, b):
    return lax.dot_general(a, b, (((0,), (0,)), ((), ())), preferred_element_type=F32)


def _bdot_nt(a, b):
    return lax.dot_general(a, b, (((2,), (2,)), ((0,), (0,))), preferred_element_type=F32)


def _bdot(a, b):
    return lax.dot_general(a, b, (((2,), (1,)), ((0,), (0,))), preferred_element_type=F32)


def _layer_norm(h, g, b):
    mu = jnp.mean(h, axis=-1, keepdims=True)
    hc = h - mu
    var = jnp.mean(hc * hc, axis=-1, keepdims=True)
    return hc * lax.rsqrt(var + LN_EPS) * g + b


def _sigmoid(x):
    return 0.5 * jnp.tanh(0.5 * x) + 0.5


def _log_sigmoid(x):
    return jnp.minimum(x, 0.0) - jnp.log1p(jnp.exp(-jnp.abs(x)))


def _pair_specs(width):
    return (pl.BlockSpec((TM, width), lambda i, *_: (jnp.minimum(i, PT - 1), 0)),
            pl.BlockSpec((TM, width), lambda i, *_: (jnp.maximum(i - PT, 0), 0)))


def _pair_tile(p_ref, s_ref):
    return jnp.where(pl.program_id(0) < PT, p_ref[...], s_ref[...])


_IN_PROJ_WIDTHS = (PA_W, PG_W, POOL_W, LANES)


def _in_proj_tile(x, w_ref, out_refs):
    xb = x.astype(BF16)
    lo = 0
    for width, out_ref in zip(_IN_PROJ_WIDTHS, out_refs):
        out_ref[...] = _dot(xb, w_ref[:, lo:lo + width])
        lo += width


def _in_proj_kernel(x_ref, w_ref, pa_ref, pg_ref, pu_ref, pag_ref):
    _in_proj_tile(x_ref[...], w_ref, (pa_ref, pg_ref, pu_ref, pag_ref))


def _in_proj_gather_kernel(pos_ref, pos_next_ref, ys_hbm, w_ref, pa_ref, pg_ref, pu_ref, pag_ref, x_ref, buf, sem):
    i = pl.program_id(0)
    slot = i % 2

    def copy(p_ref, r, s):
        return pltpu.make_async_copy(ys_hbm.at[pl.ds(p_ref[0, r], 1)], buf.at[s, pl.ds(r, 1)], sem.at[s])

    def wait_tile(p_ref, s):
        def wait(r, carry):
            copy(p_ref, r, s).wait()
            return carry
        lax.fori_loop(0, TM, wait, 0, unroll=8)

    @pl.when(i == 0)
    def _():
        for r in range(TM):
            copy(pos_ref, r, 0).start()

    wait_tile(pos_ref, slot)
    x = buf[slot]
    for r in range(TM):
        copy(pos_next_ref, r, 1 - slot).start()
    x_ref[...] = x
    _in_proj_tile(x, w_ref, (pa_ref, pg_ref, pu_ref, pag_ref))

    @pl.when(i == pl.num_programs(0) - 1)
    def _():
        wait_tile(pos_next_ref, 1 - slot)


def _in_proj(x, w_in_p, layer, pos=None):
    row = lambda w: pl.BlockSpec((TM, w), lambda i: (i, 0))
    w_spec = pl.BlockSpec((None, D_MODEL, IN_W_PAD), lambda i: (layer, 0, 0))
    outs = [jax.ShapeDtypeStruct((N_TOK, w), F32) for w in _IN_PROJ_WIDTHS]
    if pos is None:
        return pl.pallas_call(
            _in_proj_kernel,
            grid=(N_TILES,),
            in_specs=[row(D_MODEL), w_spec],
            out_specs=[row(w) for w in _IN_PROJ_WIDTHS],
            out_shape=outs,
            compiler_params=_params("arbitrary"),
            name="in_proj",
        )(x, w_in_p)
    pos_next = pl.BlockSpec((None, 1, TM), lambda i: (jnp.minimum(i + 1, N_TILES - 1), 0, 0), memory_space=pltpu.SMEM)
    return pl.pallas_call(
        _in_proj_gather_kernel,
        grid=(N_TILES,),
        in_specs=[_POS_SPEC, pos_next, _ANY, w_spec],
        out_specs=[row(w) for w in _IN_PROJ_WIDTHS] + [row(D_MODEL)],
        out_shape=outs + [jax.ShapeDtypeStruct((N_TOK, D_MODEL), F32)],
        scratch_shapes=[pltpu.VMEM((2, TM, D_MODEL), F32), pltpu.SemaphoreType.DMA((2,))],
        compiler_params=_params("arbitrary"),
        name="in_proj_gather",
    )(pos, pos, x, w_in_p)


def _softmax_sink_pv(parts, sink, pv):
    m = sink
    for s, _ in parts:
        m = jnp.maximum(m, jnp.max(s, axis=-1, keepdims=True))
    den = jnp.exp(sink - m)
    es = []
    for s, _ in parts:
        e = jnp.exp(s - m)
        den = den + jnp.sum(e, axis=-1, keepdims=True)
        es.append(e)
    inv = 1.0 / den
    out = None
    for e, (_, v) in zip(es, parts):
        o = pv((e * inv).astype(BF16), v)
        out = o if out is None else out + o
    return out


def _head_column(sink_ref, kv, rows_per_head):
    g = lax.broadcasted_iota(I32, (Q_PER_KV * rows_per_head, 1), 0) // rows_per_head
    col = jnp.zeros(g.shape, F32)
    for i in range(Q_PER_KV):
        col = jnp.where(g == i, sink_ref[kv * Q_PER_KV + i], col)
    return col


def _alibi_bias(dist, visible):
    out = np.empty((N_KV_HEADS, Q_PER_KV * dist.shape[0], dist.shape[1]), np.float32)
    for h in range(N_HEADS):
        kv, g = divmod(h, Q_PER_KV)
        out[kv, g * dist.shape[0]:(g + 1) * dist.shape[0]] = np.where(visible, -ALIBI_SLOPES[h] * dist, NEG_INF)
    return out


ATTN_QB = 4


def _attn_prompt_kernel(sink_ref, cur_ref, prev_ref, bias0_ref, bias_ref, o_ref, nk_ref, nv_ref):
    k = jnp.concatenate([prev_ref[:, 0:KV_W], cur_ref[:, ATTN_W:ATTN_W + KV_W]], axis=0).astype(BF16)
    vt = jnp.concatenate([prev_ref[:, KV_W:2 * KV_W], cur_ref[:, ATTN_W + KV_W:PA_W]], axis=0).T.astype(BF16)
    g_of_col = lax.broadcasted_iota(I32, (1, Q_PER_KV * WINDOW), 1) // WINDOW
    for kv in range(N_KV_HEADS):
        ks = slice(kv * HEAD_DIM, (kv + 1) * HEAD_DIM)
        heads = range(kv * Q_PER_KV, (kv + 1) * Q_PER_KV)
        sink = jnp.zeros(g_of_col.shape, F32)
        for g, h in enumerate(heads):
            sink = jnp.where(g_of_col == g, sink_ref[h], sink)
        for j in range(ATTN_QB):
            rows = slice(j * WINDOW, (j + 1) * WINDOW)
            keys = slice(j * WINDOW, (j + 2) * WINDOW)
            q = jnp.concatenate([cur_ref[rows, h * HEAD_DIM:(h + 1) * HEAD_DIM] for h in heads], axis=0)
            bias = bias0_ref[kv] if j == 0 else bias_ref[kv]
            st = _dot_nt(k[keys, ks], (q * (HEAD_DIM ** -0.5)).astype(BF16)) + bias
            m = jnp.maximum(sink, jnp.max(st, axis=0, keepdims=True))
            e = jnp.exp(st - m)
            inv = 1.0 / (jnp.exp(sink - m) + jnp.sum(e, axis=0, keepdims=True))
            ot = _dot(vt[ks, keys], (e * inv).astype(BF16))
            for pair in range(Q_PER_KV // 2):
                two = jnp.concatenate(
                    [ot[:, (2 * pair + g) * WINDOW:(2 * pair + g + 1) * WINDOW] for g in range(2)], axis=0)
                lo = (kv * Q_PER_KV + 2 * pair) * HEAD_DIM
                o_ref[rows, lo:lo + 2 * HEAD_DIM] = two.T.astype(BF16)

    @pl.when(pl.program_id(1) == pl.num_programs(1) - 1)
    def _():
        last = slice((ATTN_QB - 1) * WINDOW, ATTN_QB * WINDOW)
        nk_ref[...] = cur_ref[last, ATTN_W:ATTN_W + KV_W]
        nv_ref[...] = cur_ref[last, ATTN_W + KV_W:PA_W]


def _attn_prompt(pa, sink):
    nb = SEQ // WINDOW
    ns = nb // ATTN_QB
    r = np.arange(WINDOW)[:, None]
    c = np.arange(2 * WINDOW)[None, :]
    band = (c > r) & (c <= WINDOW + r)
    bias = np.stack([_alibi_bias(WINDOW + r - c, band & (c >= WINDOW)), _alibi_bias(WINDOW + r - c, band)])
    bias = jnp.asarray(bias.transpose(0, 1, 3, 2))
    bias_spec = lambda variant: pl.BlockSpec((None, N_KV_HEADS, 2 * WINDOW, Q_PER_KV * WINDOW),
                                             lambda b, i, s: (variant(i), 0, 0, 0))
    state = pl.BlockSpec((None, WINDOW, KV_W), lambda b, i, s: (b, 0, 0))
    return pl.pallas_call(
        _attn_prompt_kernel,
        grid_spec=pltpu.PrefetchScalarGridSpec(
            num_scalar_prefetch=1,
            grid=(BATCH, ns),
            in_specs=[
                pl.BlockSpec((ATTN_QB * WINDOW, PA_W), lambda b, i, s: (b * ns + i, 0)),
                pl.BlockSpec((WINDOW, 2 * KV_W),
                             lambda b, i, s: (b * nb + jnp.maximum(ATTN_QB * i - 1, 0), ATTN_W // (2 * KV_W))),
                bias_spec(lambda i: jnp.minimum(i, 1)), bias_spec(lambda i: 1),
            ],
            out_specs=[pl.BlockSpec((ATTN_QB * WINDOW, ATTN_W), lambda b, i, s: (b * ns + i, 0)), state, state],
        ),
        out_shape=[jax.ShapeDtypeStruct((NP, ATTN_W), BF16),
                   jax.ShapeDtypeStruct((BATCH, WINDOW, KV_W), F32),
                   jax.ShapeDtypeStruct((BATCH, WINDOW, KV_W), F32)],
        compiler_params=_params("arbitrary", "arbitrary"),
        name="attn_prompt",
    )(sink, pa, pa, bias, bias)


ATTN_SB = 16


def _attn_sample_kernel(sink_ref, cur_ref, kc_ref, vc_ref, bias_c_ref, bias_n_ref, o_ref, nk_ref, nv_ref):
    cur = cur_ref[...].reshape(ATTN_SB, DEC_SEQ, PA_W)
    for kv in range(N_KV_HEADS):
        ks = slice(kv * HEAD_DIM, (kv + 1) * HEAD_DIM)
        heads = range(kv * Q_PER_KV, (kv + 1) * Q_PER_KV)
        q = jnp.concatenate([cur[:, :, h * HEAD_DIM:(h + 1) * HEAD_DIM] for h in heads], axis=1)
        q = (q * (HEAD_DIM ** -0.5)).astype(BF16)
        kn = cur[:, :, ATTN_W + kv * HEAD_DIM:ATTN_W + (kv + 1) * HEAD_DIM].astype(BF16)
        vn = cur[:, :, ATTN_W + KV_W + kv * HEAD_DIM:ATTN_W + KV_W + (kv + 1) * HEAD_DIM].astype(BF16)
        sc = _bdot_nt(q, kc_ref[:, :, ks].astype(BF16)) + bias_c_ref[kv]
        sn = _bdot_nt(q, kn) + bias_n_ref[kv]
        sink = _head_column(sink_ref, kv, DEC_SEQ)
        o = _softmax_sink_pv([(sc, vc_ref[:, :, ks].astype(BF16)), (sn, vn)], sink, _bdot)
        for g, h in enumerate(heads):
            o_ref[:, h * HEAD_DIM:(h + 1) * HEAD_DIM] = (
                o[:, g * DEC_SEQ:(g + 1) * DEC_SEQ, :].reshape(ATTN_SB * DEC_SEQ, HEAD_DIM).astype(BF16))
    keep = WINDOW - DEC_SEQ
    nk_ref[:, 0:keep, :] = kc_ref[:, DEC_SEQ:WINDOW, :]
    nk_ref[:, keep:WINDOW, :] = cur[:, :, ATTN_W:ATTN_W + KV_W]
    nv_ref[:, 0:keep, :] = vc_ref[:, DEC_SEQ:WINDOW, :]
    nv_ref[:, keep:WINDOW, :] = cur[:, :, ATTN_W + KV_W:PA_W]


def _attn_sample(pa, sink, cache_k, cache_v, layer):
    rows = ATTN_SB * DEC_SEQ
    first = NP // rows
    t = np.arange(DEC_SEQ)[:, None]
    jc = np.arange(WINDOW)[None, :]
    jn = np.arange(DEC_SEQ)[None, :]
    bias_c = jnp.asarray(_alibi_bias(WINDOW + t - jc, jc > t))
    bias_n = jnp.asarray(_alibi_bias(t - jn, jn <= t))
    cache = pl.BlockSpec((None, ATTN_SB, WINDOW, KV_W), lambda j, s: (layer, j, 0, 0))
    new = pl.BlockSpec((ATTN_SB, WINDOW, KV_W), lambda j, s: (j, 0, 0))
    const = lambda a: pl.BlockSpec(a.shape, lambda j, s: (0,) * a.ndim)
    return pl.pallas_call(
        _attn_sample_kernel,
        grid_spec=pltpu.PrefetchScalarGridSpec(
            num_scalar_prefetch=1,
            grid=(DEC_BATCH // ATTN_SB,),
            in_specs=[pl.BlockSpec((rows, PA_W), lambda j, s: (first + j, 0)), cache, cache,
                      const(bias_c), const(bias_n)],
            out_specs=[pl.BlockSpec((rows, ATTN_W), lambda j, s: (j, 0)), new, new],
        ),
        out_shape=[jax.ShapeDtypeStruct((NS, ATTN_W), BF16),
                   jax.ShapeDtypeStruct((DEC_BATCH, WINDOW, KV_W), F32),
                   jax.ShapeDtypeStruct((DEC_BATCH, WINDOW, KV_W), F32)],
        compiler_params=_params("arbitrary"),
        name="attn_sample",
    )(sink, pa, cache_k, cache_v, bias_c, bias_n)


def _chunk_cumsum(x, chunk):
    pos = lax.broadcasted_iota(I32, x.shape, 0) % chunk
    sh = 1
    while sh < chunk:
        x = x + jnp.where(pos >= sh, pltpu.roll(x, sh, 0), 0.0)
        sh *= 2
    return x


def _gla_log_decay(pag_ref, wa2_ref, ba_ref):
    z = _dot(pag_ref[...].astype(BF16), wa2_ref[...]) + ba_ref[...]
    return _log_sigmoid(z) / GLA_TAU


def _gla_finish(o, rg, g):
    o = o * lax.rsqrt(jnp.mean(o * o, axis=-1, keepdims=True) + RMS_EPS)
    return (o * g * (rg * _sigmoid(rg))).astype(BF16)


GLA_TT = 512
GLA_AG = 128


def _gla_prompt_kernel(pg_ref, pag_ref, wa2_ref, ba_ref, g_ref, o_ref, s_ref,
                       qd_ref, kd_ref, kdec_ref, a_ref, sall_ref, st_ref, acc_ref):
    nc = GLA_TT // GLA_CHUNK

    @pl.when(pl.program_id(1) == 0)
    def _():
        st_ref[...] = jnp.zeros_like(st_ref)

    cum = _chunk_cumsum(_gla_log_decay(pag_ref, wa2_ref, ba_ref), GLA_CHUNK)
    cum3 = cum.reshape(nc, GLA_CHUNK, GLA_KW)
    tot3 = cum3[:, GLA_CHUNK - 1:GLA_CHUNK, :]
    k = pg_ref[:, GLA_KW:2 * GLA_KW]
    qd_ref[...] = (pg_ref[:, 0:GLA_KW] * (GLA_DK ** -0.5) * jnp.exp(cum)).astype(BF16)
    kd_ref[...] = (k * jnp.exp(-cum)).astype(BF16)
    kdec_ref[...] = (k * jnp.exp(tot3 - cum3).reshape(GLA_TT, GLA_KW)).astype(BF16)
    etot = jnp.exp(tot3)

    def hs(h):
        return slice(h * GLA_DK, (h + 1) * GLA_DK)

    def vs(h):
        return slice(2 * GLA_KW + h * GLA_DV, 2 * GLA_KW + (h + 1) * GLA_DV)

    for c in range(nc):
        rows = slice(c * GLA_CHUNK, (c + 1) * GLA_CHUNK)
        for h in range(GLA_HEADS):
            a_ref[c, :, hs(h)] = _dot_tn(pg_ref[rows, vs(h)].astype(BF16), kdec_ref[rows, hs(h)])

    st = st_ref[...]
    for c in range(nc):
        sall_ref[c] = st.astype(BF16)
        st = etot[c] * st + a_ref[c]
    st_ref[...] = st

    r = lax.broadcasted_iota(I32, (GLA_AG, GLA_AG), 0)
    col = lax.broadcasted_iota(I32, (GLA_AG, GLA_AG), 1)
    causal = (r // GLA_CHUNK == col // GLA_CHUNK) & (col <= r)
    for h in range(GLA_HEADS):
        out = slice(h * GLA_DV, (h + 1) * GLA_DV)
        for c in range(nc):
            rows = slice(c * GLA_CHUNK, (c + 1) * GLA_CHUNK)
            acc_ref[rows, out] = _dot_nt(qd_ref[rows, hs(h)], sall_ref[c, :, hs(h)])
        for a in range(GLA_TT // GLA_AG):
            rows = slice(a * GLA_AG, (a + 1) * GLA_AG)
            att = jnp.where(causal, _dot_nt(qd_ref[rows, hs(h)], kd_ref[rows, hs(h)]), 0.0)
            acc_ref[rows, out] += _dot(att.astype(BF16), pg_ref[rows, vs(h)].astype(BF16))
        rg = pg_ref[:, 2 * GLA_KW + GLA_VW + h * GLA_DV:2 * GLA_KW + GLA_VW + (h + 1) * GLA_DV]
        o_ref[:, out] = _gla_finish(acc_ref[:, out], rg, g_ref[:, out])

    @pl.when(pl.program_id(1) == pl.num_programs(1) - 1)
    def _():
        for h in range(GLA_HEADS):
            s_ref[h] = st_ref[:, hs(h)].T


def _gla_prompt(pg, pag, wa2, ba, g):
    nt = SEQ // GLA_TT
    nc = GLA_TT // GLA_CHUNK
    const = lambda shape: pl.BlockSpec(shape, lambda b, j: (0,) * len(shape))
    return pl.pallas_call(
        _gla_prompt_kernel,
        grid=(BATCH, nt),
        in_specs=[
            pl.BlockSpec((GLA_TT, PG_W), lambda b, j: (b * nt + j, 0)),
            pl.BlockSpec((GLA_TT, LANES), lambda b, j: (b * nt + j, 0)),
            const((LANES, GLA_KW)), const((1, GLA_KW)), const((1, GLA_VW)),
        ],
        out_specs=[
            pl.BlockSpec((GLA_TT, GLA_VW), lambda b, j: (b * nt + j, 0)),
            pl.BlockSpec((None, GLA_HEADS, GLA_DK, GLA_DV), lambda b, j: (b, 0, 0, 0)),
        ],
        out_shape=[jax.ShapeDtypeStruct((NP, GLA_VW), BF16),
                   jax.ShapeDtypeStruct((BATCH, GLA_HEADS, GLA_DK, GLA_DV), F32)],
        scratch_shapes=[pltpu.VMEM((GLA_TT, GLA_KW), BF16), pltpu.VMEM((GLA_TT, GLA_KW), BF16),
                        pltpu.VMEM((GLA_TT, GLA_KW), BF16),
                        pltpu.VMEM((nc, GLA_DV, GLA_KW), F32), pltpu.VMEM((nc, GLA_DV, GLA_KW), BF16),
                        pltpu.VMEM((GLA_DV, GLA_KW), F32), pltpu.VMEM((GLA_TT, GLA_VW), F32)],
        compiler_params=_params("arbitrary", "arbitrary"),
        name="gla_prompt",
    )(pg, pag, wa2, ba, g)


GLA_SB = 8


def _gla_sample_kernel(pg_ref, pag_ref, wa2_ref, ba_ref, g_ref, s0_ref, o_ref, s_ref):
    cum_all = _chunk_cumsum(_gla_log_decay(pag_ref, wa2_ref, ba_ref), DEC_SEQ)
    tri = (lax.broadcasted_iota(I32, (DEC_SEQ, DEC_SEQ), 0) >= lax.broadcasted_iota(I32, (DEC_SEQ, DEC_SEQ), 1))
    for s in range(GLA_SB):
        rows = slice(s * DEC_SEQ, (s + 1) * DEC_SEQ)
        cum = cum_all[rows, :]
        tot = cum[DEC_SEQ - 1:DEC_SEQ, :]
        q = pg_ref[rows, 0:GLA_KW] * (GLA_DK ** -0.5)
        k = pg_ref[rows, GLA_KW:2 * GLA_KW]
        qd = (q * jnp.exp(cum)).astype(BF16)
        kd = (k * jnp.exp(-cum)).astype(BF16)
        kdec = (k * jnp.exp(tot - cum)).astype(BF16)
        etot = jnp.exp(tot)
        etot_col = [jnp.broadcast_to(etot[:, p * LANES:(p + 1) * LANES], (SUBLANES, LANES)).T[:, 0:1]
                    for p in range(GLA_KW // LANES)]
        for h in range(GLA_HEADS):
            ks = slice(h * GLA_DK, (h + 1) * GLA_DK)
            vs = slice(2 * GLA_KW + h * GLA_DV, 2 * GLA_KW + (h + 1) * GLA_DV)
            rs = slice(2 * GLA_KW + GLA_VW + h * GLA_DV, 2 * GLA_KW + GLA_VW + (h + 1) * GLA_DV)
            v = pg_ref[rows, vs].astype(BF16)
            st = s0_ref[s, h]
            att = jnp.where(tri, _dot_nt(qd[:, ks], kd[:, ks]), 0.0)
            o = _dot(qd[:, ks], st.astype(BF16)) + _dot(att.astype(BF16), v)
            per = LANES // GLA_DK
            col = etot_col[h // per][(h % per) * GLA_DK:(h % per + 1) * GLA_DK, :]
            s_ref[s, h] = col * st + _dot_tn(kdec[:, ks], v)
            o_ref[rows, h * GLA_DV:(h + 1) * GLA_DV] = _gla_finish(o, pg_ref[rows, rs], g_ref[:, h * GLA_DV:(h + 1) * GLA_DV])


def _gla_sample(pg, pag, wa2, ba, g, state, layer):
    rows = GLA_SB * DEC_SEQ
    first = NP // rows
    const = lambda shape: pl.BlockSpec(shape, lambda j: (0,) * len(shape))
    return pl.pallas_call(
        _gla_sample_kernel,
        grid=(DEC_BATCH // GLA_SB,),
        in_specs=[
            pl.BlockSpec((rows, PG_W), lambda j: (first + j, 0)),
            pl.BlockSpec((rows, LANES), lambda j: (first + j, 0)),
            const((LANES, GLA_KW)), const((1, GLA_KW)), const((1, GLA_VW)),
            pl.BlockSpec((None, GLA_SB, GLA_HEADS, GLA_DK, GLA_DV), lambda j: (layer, j, 0, 0, 0)),
        ],
        out_specs=[
            pl.BlockSpec((rows, GLA_VW), lambda j: (j, 0)),
            pl.BlockSpec((GLA_SB, GLA_HEADS, GLA_DK, GLA_DV), lambda j: (j, 0, 0, 0)),
        ],
        out_shape=[jax.ShapeDtypeStruct((NS, GLA_VW), BF16),
                   jax.ShapeDtypeStruct((DEC_BATCH, GLA_HEADS, GLA_DK, GLA_DV), F32)],
        compiler_params=_params("arbitrary"),
        name="gla_sample",
    )(pg, pag, wa2, ba, g, state)


POOL_HIST = 16


def _pool_groups(ext, cnt, pw_ref, ps_ref, out_rows):
    ax = ext.ndim - 2
    outs = []
    for g, w in enumerate(POOL_WINDOWS):
        x = ext[..., g * POOL_GW:(g + 1) * POOL_GW]
        s, sh = x, 1
        while sh < w:
            s = s + pltpu.roll(s, sh, ax)
            sh *= 2
        if ext.ndim == 3:
            d = (s[:, POOL_HIST:, :] / cnt[g] - x[:, POOL_HIST:, :]).reshape(out_rows, POOL_GW)
        else:
            d = s[POOL_HIST:, :] / cnt[g] - x[POOL_HIST:, :]
        y = _dot(d.astype(BF16), pw_ref[g]) * ps_ref[:, g * POOL_GW:(g + 1) * POOL_GW]
        outs.append(y.astype(BF16))
    return outs


POOL_TT = 512


def _pool_prompt_kernel(pu_ref, pw_ref, ps_ref, o_ref, last_ref, hist_ref):
    j = pl.program_id(1)

    @pl.when(j == 0)
    def _():
        hist_ref[...] = jnp.zeros_like(hist_ref)

    u = pu_ref[...]
    ext = jnp.concatenate([hist_ref[...], u], axis=0)
    pos = j * POOL_TT + lax.broadcasted_iota(I32, (POOL_TT, 1), 0)
    cnt = [jnp.minimum(pos + 1, w).astype(F32) for w in POOL_WINDOWS]
    for g, y in enumerate(_pool_groups(ext, cnt, pw_ref, ps_ref, POOL_TT)):
        o_ref[:, g * POOL_GW:(g + 1) * POOL_GW] = y
    hist_ref[...] = u[POOL_TT - POOL_HIST:, :]

    @pl.when(j == pl.num_programs(1) - 1)
    def _():
        last_ref[...] = u[POOL_TT - POOL_HIST:, :]


def _pool_prompt(pu, pw, ps):
    nt = SEQ // POOL_TT
    return pl.pallas_call(
        _pool_prompt_kernel,
        grid=(BATCH, nt),
        in_specs=[
            pl.BlockSpec((POOL_TT, POOL_W), lambda b, j: (b * nt + j, 0)),
            pl.BlockSpec((POOL_GROUPS, POOL_GW, POOL_GW), lambda b, j: (0, 0, 0)),
            pl.BlockSpec((1, POOL_W), lambda b, j: (0, 0)),
        ],
        out_specs=[pl.BlockSpec((POOL_TT, POOL_W), lambda b, j: (b * nt + j, 0)),
                   pl.BlockSpec((None, POOL_HIST, POOL_W), lambda b, j: (b, 0, 0))],
        out_shape=[jax.ShapeDtypeStruct((NP, POOL_W), BF16),
                   jax.ShapeDtypeStruct((BATCH, POOL_HIST, POOL_W), F32)],
        scratch_shapes=[pltpu.VMEM((POOL_HIST, POOL_W), F32)],
        compiler_params=_params("arbitrary", "arbitrary"),
        name="pool_prompt",
    )(pu, pw, ps)


POOL_SB = 16


def _pool_sample_kernel(ext_ref, pw_ref, ps_ref, o_ref):
    cnt = [float(w) for w in POOL_WINDOWS]
    for g, y in enumerate(_pool_groups(ext_ref[...], cnt, pw_ref, ps_ref, POOL_SB * DEC_SEQ)):
        o_ref[:, g * POOL_GW:(g + 1) * POOL_GW] = y


def _pool_sample(ext, pw, ps):
    rows = POOL_SB * DEC_SEQ
    return pl.pallas_call(
        _pool_sample_kernel,
        grid=(DEC_BATCH // POOL_SB,),
        in_specs=[
            pl.BlockSpec((POOL_SB, POOL_HIST + DEC_SEQ, POOL_W), lambda j: (j, 0, 0)),
            pl.BlockSpec((POOL_GROUPS, POOL_GW, POOL_GW), lambda j: (0, 0, 0)),
            pl.BlockSpec((1, POOL_W), lambda j: (0, 0)),
        ],
        out_specs=pl.BlockSpec((rows, POOL_W), lambda j: (j, 0)),
        out_shape=jax.ShapeDtypeStruct((NS, POOL_W), BF16),
        compiler_params=_params("arbitrary"),
        name="pool_sample",
    )(ext, pw, ps)


def _route(sc, sel):
    gscore = []
    for g in range(N_GROUPS):
        v = sel[EXPERTS_PER_GROUP * g:EXPERTS_PER_GROUP * (g + 1)]
        best = None
        for a, b in PAIRS:
            pair = v[a] + v[b]
            best = pair if best is None else jnp.maximum(best, pair)
        gscore.append(best)
    gi = jnp.zeros_like(gscore[0], dtype=I32)
    best = gscore[0]
    for g in range(1, N_GROUPS):
        upd = gscore[g] > best
        gi = jnp.where(upd, g, gi)
        best = jnp.where(upd, gscore[g], best)

    def in_group(rows, j):
        out = rows[(N_GROUPS - 1) * EXPERTS_PER_GROUP + j]
        for g in range(N_GROUPS - 2, -1, -1):
            out = jnp.where(gi == g, rows[g * EXPERTS_PER_GROUP + j], out)
        return out

    u = [in_group(sel, j) for j in range(EXPERTS_PER_GROUP)]
    s_in = [in_group(sc, j) for j in range(EXPERTS_PER_GROUP)]

    def argmax4(vals):
        idx = jnp.zeros_like(gi)
        m = vals[0]
        for j in range(1, EXPERTS_PER_GROUP):
            upd = vals[j] > m
            idx = jnp.where(upd, j, idx)
            m = jnp.where(upd, vals[j], m)
        return idx

    def pick(vals, idx):
        out = vals[EXPERTS_PER_GROUP - 1]
        for j in range(EXPERTS_PER_GROUP - 2, -1, -1):
            out = jnp.where(idx == j, vals[j], out)
        return out

    i1 = argmax4(u)
    i2 = argmax4([jnp.where(i1 == j, NEG_INF, u[j]) for j in range(EXPERTS_PER_GROUP)])
    w1, w2 = pick(s_in, i1), pick(s_in, i2)
    tot = w1 + w2
    w1, w2 = w1 / tot, w2 / tot
    lo, hi = jnp.minimum(i1, i2), jnp.maximum(i1, i2)
    first_lo = i1 < i2
    w_lo, w_hi = jnp.where(first_lo, w1, w2), jnp.where(first_lo, w2, w1)
    pair = jnp.where(lo == 0, hi - 1, jnp.where(lo == 1, hi + 1, 5))
    return gi * len(PAIRS) + pair, w_lo, w_hi


def _merge_kernel(x_ref, oap_ref, oas_ref, ogp_ref, ogs_ref, opp_ref, ops_ref, wg_ref, wb_ref, wo_ref,
                  g1_ref, b1_ref, wr_ref, rb_ref, xe_ref, meta_ref):
    x = x_ref[...]
    xb = x.astype(BF16)
    merged = None
    for n, (brp, brs) in enumerate(((oap_ref, oas_ref), (ogp_ref, ogs_ref), (opp_ref, ops_ref))):
        gate = _sigmoid(_dot(xb, wg_ref[:, n * D_MODEL:(n + 1) * D_MODEL]))
        term = gate * _dot(_pair_tile(brp, brs), wb_ref[n])
        merged = term if merged is None else merged + term
    mix = _dot(merged.astype(BF16), wo_ref[...])
    x1 = _layer_norm(DN_ALPHA * x + mix, g1_ref[...], b1_ref[...])
    xe_ref[:, 0:D_MODEL] = x1
    sc_t = _sigmoid(_dot_nt(wr_ref[...], x1.astype(BF16)))
    sel_t = sc_t + rb_ref[...]
    sc = [sc_t[e:e + 1, :] for e in range(N_EXPERTS)]
    sel = [sel_t[e:e + 1, :] for e in range(N_EXPERTS)]
    cls, w_lo, w_hi = _route(sc, sel)
    cls = cls.astype(F32)

    def rows(n):
        rid = lax.broadcasted_iota(I32, (n, TM), 0)
        return jnp.where(rid == 0, cls, jnp.where(rid == 1, w_lo, jnp.where(rid == 2, w_hi, 0.0)))

    meta_ref[...] = rows(SUBLANES)
    xe_ref[:, D_MODEL:XE_W] = rows(LANES).T


def _merge(x, branches, wg, wb, wo, g1, b1, wr_t, rb, layer):
    row = lambda w: pl.BlockSpec((TM, w), lambda i: (i, 0))
    lay = lambda *shape: pl.BlockSpec((None,) + shape, lambda i: (layer,) + (0,) * len(shape))
    const = lambda *shape: pl.BlockSpec(shape, lambda i: (0,) * len(shape))
    return pl.pallas_call(
        _merge_kernel,
        grid=(N_TILES,),
        in_specs=[row(D_MODEL), *(_pair_specs(BRANCH_W) * N_BRANCH),
                  lay(D_MODEL, N_BRANCH * D_MODEL), lay(N_BRANCH, BRANCH_W, D_MODEL), lay(D_MODEL, D_MODEL),
                  lay(1, D_MODEL), lay(1, D_MODEL), const(N_EXPERTS, D_MODEL), const(N_EXPERTS, 1)],
        out_specs=[row(XE_W), pl.BlockSpec((None, SUBLANES, TM), lambda i: (i, 0, 0))],
        out_shape=[jax.ShapeDtypeStruct((N_TOK, XE_W), F32), jax.ShapeDtypeStruct((N_TILES, SUBLANES, TM), F32)],
        compiler_params=_params("arbitrary"),
        name="merge",
    )(x, *branches, wg, wb, wo, g1, b1, wr_t, rb)


def _plan_kernel(cls_ref, pos_ref, tcls_ref, nused_ref):
    cls = cls_ref[...]
    lane_r = lax.broadcasted_iota(I32, (LANES, 2 * LANES), 0)
    lane_c = lax.broadcasted_iota(I32, (LANES, 2 * LANES), 1)
    lane_mat = ((lane_c >= LANES) | (lane_r < lane_c)).astype(BF16)
    row_r = lax.broadcasted_iota(I32, (2 * PLAN_ROWS, PLAN_ROWS), 0)
    row_c = lax.broadcasted_iota(I32, (2 * PLAN_ROWS, PLAN_ROWS), 1)
    row_mat = ((row_r >= PLAN_ROWS) | (row_c < row_r)).astype(BF16)
    tile_start = (lax.broadcasted_iota(I32, (SUBLANES, LANES), 1) * TM_E).astype(F32)
    pos = jnp.zeros((PLAN_ROWS, LANES), F32)
    off = jnp.zeros((PLAN_ROWS, LANES), F32)
    tcls = jnp.zeros((SUBLANES, LANES), I32)
    for c in range(N_CLASSES):
        m = cls == c
        lanes = _dot(m.astype(BF16), lane_mat)
        rows = _dot(row_mat, lanes[:, LANES:].astype(BF16))
        rank = lanes[:, 0:LANES] + rows[0:PLAN_ROWS]
        count = rows[PLAN_ROWS:]
        pos = jnp.where(m, off + rank, pos)
        off = off + jnp.ceil(count * (1.0 / TM_E)) * TM_E
        tcls = tcls + (off[0:SUBLANES] <= tile_start).astype(I32)
    pos_ref[...] = pos.astype(I32)
    tcls_ref[...] = tcls
    nused_ref[...] = (off[0:SUBLANES] * (1.0 / TM_E)).astype(I32)


def _plan(cls2d):
    return pl.pallas_call(
        _plan_kernel,
        out_shape=[jax.ShapeDtypeStruct((PLAN_ROWS, LANES), I32),
                   jax.ShapeDtypeStruct((SUBLANES, LANES), I32),
                   jax.ShapeDtypeStruct((SUBLANES, LANES), I32)],
        compiler_params=pltpu.CompilerParams(vmem_limit_bytes=VMEM_LIMIT),
        name="plan",
    )(cls2d)


def _row_copies(pos_ref, tile_ref, sorted_hbm, sem, scatter):
    def copy(r):
        row, srt = tile_ref.at[pl.ds(r, 1)], sorted_hbm.at[pl.ds(pos_ref[0, r], 1)]
        return pltpu.make_async_copy(row, srt, sem) if scatter else pltpu.make_async_copy(srt, row, sem)

    def wait(r, carry):
        copy(r).wait()
        return carry

    for r in range(TM):
        copy(r).start()
    lax.fori_loop(0, TM, wait, 0, unroll=8)


def _dispatch_kernel(pos_ref, x_ref, xs_in_hbm, xs_hbm, sem):
    del xs_in_hbm
    _row_copies(pos_ref, x_ref, xs_hbm, sem, scatter=True)


def _combine_kernel(pos_ref, ys_hbm, xp_ref, xs_ref, sem):
    @pl.when(pl.program_id(0) < PT)
    def _():
        _row_copies(pos_ref, xp_ref, ys_hbm, sem, scatter=False)

    @pl.when(pl.program_id(0) >= PT)
    def _():
        _row_copies(pos_ref, xs_ref, ys_hbm, sem, scatter=False)


_POS_SPEC = pl.BlockSpec((None, 1, TM), lambda i: (i, 0, 0), memory_space=pltpu.SMEM)
_ANY = pl.BlockSpec(memory_space=pl.ANY)


def _dispatch(pos, xe, xs_prev):
    return pl.pallas_call(
        _dispatch_kernel,
        grid=(N_TILES,),
        in_specs=[_POS_SPEC, pl.BlockSpec((TM, XE_W), lambda i: (i, 0)), _ANY],
        out_specs=_ANY,
        out_shape=jax.ShapeDtypeStruct((N_SORTED, XE_W), F32),
        scratch_shapes=[pltpu.SemaphoreType.DMA(())],
        input_output_aliases={2: 0},
        compiler_params=_params("arbitrary"),
        name="dispatch",
    )(pos, xe, xs_prev)


def _combine(pos, ys):
    return pl.pallas_call(
        _combine_kernel,
        grid=(N_TILES,),
        in_specs=[_POS_SPEC, _ANY],
        out_specs=list(_pair_specs(D_MODEL)),
        out_shape=[jax.ShapeDtypeStruct((NP, D_MODEL), F32), jax.ShapeDtypeStruct((NS, D_MODEL), F32)],
        scratch_shapes=[pltpu.SemaphoreType.DMA(())],
        compiler_params=_params("arbitrary"),
        name="combine",
    )(pos, ys)


def _experts_kernel(ea_ref, eb_ref, new_a_ref, new_b_ref, nused_ref, xs_ref,
                    wga_ref, wua_ref, wda_ref, wgb_ref, wub_ref, wdb_ref, g2_ref, b2_ref, ys_ref,
                    wga_s, wua_s, wda_s, wgb_s, wub_s, wdb_s):
    del ea_ref, eb_ref
    i = pl.program_id(0)
    used = i < nused_ref[0]

    @pl.when(jnp.logical_not(used))
    def _():
        ys_ref[...] = jnp.zeros_like(ys_ref)

    @pl.when(new_a_ref[i] == 1)
    def _():
        for src, dst in ((wga_ref, wga_s), (wua_ref, wua_s), (wda_ref, wda_s)):
            dst[...] = src[...].astype(BF16)

    @pl.when(new_b_ref[i] == 1)
    def _():
        for src, dst in ((wgb_ref, wgb_s), (wub_ref, wub_s), (wdb_ref, wdb_s)):
            dst[...] = src[...].astype(BF16)

    @pl.when(used)
    def _():
        x1 = xs_ref[:, 0:D_MODEL]
        xb = x1.astype(BF16)

        def expert(wg, wu, wd):
            a = _dot(xb, wg[...])
            h = a * _sigmoid(a) * _dot(xb, wu[...])
            return _dot(h.astype(BF16), wd[...])

        ffn = xs_ref[:, D_MODEL + 1:D_MODEL + 2] * expert(wga_s, wua_s, wda_s)
        ffn = ffn + xs_ref[:, D_MODEL + 2:D_MODEL + 3] * expert(wgb_s, wub_s, wdb_s)
        ys_ref[...] = _layer_norm(DN_ALPHA * x1 + ffn, g2_ref[...], b2_ref[...])


def _experts(tile_a, tile_b, new_a, new_b, nused, xs, w_gate, w_up, w_down, g2, b2, layer):
    tile = lambda w: pl.BlockSpec((TM_E, w), lambda i, ea, eb, na, nb, nu: (jnp.minimum(i, nu[0] - 1), 0))
    wa = lambda *shape: pl.BlockSpec((None, None) + shape, lambda i, ea, eb, na, nb, nu: (layer, ea[i], 0, 0))
    wb = lambda *shape: pl.BlockSpec((None, None) + shape, lambda i, ea, eb, na, nb, nu: (layer, eb[i], 0, 0))
    lay = pl.BlockSpec((None, 1, D_MODEL), lambda i, ea, eb, na, nb, nu: (layer, 0, 0))
    up, down = pltpu.VMEM((D_MODEL, D_EXPERT), BF16), pltpu.VMEM((D_EXPERT, D_MODEL), BF16)
    return pl.pallas_call(
        _experts_kernel,
        grid_spec=pltpu.PrefetchScalarGridSpec(
            num_scalar_prefetch=5,
            grid=(NT_E,),
            in_specs=[tile(XE_W),
                      wa(D_MODEL, D_EXPERT), wa(D_MODEL, D_EXPERT), wa(D_EXPERT, D_MODEL),
                      wb(D_MODEL, D_EXPERT), wb(D_MODEL, D_EXPERT), wb(D_EXPERT, D_MODEL),
                      lay, lay],
            out_specs=pl.BlockSpec((TM_E, D_MODEL), lambda i, ea, eb, na, nb, nu: (i, 0)),
            scratch_shapes=[up, up, down, up, up, down],
        ),
        out_shape=jax.ShapeDtypeStruct((N_SORTED, D_MODEL), F32),
        compiler_params=_params("arbitrary"),
        name="experts",
    )(tile_a, tile_b, new_a, new_b, nused, xs, w_gate, w_up, w_down, w_gate, w_up, w_down, g2, b2)


_PAIR_LO = tuple(a for a, _ in PAIRS)
_PAIR_HI = tuple(b for _, b in PAIRS)


def kernel(x_prompt, x_sample, cache_attn_k, cache_attn_v, state_gla, state_pool, w_in, w_gate, attn_sink,
           gla_w_a2, gla_b_a, gla_norm_g, pool_w, pool_scale, w_branch, w_o, ln1_g, ln1_b, ln2_g, ln2_b,
           w_router, router_bias, w_e_gate, w_e_up, w_e_down):
    ag0 = PA_W + PG_W
    w_in_p = jnp.concatenate(
        [w_in[:, :, :ag0], w_in[:, :, ag0 + GLA_RANK:], w_in[:, :, ag0:ag0 + GLA_RANK],
         jnp.zeros((DEPTH, D_MODEL, LANES - GLA_RANK), w_in.dtype)], axis=-1).astype(BF16)
    w_gate_b, w_branch_b, w_o_b = w_gate.astype(BF16), w_branch.astype(BF16), w_o.astype(BF16)
    wa2_p = jnp.pad(gla_w_a2, ((0, 0), (0, LANES - GLA_RANK), (0, 0))).astype(BF16)
    pool_w_b = pool_w.astype(BF16)
    wr_t = w_router.T.astype(BF16)
    rb = router_bias.reshape(N_EXPERTS, 1).astype(F32)
    cache_k = cache_attn_k.reshape(DEPTH, DEC_BATCH, WINDOW, KV_W)
    cache_v = cache_attn_v.reshape(DEPTH, DEC_BATCH, WINDOW, KV_W)
    pair_lo, pair_hi = jnp.array(_PAIR_LO, I32), jnp.array(_PAIR_HI, I32)

    x = jnp.concatenate([x_prompt.reshape(NP, D_MODEL), x_sample.reshape(NS, D_MODEL)], axis=0)
    xs = jnp.zeros((N_SORTED, XE_W), F32)
    ys = pos = None
    pk, pv, ps, pp, sk, sv, ss, sp = ([] for _ in range(8))
    for l in range(DEPTH):
        if l == 0:
            pa, pg, pu, pag = _in_proj(x, w_in_p, l)
        else:
            pa, pg, pu, pag, x = _in_proj(ys, w_in_p, l, pos)
        row2 = lambda a: a[l].reshape(1, -1)

        oa_p, k_p, v_p = _attn_prompt(pa, attn_sink[l])
        oa_s, nk, nv = _attn_sample(pa, attn_sink[l], cache_k, cache_v, l)
        og_p, s_p = _gla_prompt(pg, pag, wa2_p[l], row2(gla_b_a), row2(gla_norm_g))
        og_s, s_s = _gla_sample(pg, pag, wa2_p[l], row2(gla_b_a), row2(gla_norm_g), state_gla, l)
        op_p, u_last = _pool_prompt(pu, pool_w_b[l], row2(pool_scale))
        u_s = pu[NP:].reshape(DEC_BATCH, DEC_SEQ, POOL_W)
        ext_s = jnp.concatenate(
            [jnp.zeros((DEC_BATCH, POOL_HIST - POOL_STATE, POOL_W), F32), state_pool[l], u_s], axis=1)
        op_s = _pool_sample(ext_s, pool_w_b[l], row2(pool_scale))

        xe, meta = _merge(x, (oa_p, oa_s, og_p, og_s, op_p, op_s), w_gate_b, w_branch_b, w_o_b,
                          ln1_g.reshape(DEPTH, 1, D_MODEL), ln1_b.reshape(DEPTH, 1, D_MODEL), wr_t, rb, l)
        cls = meta[:, 0, :].reshape(N_TOK // LANES, LANES).astype(I32)
        cls2d = jnp.pad(cls, ((0, PLAN_ROWS - N_TOK // LANES), (0, 0)), constant_values=-1)
        pos2d, tcls, nused = _plan(cls2d)
        pos = pos2d.reshape(-1)[:N_TOK].reshape(N_TILES, 1, TM)
        n_used = nused[0, :1]
        tile_cls = tcls[0, :NT_E]
        tile_cls = jnp.where(jnp.arange(NT_E) < n_used, tile_cls, tile_cls[n_used[0] - 1])
        tile_a = (tile_cls // len(PAIRS)) * EXPERTS_PER_GROUP + pair_lo[tile_cls % len(PAIRS)]
        tile_b = (tile_cls // len(PAIRS)) * EXPERTS_PER_GROUP + pair_hi[tile_cls % len(PAIRS)]
        first = jnp.ones((1,), I32)
        new_a = jnp.concatenate([first, (tile_a[1:] != tile_a[:-1]).astype(I32)])
        new_b = jnp.concatenate([first, (tile_b[1:] != tile_b[:-1]).astype(I32)])
        xs = _dispatch(pos, xe, xs)
        ys = _experts(tile_a, tile_b, new_a, new_b, n_used, xs, w_e_gate, w_e_up, w_e_down,
                      ln2_g.reshape(DEPTH, 1, D_MODEL), ln2_b.reshape(DEPTH, 1, D_MODEL), l)

        pk.append(k_p.reshape(BATCH, WINDOW, N_KV_HEADS, HEAD_DIM))
        pv.append(v_p.reshape(BATCH, WINDOW, N_KV_HEADS, HEAD_DIM))
        ps.append(s_p)
        pp.append(u_last[:, POOL_HIST - POOL_STATE:, :])
        sk.append(nk.reshape(DEC_BATCH, WINDOW, N_KV_HEADS, HEAD_DIM))
        sv.append(nv.reshape(DEC_BATCH, WINDOW, N_KV_HEADS, HEAD_DIM))
        ss.append(s_s)
        sp.append(ext_s[:, POOL_HIST + DEC_SEQ - POOL_STATE:, :])
    y_prompt, y_sample = _combine(pos, ys)
    return (y_prompt.reshape(BATCH, SEQ, D_MODEL), y_sample.reshape(DEC_BATCH, DEC_SEQ, D_MODEL),
            jnp.stack(pk), jnp.stack(pv), jnp.stack(ps), jnp.stack(pp),
            jnp.stack(sk), jnp.stack(sv), jnp.stack(ss), jnp.stack(sp))
```

```python
import jax
import jax.numpy as jnp
import numpy as np
from jax import lax
from jax.experimental import pallas as pl
from jax.experimental.pallas import tpu as pltpu

F32, BF16, I32 = jnp.float32, jnp.bfloat16, jnp.int32

D_MODEL = 1024
BATCH = 8
SEQ = 2048
DEPTH = 4
DEC_BATCH = 128
DEC_SEQ = 8
N_HEADS = 8
N_KV_HEADS = 2
HEAD_DIM = 64
WINDOW = 128
ATTN_W = N_HEADS * HEAD_DIM
KV_W = N_KV_HEADS * HEAD_DIM
Q_PER_KV = N_HEADS // N_KV_HEADS
GLA_HEADS = 4
GLA_DK = 64
GLA_DV = 128
GLA_KW = GLA_HEADS * GLA_DK
GLA_VW = GLA_HEADS * GLA_DV
GLA_RANK = 16
GLA_TAU = 16.0
GLA_CHUNK = 64
POOL_WINDOWS = (2, 4, 8, 16)
POOL_GROUPS = 4
POOL_GW = 128
POOL_W = POOL_GROUPS * POOL_GW
POOL_STATE = max(POOL_WINDOWS) - 1
N_BRANCH = 3
BRANCH_W = 512
N_EXPERTS = 16
N_GROUPS = 4
EXPERTS_PER_GROUP = N_EXPERTS // N_GROUPS
D_EXPERT = 512
DN_ALPHA = (2.0 * DEPTH) ** 0.25
LN_EPS = 1e-5
RMS_EPS = 1e-6
NEG_INF = -1e30
ALIBI_SLOPES = tuple(2.0 ** (-8.0 * h / N_HEADS) for h in range(1, N_HEADS + 1))

LANES = 128
SUBLANES = 8

NP = BATCH * SEQ
NS = DEC_BATCH * DEC_SEQ
N_TOK = NP + NS
TM = 512
N_TILES = N_TOK // TM
PT = NP // TM

PA_W = ATTN_W + 2 * KV_W
PG_W = 2 * GLA_KW + 2 * GLA_VW

PAIRS = ((0, 1), (0, 2), (0, 3), (1, 2), (1, 3), (2, 3))
N_CLASSES = N_GROUPS * len(PAIRS)
TM_E = 512
NT_E = -(-(N_TOK + N_CLASSES * (TM_E - 1)) // TM_E)
N_SORTED = NT_E * TM_E
XE_W = D_MODEL + LANES
PLAN_ROWS = 256

VMEM_LIMIT = 56 * 1024 * 1024
DMA_THREADS = 2


def _params(*sem):
    return pltpu.CompilerParams(dimension_semantics=sem, vmem_limit_bytes=VMEM_LIMIT)


def _dot(a, b):
    return jnp.dot(a, b, preferred_element_type=F32)


def _dot_nt(a, b):
    return lax.dot_general(a, b, (((1,), (1,)), ((), ())), preferred_element_type=F32)


def _dot_tn(a, b):
    return lax.dot_general(a, b, (((0,), (0,)), ((), ())), preferred_element_type=F32)


def _bdot_nt(a, b):
    return lax.dot_general(a, b, (((2,), (2,)), ((0,), (0,))), preferred_element_type=F32)


def _bdot(a, b):
    return lax.dot_general(a, b, (((2,), (1,)), ((0,), (0,))), preferred_element_type=F32)


def _layer_norm(h, g, b):
    mu = jnp.mean(h, axis=-1, keepdims=True)
    hc = h - mu
    var = jnp.mean(hc * hc, axis=-1, keepdims=True)
    return hc * lax.rsqrt(var + LN_EPS) * g + b


def _sigmoid(x):
    return 0.5 * jnp.tanh(0.5 * x) + 0.5


def _log_sigmoid(x):
    return jnp.minimum(x, 0.0) - jnp.log1p(jnp.exp(-jnp.abs(x)))


def _pair_specs(width):
    return (pl.BlockSpec((TM, width), lambda i, *_: (jnp.minimum(i, PT - 1), 0)),
            pl.BlockSpec((TM, width), lambda i, *_: (jnp.maximum(i - PT, 0), 0)))


def _pair_tile(p_ref, s_ref):
    return jnp.where(pl.program_id(0) < PT, p_ref[...], s_ref[...])


_IN_PROJ_WIDTHS = (PA_W, PG_W, POOL_W, LANES)


def _in_proj_tile(x, w_refs, out_refs):
    w_main, w_pool, w_rank = w_refs
    pa_ref, pg_ref, pu_ref, pag_ref = out_refs
    xb = x.astype(BF16)
    pa_ref[...] = _dot(xb, w_main[:, 0:PA_W])
    pg_ref[...] = _dot(xb, w_main[:, PA_W:PA_W + PG_W])
    pu_ref[...] = _dot(xb, w_pool[...])
    pag_ref[...] = _dot(xb, w_rank[...])


def _in_proj_kernel(xp_ref, xs_ref, wm_ref, wp_ref, wr_ref, pa_ref, pg_ref, pu_ref, pag_ref):
    _in_proj_tile(_pair_tile(xp_ref, xs_ref), (wm_ref, wp_ref, wr_ref), (pa_ref, pg_ref, pu_ref, pag_ref))


def _in_proj_gather_kernel(pos_ref, pos_next_ref, ys_hbm, wm_ref, wp_ref, wr_ref,
                           pa_ref, pg_ref, pu_ref, pag_ref, x_ref, buf, sem):
    i = pl.program_id(0)
    slot = i % 2

    def copy(p_ref, r, s):
        return pltpu.make_async_copy(ys_hbm.at[pl.ds(p_ref[0, r], 1)], buf.at[s, pl.ds(r, 1)], sem.at[s])

    def wait_tile(p_ref, s):
        def wait(r, carry):
            copy(p_ref, r, s).wait()
            return carry
        lax.fori_loop(0, TM, wait, 0, unroll=8)

    @pl.when(i == 0)
    def _():
        for r in range(TM):
            copy(pos_ref, r, 0).start(priority=r % DMA_THREADS)

    wait_tile(pos_ref, slot)
    x = buf[slot]
    for r in range(TM):
        copy(pos_next_ref, r, 1 - slot).start(priority=r % DMA_THREADS)
    x_ref[...] = x
    _in_proj_tile(x, (wm_ref, wp_ref, wr_ref), (pa_ref, pg_ref, pu_ref, pag_ref))

    @pl.when(i == pl.num_programs(0) - 1)
    def _():
        wait_tile(pos_next_ref, 1 - slot)


def _in_proj(x, w_in_parts, layer, pos=None):
    row = lambda w: pl.BlockSpec((TM, w), lambda i: (i, 0))
    w_specs = [pl.BlockSpec((None,) + w.shape[1:], lambda i: (layer, 0, 0)) for w in w_in_parts]
    outs = [jax.ShapeDtypeStruct((N_TOK, w), F32) for w in _IN_PROJ_WIDTHS]
    if pos is None:
        return pl.pallas_call(
            _in_proj_kernel,
            grid=(N_TILES,),
            in_specs=[*_pair_specs(D_MODEL), *w_specs],
            out_specs=[row(w) for w in _IN_PROJ_WIDTHS],
            out_shape=outs,
            compiler_params=_params("arbitrary"),
            name="in_proj",
        )(*x, *w_in_parts)
    pos_next = pl.BlockSpec((None, 1, TM), lambda i: (jnp.minimum(i + 1, N_TILES - 1), 0, 0), memory_space=pltpu.SMEM)
    return pl.pallas_call(
        _in_proj_gather_kernel,
        grid=(N_TILES,),
        in_specs=[_POS_SPEC, pos_next, _ANY, *w_specs],
        out_specs=[row(w) for w in _IN_PROJ_WIDTHS] + [row(D_MODEL)],
        out_shape=outs + [jax.ShapeDtypeStruct((N_TOK, D_MODEL), F32)],
        scratch_shapes=[pltpu.VMEM((2, TM, D_MODEL), F32), pltpu.SemaphoreType.DMA((2,))],
        compiler_params=_params("arbitrary"),
        name="in_proj_gather",
    )(pos, pos, x, *w_in_parts)


def _softmax_sink_pv(parts, sink, pv):
    m = sink
    for s, _ in parts:
        m = jnp.maximum(m, jnp.max(s, axis=-1, keepdims=True))
    den = jnp.exp(sink - m)
    es = []
    for s, _ in parts:
        e = jnp.exp(s - m)
        den = den + jnp.sum(e, axis=-1, keepdims=True)
        es.append(e)
    inv = 1.0 / den
    out = None
    for e, (_, v) in zip(es, parts):
        o = pv((e * inv).astype(BF16), v)
        out = o if out is None else out + o
    return out


def _head_column(sink_ref, kv, rows_per_head):
    g = lax.broadcasted_iota(I32, (Q_PER_KV * rows_per_head, 1), 0) // rows_per_head
    col = jnp.zeros(g.shape, F32)
    for i in range(Q_PER_KV):
        col = jnp.where(g == i, sink_ref[kv * Q_PER_KV + i], col)
    return col


def _alibi_bias(dist, visible):
    out = np.empty((N_KV_HEADS, Q_PER_KV * dist.shape[0], dist.shape[1]), np.float32)
    for h in range(N_HEADS):
        kv, g = divmod(h, Q_PER_KV)
        out[kv, g * dist.shape[0]:(g + 1) * dist.shape[0]] = np.where(visible, -ALIBI_SLOPES[h] * dist, NEG_INF)
    return out


ATTN_QB = 4


def _attn_prompt_kernel(sink_ref, cur_ref, prev_ref, bias0_ref, bias_ref, o_ref, nk_ref, nv_ref):
    k = jnp.concatenate([prev_ref[:, 0:KV_W], cur_ref[:, ATTN_W:ATTN_W + KV_W]], axis=0).astype(BF16)
    vt = jnp.concatenate([prev_ref[:, KV_W:2 * KV_W], cur_ref[:, ATTN_W + KV_W:PA_W]], axis=0).T.astype(BF16)
    g_of_col = lax.broadcasted_iota(I32, (1, Q_PER_KV * WINDOW), 1) // WINDOW
    for kv in range(N_KV_HEADS):
        ks = slice(kv * HEAD_DIM, (kv + 1) * HEAD_DIM)
        heads = range(kv * Q_PER_KV, (kv + 1) * Q_PER_KV)
        sink = jnp.zeros(g_of_col.shape, F32)
        for g, h in enumerate(heads):
            sink = jnp.where(g_of_col == g, sink_ref[h], sink)
        for j in range(ATTN_QB):
            rows = slice(j * WINDOW, (j + 1) * WINDOW)
            keys = slice(j * WINDOW, (j + 2) * WINDOW)
            q = jnp.concatenate([cur_ref[rows, h * HEAD_DIM:(h + 1) * HEAD_DIM] for h in heads], axis=0)
            bias = bias0_ref[kv] if j == 0 else bias_ref[kv]
            st = _dot_nt(k[keys, ks], (q * (HEAD_DIM ** -0.5)).astype(BF16)) + bias
            m = jnp.maximum(sink, jnp.max(st, axis=0, keepdims=True))
            e = jnp.exp(st - m)
            inv = 1.0 / (jnp.exp(sink - m) + jnp.sum(e, axis=0, keepdims=True))
            ot = _dot(vt[ks, keys], (e * inv).astype(BF16))
            for pair in range(Q_PER_KV // 2):
                two = jnp.concatenate(
                    [ot[:, (2 * pair + g) * WINDOW:(2 * pair + g + 1) * WINDOW] for g in range(2)], axis=0)
                lo = (kv * Q_PER_KV + 2 * pair) * HEAD_DIM
                o_ref[rows, lo:lo + 2 * HEAD_DIM] = two.T.astype(BF16)

    @pl.when(pl.program_id(1) == pl.num_programs(1) - 1)
    def _():
        last = slice((ATTN_QB - 1) * WINDOW, ATTN_QB * WINDOW)
        nk_ref[...] = cur_ref[last, ATTN_W:ATTN_W + KV_W]
        nv_ref[...] = cur_ref[last, ATTN_W + KV_W:PA_W]


def _attn_prompt(pa, sink):
    nb = SEQ // WINDOW
    ns = nb // ATTN_QB
    r = np.arange(WINDOW)[:, None]
    c = np.arange(2 * WINDOW)[None, :]
    band = (c > r) & (c <= WINDOW + r)
    bias = np.stack([_alibi_bias(WINDOW + r - c, band & (c >= WINDOW)), _alibi_bias(WINDOW + r - c, band)])
    bias = jnp.asarray(bias.transpose(0, 1, 3, 2))
    bias_spec = lambda variant: pl.BlockSpec((None, N_KV_HEADS, 2 * WINDOW, Q_PER_KV * WINDOW),
                                             lambda b, i, s: (variant(i), 0, 0, 0))
    state = pl.BlockSpec((None, WINDOW, KV_W), lambda b, i, s: (b, 0, 0))
    return pl.pallas_call(
        _attn_prompt_kernel,
        grid_spec=pltpu.PrefetchScalarGridSpec(
            num_scalar_prefetch=1,
            grid=(BATCH, ns),
            in_specs=[
                pl.BlockSpec((ATTN_QB * WINDOW, PA_W), lambda b, i, s: (b * ns + i, 0)),
                pl.BlockSpec((WINDOW, 2 * KV_W),
                             lambda b, i, s: (b * nb + jnp.maximum(ATTN_QB * i - 1, 0), ATTN_W // (2 * KV_W))),
                bias_spec(lambda i: jnp.minimum(i, 1)), bias_spec(lambda i: 1),
            ],
            out_specs=[pl.BlockSpec((ATTN_QB * WINDOW, ATTN_W), lambda b, i, s: (b * ns + i, 0)), state, state],
        ),
        out_shape=[jax.ShapeDtypeStruct((NP, ATTN_W), BF16),
                   jax.ShapeDtypeStruct((BATCH, WINDOW, KV_W), F32),
                   jax.ShapeDtypeStruct((BATCH, WINDOW, KV_W), F32)],
        compiler_params=_params("arbitrary", "arbitrary"),
        name="attn_prompt",
    )(sink, pa, pa, bias, bias)


ATTN_SB = 16


def _attn_sample_kernel(sink_ref, cur_ref, kc_ref, vc_ref, bias_c_ref, bias_n_ref, o_ref, nk_ref, nv_ref):
    cur = cur_ref[...].reshape(ATTN_SB, DEC_SEQ, PA_W)
    for kv in range(N_KV_HEADS):
        ks = slice(kv * HEAD_DIM, (kv + 1) * HEAD_DIM)
        heads = range(kv * Q_PER_KV, (kv + 1) * Q_PER_KV)
        q = jnp.concatenate([cur[:, :, h * HEAD_DIM:(h + 1) * HEAD_DIM] for h in heads], axis=1)
        q = (q * (HEAD_DIM ** -0.5)).astype(BF16)
        kn = cur[:, :, ATTN_W + kv * HEAD_DIM:ATTN_W + (kv + 1) * HEAD_DIM].astype(BF16)
        vn = cur[:, :, ATTN_W + KV_W + kv * HEAD_DIM:ATTN_W + KV_W + (kv + 1) * HEAD_DIM].astype(BF16)
        sc = _bdot_nt(q, kc_ref[:, :, ks].astype(BF16)) + bias_c_ref[kv]
        sn = _bdot_nt(q, kn) + bias_n_ref[kv]
        sink = _head_column(sink_ref, kv, DEC_SEQ)
        o = _softmax_sink_pv([(sc, vc_ref[:, :, ks].astype(BF16)), (sn, vn)], sink, _bdot)
        for g, h in enumerate(heads):
            o_ref[:, h * HEAD_DIM:(h + 1) * HEAD_DIM] = (
                o[:, g * DEC_SEQ:(g + 1) * DEC_SEQ, :].reshape(ATTN_SB * DEC_SEQ, HEAD_DIM).astype(BF16))
    keep = WINDOW - DEC_SEQ
    nk_ref[:, 0:keep, :] = kc_ref[:, DEC_SEQ:WINDOW, :]
    nk_ref[:, keep:WINDOW, :] = cur[:, :, ATTN_W:ATTN_W + KV_W]
    nv_ref[:, 0:keep, :] = vc_ref[:, DEC_SEQ:WINDOW, :]
    nv_ref[:, keep:WINDOW, :] = cur[:, :, ATTN_W + KV_W:PA_W]


def _attn_sample(pa, sink, cache_k, cache_v, layer):
    rows = ATTN_SB * DEC_SEQ
    first = NP // rows
    t = np.arange(DEC_SEQ)[:, None]
    jc = np.arange(WINDOW)[None, :]
    jn = np.arange(DEC_SEQ)[None, :]
    bias_c = jnp.asarray(_alibi_bias(WINDOW + t - jc, jc > t))
    bias_n = jnp.asarray(_alibi_bias(t - jn, jn <= t))
    cache = pl.BlockSpec((None, ATTN_SB, WINDOW, KV_W), lambda j, s: (layer, j, 0, 0))
    new = pl.BlockSpec((ATTN_SB, WINDOW, KV_W), lambda j, s: (j, 0, 0))
    const = lambda a: pl.BlockSpec(a.shape, lambda j, s: (0,) * a.ndim)
    return pl.pallas_call(
        _attn_sample_kernel,
        grid_spec=pltpu.PrefetchScalarGridSpec(
            num_scalar_prefetch=1,
            grid=(DEC_BATCH // ATTN_SB,),
            in_specs=[pl.BlockSpec((rows, PA_W), lambda j, s: (first + j, 0)), cache, cache,
                      const(bias_c), const(bias_n)],
            out_specs=[pl.BlockSpec((rows, ATTN_W), lambda j, s: (j, 0)), new, new],
        ),
        out_shape=[jax.ShapeDtypeStruct((NS, ATTN_W), BF16),
                   jax.ShapeDtypeStruct((DEC_BATCH, WINDOW, KV_W), F32),
                   jax.ShapeDtypeStruct((DEC_BATCH, WINDOW, KV_W), F32)],
        compiler_params=_params("arbitrary"),
        name="attn_sample",
    )(sink, pa, cache_k, cache_v, bias_c, bias_n)


def _chunk_cumsum(x, chunk):
    pos = lax.broadcasted_iota(I32, x.shape, 0) % chunk
    sh = 1
    while sh < chunk:
        x = x + jnp.where(pos >= sh, pltpu.roll(x, sh, 0), 0.0)
        sh *= 2
    return x


def _gla_log_decay(pag_ref, wa2_ref, ba_ref):
    z = _dot(pag_ref[...].astype(BF16), wa2_ref[...]) + ba_ref[...]
    return _log_sigmoid(z) / GLA_TAU


def _gla_finish(o, rg, g):
    o = o * lax.rsqrt(jnp.mean(o * o, axis=-1, keepdims=True) + RMS_EPS)
    return (o * g * (rg * _sigmoid(rg))).astype(BF16)


GLA_TT = 512
GLA_AG = 128


def _gla_prompt_kernel(pg_ref, pag_ref, wa2_ref, ba_ref, g_ref, o_ref, s_ref,
                       qd_ref, kd_ref, kdec_ref, a_ref, sall_ref, st_ref, acc_ref):
    nc = GLA_TT // GLA_CHUNK

    @pl.when(pl.program_id(1) == 0)
    def _():
        st_ref[...] = jnp.zeros_like(st_ref)

    cum = _chunk_cumsum(_gla_log_decay(pag_ref, wa2_ref, ba_ref), GLA_CHUNK)
    cum3 = cum.reshape(nc, GLA_CHUNK, GLA_KW)
    tot3 = cum3[:, GLA_CHUNK - 1:GLA_CHUNK, :]
    k = pg_ref[:, GLA_KW:2 * GLA_KW]
    qd_ref[...] = (pg_ref[:, 0:GLA_KW] * (GLA_DK ** -0.5) * jnp.exp(cum)).astype(BF16)
    kd_ref[...] = (k * jnp.exp(-cum)).astype(BF16)
    kdec_ref[...] = (k * jnp.exp(tot3 - cum3).reshape(GLA_TT, GLA_KW)).astype(BF16)
    etot = jnp.exp(tot3)

    def hs(h):
        return slice(h * GLA_DK, (h + 1) * GLA_DK)

    def vs(h):
        return slice(2 * GLA_KW + h * GLA_DV, 2 * GLA_KW + (h + 1) * GLA_DV)

    for c in range(nc):
        rows = slice(c * GLA_CHUNK, (c + 1) * GLA_CHUNK)
        for h in range(GLA_HEADS):
            a_ref[c, :, hs(h)] = _dot_tn(pg_ref[rows, vs(h)].astype(BF16), kdec_ref[rows, hs(h)])

    st = st_ref[...]
    for c in range(nc):
        sall_ref[c] = st.astype(BF16)
        st = etot[c] * st + a_ref[c]
    st_ref[...] = st

    r = lax.broadcasted_iota(I32, (GLA_AG, GLA_AG), 0)
    col = lax.broadcasted_iota(I32, (GLA_AG, GLA_AG), 1)
    causal = (r // GLA_CHUNK == col // GLA_CHUNK) & (col <= r)
    for h in range(GLA_HEADS):
        out = slice(h * GLA_DV, (h + 1) * GLA_DV)
        for c in range(nc):
            rows = slice(c * GLA_CHUNK, (c + 1) * GLA_CHUNK)
            acc_ref[rows, out] = _dot_nt(qd_ref[rows, hs(h)], sall_ref[c, :, hs(h)])
        for a in range(GLA_TT // GLA_AG):
            rows = slice(a * GLA_AG, (a + 1) * GLA_AG)
            att = jnp.where(causal, _dot_nt(qd_ref[rows, hs(h)], kd_ref[rows, hs(h)]), 0.0)
            acc_ref[rows, out] += _dot(att.astype(BF16), pg_ref[rows, vs(h)].astype(BF16))
        rg = pg_ref[:, 2 * GLA_KW + GLA_VW + h * GLA_DV:2 * GLA_KW + GLA_VW + (h + 1) * GLA_DV]
        o_ref[:, out] = _gla_finish(acc_ref[:, out], rg, g_ref[:, out])

    @pl.when(pl.program_id(1) == pl.num_programs(1) - 1)
    def _():
        for h in range(GLA_HEADS):
            s_ref[h] = st_ref[:, hs(h)].T


def _gla_prompt(pg, pag, wa2, ba, g):
    nt = SEQ // GLA_TT
    nc = GLA_TT // GLA_CHUNK
    const = lambda shape: pl.BlockSpec(shape, lambda b, j: (0,) * len(shape))
    return pl.pallas_call(
        _gla_prompt_kernel,
        grid=(BATCH, nt),
        in_specs=[
            pl.BlockSpec((GLA_TT, PG_W), lambda b, j: (b * nt + j, 0)),
            pl.BlockSpec((GLA_TT, LANES), lambda b, j: (b * nt + j, 0)),
            const((LANES, GLA_KW)), const((1, GLA_KW)), const((1, GLA_VW)),
        ],
        out_specs=[
            pl.BlockSpec((GLA_TT, GLA_VW), lambda b, j: (b * nt + j, 0)),
            pl.BlockSpec((None, GLA_HEADS, GLA_DK, GLA_DV), lambda b, j: (b, 0, 0, 0)),
        ],
        out_shape=[jax.ShapeDtypeStruct((NP, GLA_VW), BF16),
                   jax.ShapeDtypeStruct((BATCH, GLA_HEADS, GLA_DK, GLA_DV), F32)],
        scratch_shapes=[pltpu.VMEM((GLA_TT, GLA_KW), BF16), pltpu.VMEM((GLA_TT, GLA_KW), BF16),
                        pltpu.VMEM((GLA_TT, GLA_KW), BF16),
                        pltpu.VMEM((nc, GLA_DV, GLA_KW), F32), pltpu.VMEM((nc, GLA_DV, GLA_KW), BF16),
                        pltpu.VMEM((GLA_DV, GLA_KW), F32), pltpu.VMEM((GLA_TT, GLA_VW), F32)],
        compiler_params=_params("arbitrary", "arbitrary"),
        name="gla_prompt",
    )(pg, pag, wa2, ba, g)


GLA_SB = 8


def _gla_sample_kernel(pg_ref, pag_ref, wa2_ref, ba_ref, g_ref, s0_ref, o_ref, s_ref):
    cum_all = _chunk_cumsum(_gla_log_decay(pag_ref, wa2_ref, ba_ref), DEC_SEQ)
    tri = (lax.broadcasted_iota(I32, (DEC_SEQ, DEC_SEQ), 0) >= lax.broadcasted_iota(I32, (DEC_SEQ, DEC_SEQ), 1))
    for s in range(GLA_SB):
        rows = slice(s * DEC_SEQ, (s + 1) * DEC_SEQ)
        cum = cum_all[rows, :]
        tot = cum[DEC_SEQ - 1:DEC_SEQ, :]
        q = pg_ref[rows, 0:GLA_KW] * (GLA_DK ** -0.5)
        k = pg_ref[rows, GLA_KW:2 * GLA_KW]
        qd = (q * jnp.exp(cum)).astype(BF16)
        kd = (k * jnp.exp(-cum)).astype(BF16)
        kdec = (k * jnp.exp(tot - cum)).astype(BF16)
        etot = jnp.exp(tot)
        etot_col = [jnp.broadcast_to(etot[:, p * LANES:(p + 1) * LANES], (SUBLANES, LANES)).T[:, 0:1]
                    for p in range(GLA_KW // LANES)]
        for h in range(GLA_HEADS):
            ks = slice(h * GLA_DK, (h + 1) * GLA_DK)
            vs = slice(2 * GLA_KW + h * GLA_DV, 2 * GLA_KW + (h + 1) * GLA_DV)
            rs = slice(2 * GLA_KW + GLA_VW + h * GLA_DV, 2 * GLA_KW + GLA_VW + (h + 1) * GLA_DV)
            v = pg_ref[rows, vs].astype(BF16)
            st = s0_ref[s, h]
            att = jnp.where(tri, _dot_nt(qd[:, ks], kd[:, ks]), 0.0)
            o = _dot(qd[:, ks], st.astype(BF16)) + _dot(att.astype(BF16), v)
            per = LANES // GLA_DK
            col = etot_col[h // per][(h % per) * GLA_DK:(h % per + 1) * GLA_DK, :]
            s_ref[s, h] = col * st + _dot_tn(kdec[:, ks], v)
            o_ref[rows, h * GLA_DV:(h + 1) * GLA_DV] = _gla_finish(o, pg_ref[rows, rs], g_ref[:, h * GLA_DV:(h + 1) * GLA_DV])


def _gla_sample(pg, pag, wa2, ba, g, state, layer):
    rows = GLA_SB * DEC_SEQ
    first = NP // rows
    const = lambda shape: pl.BlockSpec(shape, lambda j: (0,) * len(shape))
    return pl.pallas_call(
        _gla_sample_kernel,
        grid=(DEC_BATCH // GLA_SB,),
        in_specs=[
            pl.BlockSpec((rows, PG_W), lambda j: (first + j, 0)),
            pl.BlockSpec((rows, LANES), lambda j: (first + j, 0)),
            const((LANES, GLA_KW)), const((1, GLA_KW)), const((1, GLA_VW)),
            pl.BlockSpec((None, GLA_SB, GLA_HEADS, GLA_DK, GLA_DV), lambda j: (layer, j, 0, 0, 0)),
        ],
        out_specs=[
            pl.BlockSpec((rows, GLA_VW), lambda j: (j, 0)),
            pl.BlockSpec((GLA_SB, GLA_HEADS, GLA_DK, GLA_DV), lambda j: (j, 0, 0, 0)),
        ],
        out_shape=[jax.ShapeDtypeStruct((NS, GLA_VW), BF16),
                   jax.ShapeDtypeStruct((DEC_BATCH, GLA_HEADS, GLA_DK, GLA_DV), F32)],
        compiler_params=_params("arbitrary"),
        name="gla_sample",
    )(pg, pag, wa2, ba, g, state)


POOL_HIST = 16


def _pool_groups(ext, cnt, pw_ref, ps_ref, out_rows):
    ax = ext.ndim - 2
    outs = []
    for g, w in enumerate(POOL_WINDOWS):
        x = ext[..., g * POOL_GW:(g + 1) * POOL_GW]
        s, sh = x, 1
        while sh < w:
            s = s + pltpu.roll(s, sh, ax)
            sh *= 2
        if ext.ndim == 3:
            d = (s[:, POOL_HIST:, :] / cnt[g] - x[:, POOL_HIST:, :]).reshape(out_rows, POOL_GW)
        else:
            d = s[POOL_HIST:, :] / cnt[g] - x[POOL_HIST:, :]
        y = _dot(d.astype(BF16), pw_ref[g]) * ps_ref[:, g * POOL_GW:(g + 1) * POOL_GW]
        outs.append(y.astype(BF16))
    return outs


POOL_TT = 512


def _pool_prompt_kernel(pu_ref, pw_ref, ps_ref, o_ref, last_ref, hist_ref):
    j = pl.program_id(1)

    @pl.when(j == 0)
    def _():
        hist_ref[...] = jnp.zeros_like(hist_ref)

    u = pu_ref[...]
    ext = jnp.concatenate([hist_ref[...], u], axis=0)
    pos = j * POOL_TT + lax.broadcasted_iota(I32, (POOL_TT, 1), 0)
    cnt = [jnp.minimum(pos + 1, w).astype(F32) for w in POOL_WINDOWS]
    for g, y in enumerate(_pool_groups(ext, cnt, pw_ref, ps_ref, POOL_TT)):
        o_ref[:, g * POOL_GW:(g + 1) * POOL_GW] = y
    hist_ref[...] = u[POOL_TT - POOL_HIST:, :]

    @pl.when(j == pl.num_programs(1) - 1)
    def _():
        last_ref[...] = u[POOL_TT - POOL_HIST:, :]


def _pool_prompt(pu, pw, ps):
    nt = SEQ // POOL_TT
    return pl.pallas_call(
        _pool_prompt_kernel,
        grid=(BATCH, nt),
        in_specs=[
            pl.BlockSpec((POOL_TT, POOL_W), lambda b, j: (b * nt + j, 0)),
            pl.BlockSpec((POOL_GROUPS, POOL_GW, POOL_GW), lambda b, j: (0, 0, 0)),
            pl.BlockSpec((1, POOL_W), lambda b, j: (0, 0)),
        ],
        out_specs=[pl.BlockSpec((POOL_TT, POOL_W), lambda b, j: (b * nt + j, 0)),
                   pl.BlockSpec((None, POOL_HIST, POOL_W), lambda b, j: (b, 0, 0))],
        out_shape=[jax.ShapeDtypeStruct((NP, POOL_W), BF16),
                   jax.ShapeDtypeStruct((BATCH, POOL_HIST, POOL_W), F32)],
        scratch_shapes=[pltpu.VMEM((POOL_HIST, POOL_W), F32)],
        compiler_params=_params("arbitrary", "arbitrary"),
        name="pool_prompt",
    )(pu, pw, ps)


POOL_SB = 16


def _pool_sample_kernel(ext_ref, pw_ref, ps_ref, o_ref):
    cnt = [float(w) for w in POOL_WINDOWS]
    for g, y in enumerate(_pool_groups(ext_ref[...], cnt, pw_ref, ps_ref, POOL_SB * DEC_SEQ)):
        o_ref[:, g * POOL_GW:(g + 1) * POOL_GW] = y


def _pool_sample(ext, pw, ps):
    rows = POOL_SB * DEC_SEQ
    return pl.pallas_call(
        _pool_sample_kernel,
        grid=(DEC_BATCH // POOL_SB,),
        in_specs=[
            pl.BlockSpec((POOL_SB, POOL_HIST + DEC_SEQ, POOL_W), lambda j: (j, 0, 0)),
            pl.BlockSpec((POOL_GROUPS, POOL_GW, POOL_GW), lambda j: (0, 0, 0)),
            pl.BlockSpec((1, POOL_W), lambda j: (0, 0)),
        ],
        out_specs=pl.BlockSpec((rows, POOL_W), lambda j: (j, 0)),
        out_shape=jax.ShapeDtypeStruct((NS, POOL_W), BF16),
        compiler_params=_params("arbitrary"),
        name="pool_sample",
    )(ext, pw, ps)


def _route(sc, sel):
    gscore = []
    for g in range(N_GROUPS):
        v = sel[EXPERTS_PER_GROUP * g:EXPERTS_PER_GROUP * (g + 1)]
        best = None
        for a, b in PAIRS:
            pair = v[a] + v[b]
            best = pair if best is None else jnp.maximum(best, pair)
        gscore.append(best)
    gi = jnp.zeros_like(gscore[0], dtype=I32)
    best = gscore[0]
    for g in range(1, N_GROUPS):
        upd = gscore[g] > best
        gi = jnp.where(upd, g, gi)
        best = jnp.where(upd, gscore[g], best)

    def in_group(rows, j):
        out = rows[(N_GROUPS - 1) * EXPERTS_PER_GROUP + j]
        for g in range(N_GROUPS - 2, -1, -1):
            out = jnp.where(gi == g, rows[g * EXPERTS_PER_GROUP + j], out)
        return out

    u = [in_group(sel, j) for j in range(EXPERTS_PER_GROUP)]
    s_in = [in_group(sc, j) for j in range(EXPERTS_PER_GROUP)]

    def argmax4(vals):
        idx = jnp.zeros_like(gi)
        m = vals[0]
        for j in range(1, EXPERTS_PER_GROUP):
            upd = vals[j] > m
            idx = jnp.where(upd, j, idx)
            m = jnp.where(upd, vals[j], m)
        return idx

    def pick(vals, idx):
        out = vals[EXPERTS_PER_GROUP - 1]
        for j in range(EXPERTS_PER_GROUP - 2, -1, -1):
            out = jnp.where(idx == j, vals[j], out)
        return out

    i1 = argmax4(u)
    i2 = argmax4([jnp.where(i1 == j, NEG_INF, u[j]) for j in range(EXPERTS_PER_GROUP)])
    w1, w2 = pick(s_in, i1), pick(s_in, i2)
    tot = w1 + w2
    w1, w2 = w1 / tot, w2 / tot
    lo, hi = jnp.minimum(i1, i2), jnp.maximum(i1, i2)
    first_lo = i1 < i2
    w_lo, w_hi = jnp.where(first_lo, w1, w2), jnp.where(first_lo, w2, w1)
    pair = jnp.where(lo == 0, hi - 1, jnp.where(lo == 1, hi + 1, 5))
    return gi * len(PAIRS) + pair, w_lo, w_hi


def _merge_kernel(*refs):
    (oap_ref, oas_ref, ogp_ref, ogs_ref, opp_ref, ops_ref, wg_ref, wb_ref, wo_ref,
     g1_ref, b1_ref, wr_ref, rb_ref, xe_ref, meta_ref) = refs[-15:]
    x = refs[0][...] if len(refs) == 16 else _pair_tile(refs[0], refs[1])
    xb = x.astype(BF16)
    merged = None
    for n, (brp, brs) in enumerate(((oap_ref, oas_ref), (ogp_ref, ogs_ref), (opp_ref, ops_ref))):
        gate = _sigmoid(_dot(xb, wg_ref[:, n * D_MODEL:(n + 1) * D_MODEL]))
        term = gate * _dot(_pair_tile(brp, brs), wb_ref[n])
        merged = term if merged is None else merged + term
    mix = _dot(merged.astype(BF16), wo_ref[...])
    x1 = _layer_norm(DN_ALPHA * x + mix, g1_ref[...], b1_ref[...])
    xe_ref[:, 0:D_MODEL] = x1
    sc_t = _sigmoid(_dot_nt(wr_ref[...], x1.astype(BF16)))
    sel_t = sc_t + rb_ref[...]
    sc = [sc_t[e:e + 1, :] for e in range(N_EXPERTS)]
    sel = [sel_t[e:e + 1, :] for e in range(N_EXPERTS)]
    cls, w_lo, w_hi = _route(sc, sel)
    cls = cls.astype(F32)

    def rows(n):
        rid = lax.broadcasted_iota(I32, (n, TM), 0)
        return jnp.where(rid == 0, cls, jnp.where(rid == 1, w_lo, jnp.where(rid == 2, w_hi, 0.0)))

    meta_ref[...] = rows(SUBLANES)
    xe_ref[:, D_MODEL:XE_W] = rows(LANES).T


def _merge(x, branches, wg, wb, wo, g1, b1, wr_t, rb, layer):
    row = lambda w: pl.BlockSpec((TM, w), lambda i: (i, 0))
    lay = lambda *shape: pl.BlockSpec((None,) + shape, lambda i: (layer,) + (0,) * len(shape))
    const = lambda *shape: pl.BlockSpec(shape, lambda i: (0,) * len(shape))
    return pl.pallas_call(
        _merge_kernel,
        grid=(N_TILES,),
        in_specs=[*(_pair_specs(D_MODEL) if isinstance(x, tuple) else (row(D_MODEL),)),
                  *(_pair_specs(BRANCH_W) * N_BRANCH),
                  lay(D_MODEL, N_BRANCH * D_MODEL), lay(N_BRANCH, BRANCH_W, D_MODEL), lay(D_MODEL, D_MODEL),
                  lay(1, D_MODEL), lay(1, D_MODEL), const(N_EXPERTS, D_MODEL), const(N_EXPERTS, 1)],
        out_specs=[row(XE_W), pl.BlockSpec((None, SUBLANES, TM), lambda i: (i, 0, 0))],
        out_shape=[jax.ShapeDtypeStruct((N_TOK, XE_W), F32), jax.ShapeDtypeStruct((N_TILES, SUBLANES, TM), F32)],
        compiler_params=_params("arbitrary"),
        name="merge",
    )(*(x if isinstance(x, tuple) else (x,)), *branches, wg, wb, wo, g1, b1, wr_t, rb)


def _plan_kernel(cls_ref, pos_ref, tcls_ref, nused_ref):
    cls = cls_ref[...]
    lane_r = lax.broadcasted_iota(I32, (LANES, 2 * LANES), 0)
    lane_c = lax.broadcasted_iota(I32, (LANES, 2 * LANES), 1)
    lane_mat = ((lane_c >= LANES) | (lane_r < lane_c)).astype(BF16)
    row_r = lax.broadcasted_iota(I32, (2 * PLAN_ROWS, PLAN_ROWS), 0)
    row_c = lax.broadcasted_iota(I32, (2 * PLAN_ROWS, PLAN_ROWS), 1)
    row_mat = ((row_r >= PLAN_ROWS) | (row_c < row_r)).astype(BF16)
    tile_start = (lax.broadcasted_iota(I32, (SUBLANES, LANES), 1) * TM_E).astype(F32)
    pos = jnp.zeros((PLAN_ROWS, LANES), F32)
    off = jnp.zeros((PLAN_ROWS, LANES), F32)
    tcls = jnp.zeros((SUBLANES, LANES), I32)
    for c in range(N_CLASSES):
        m = cls == c
        lanes = _dot(m.astype(BF16), lane_mat)
        rows = _dot(row_mat, lanes[:, LANES:].astype(BF16))
        rank = lanes[:, 0:LANES] + rows[0:PLAN_ROWS]
        count = rows[PLAN_ROWS:]
        pos = jnp.where(m, off + rank, pos)
        off = off + jnp.ceil(count * (1.0 / TM_E)) * TM_E
        tcls = tcls + (off[0:SUBLANES] <= tile_start).astype(I32)
    pos_ref[...] = pos.astype(I32)
    tcls_ref[...] = tcls
    nused_ref[...] = (off[0:SUBLANES] * (1.0 / TM_E)).astype(I32)


def _plan(cls2d):
    return pl.pallas_call(
        _plan_kernel,
        out_shape=[jax.ShapeDtypeStruct((PLAN_ROWS, LANES), I32),
                   jax.ShapeDtypeStruct((SUBLANES, LANES), I32),
                   jax.ShapeDtypeStruct((SUBLANES, LANES), I32)],
        compiler_params=pltpu.CompilerParams(vmem_limit_bytes=VMEM_LIMIT),
        name="plan",
    )(cls2d)


def _row_copies(pos_ref, tile_ref, sorted_hbm, sem, scatter):
    def copy(r):
        row, srt = tile_ref.at[pl.ds(r, 1)], sorted_hbm.at[pl.ds(pos_ref[0, r], 1)]
        return pltpu.make_async_copy(row, srt, sem) if scatter else pltpu.make_async_copy(srt, row, sem)

    def wait(r, carry):
        copy(r).wait()
        return carry

    for r in range(TM):
        copy(r).start(priority=r % DMA_THREADS)
    lax.fori_loop(0, TM, wait, 0, unroll=8)


def _dispatch_kernel(pos_ref, x_ref, xs_in_hbm, xs_hbm, sem):
    del xs_in_hbm
    _row_copies(pos_ref, x_ref, xs_hbm, sem, scatter=True)


def _combine_kernel(pos_ref, ys_hbm, xp_ref, xs_ref, sem):
    @pl.when(pl.program_id(0) < PT)
    def _():
        _row_copies(pos_ref, xp_ref, ys_hbm, sem, scatter=False)

    @pl.when(pl.program_id(0) >= PT)
    def _():
        _row_copies(pos_ref, xs_ref, ys_hbm, sem, scatter=False)


_POS_SPEC = pl.BlockSpec((None, 1, TM), lambda i: (i, 0, 0), memory_space=pltpu.SMEM)
_ANY = pl.BlockSpec(memory_space=pl.ANY)


def _dispatch(pos, xe, xs_prev):
    return pl.pallas_call(
        _dispatch_kernel,
        grid=(N_TILES,),
        in_specs=[_POS_SPEC, pl.BlockSpec((TM, XE_W), lambda i: (i, 0)), _ANY],
        out_specs=_ANY,
        out_shape=jax.ShapeDtypeStruct((N_SORTED, XE_W), F32),
        scratch_shapes=[pltpu.SemaphoreType.DMA(())],
        input_output_aliases={2: 0},
        compiler_params=_params("arbitrary"),
        name="dispatch",
    )(pos, xe, xs_prev)


def _combine(pos, ys):
    return pl.pallas_call(
        _combine_kernel,
        grid=(N_TILES,),
        in_specs=[_POS_SPEC, _ANY],
        out_specs=list(_pair_specs(D_MODEL)),
        out_shape=[jax.ShapeDtypeStruct((NP, D_MODEL), F32), jax.ShapeDtypeStruct((NS, D_MODEL), F32)],
        scratch_shapes=[pltpu.SemaphoreType.DMA(())],
        compiler_params=_params("arbitrary"),
        name="combine",
    )(pos, ys)


def _experts_kernel(ea_ref, eb_ref, new_a_ref, new_b_ref, nused_ref, xs_ref,
                    wga_ref, wua_ref, wda_ref, wgb_ref, wub_ref, wdb_ref, g2_ref, b2_ref, ys_ref,
                    wga_s, wua_s, wda_s, wgb_s, wub_s, wdb_s):
    del ea_ref, eb_ref
    i = pl.program_id(0)
    used = i < nused_ref[0]

    @pl.when(jnp.logical_not(used))
    def _():
        ys_ref[...] = jnp.zeros_like(ys_ref)

    @pl.when(new_a_ref[i] == 1)
    def _():
        for src, dst in ((wga_ref, wga_s), (wua_ref, wua_s), (wda_ref, wda_s)):
            dst[...] = src[...].astype(BF16)

    @pl.when(new_b_ref[i] == 1)
    def _():
        for src, dst in ((wgb_ref, wgb_s), (wub_ref, wub_s), (wdb_ref, wdb_s)):
            dst[...] = src[...].astype(BF16)

    @pl.when(used)
    def _():
        x1 = xs_ref[:, 0:D_MODEL]
        xb = x1.astype(BF16)

        def expert(wg, wu, wd):
            a = _dot(xb, wg[...])
            h = a * _sigmoid(a) * _dot(xb, wu[...])
            return _dot(h.astype(BF16), wd[...])

        ffn = xs_ref[:, D_MODEL + 1:D_MODEL + 2] * expert(wga_s, wua_s, wda_s)
        ffn = ffn + xs_ref[:, D_MODEL + 2:D_MODEL + 3] * expert(wgb_s, wub_s, wdb_s)
        ys_ref[...] = _layer_norm(DN_ALPHA * x1 + ffn, g2_ref[...], b2_ref[...])


def _experts(tile_a, tile_b, new_a, new_b, nused, xs, w_gate, w_up, w_down, g2, b2, layer):
    tile = lambda w: pl.BlockSpec((TM_E, w), lambda i, ea, eb, na, nb, nu: (jnp.minimum(i, nu[0] - 1), 0))
    wa = lambda *shape: pl.BlockSpec((None, None) + shape, lambda i, ea, eb, na, nb, nu: (layer, ea[i], 0, 0))
    wb = lambda *shape: pl.BlockSpec((None, None) + shape, lambda i, ea, eb, na, nb, nu: (layer, eb[i], 0, 0))
    lay = pl.BlockSpec((None, 1, D_MODEL), lambda i, ea, eb, na, nb, nu: (layer, 0, 0))
    up, down = pltpu.VMEM((D_MODEL, D_EXPERT), BF16), pltpu.VMEM((D_EXPERT, D_MODEL), BF16)
    return pl.pallas_call(
        _experts_kernel,
        grid_spec=pltpu.PrefetchScalarGridSpec(
            num_scalar_prefetch=5,
            grid=(NT_E,),
            in_specs=[tile(XE_W),
                      wa(D_MODEL, D_EXPERT), wa(D_MODEL, D_EXPERT), wa(D_EXPERT, D_MODEL),
                      wb(D_MODEL, D_EXPERT), wb(D_MODEL, D_EXPERT), wb(D_EXPERT, D_MODEL),
                      lay, lay],
            out_specs=pl.BlockSpec((TM_E, D_MODEL), lambda i, ea, eb, na, nb, nu: (i, 0)),
            scratch_shapes=[up, up, down, up, up, down],
        ),
        out_shape=jax.ShapeDtypeStruct((N_SORTED, D_MODEL), F32),
        compiler_params=_params("arbitrary"),
        name="experts",
    )(tile_a, tile_b, new_a, new_b, nused, xs, w_gate, w_up, w_down, w_gate, w_up, w_down, g2, b2)


_PAIR_LO = tuple(a for a, _ in PAIRS)
_PAIR_HI = tuple(b for _, b in PAIRS)


def kernel(x_prompt, x_sample, cache_attn_k, cache_attn_v, state_gla, state_pool, w_in, w_gate, attn_sink,
           gla_w_a2, gla_b_a, gla_norm_g, pool_w, pool_scale, w_branch, w_o, ln1_g, ln1_b, ln2_g, ln2_b,
           w_router, router_bias, w_e_gate, w_e_up, w_e_down):
    ag0 = PA_W + PG_W
    w_in_p = (w_in[:, :, :ag0].astype(BF16), w_in[:, :, ag0 + GLA_RANK:].astype(BF16),
              jnp.pad(w_in[:, :, ag0:ag0 + GLA_RANK], ((0, 0), (0, 0), (0, LANES - GLA_RANK))).astype(BF16))
    w_gate_b, w_branch_b, w_o_b = w_gate.astype(BF16), w_branch.astype(BF16), w_o.astype(BF16)
    wa2_p = jnp.pad(gla_w_a2, ((0, 0), (0, LANES - GLA_RANK), (0, 0))).astype(BF16)
    pool_w_b = pool_w.astype(BF16)
    wr_t = w_router.T.astype(BF16)
    rb = router_bias.reshape(N_EXPERTS, 1).astype(F32)
    cache_k = cache_attn_k.reshape(DEPTH, DEC_BATCH, WINDOW, KV_W)
    cache_v = cache_attn_v.reshape(DEPTH, DEC_BATCH, WINDOW, KV_W)
    pair_lo, pair_hi = jnp.array(_PAIR_LO, I32), jnp.array(_PAIR_HI, I32)

    x = (x_prompt.reshape(NP, D_MODEL), x_sample.reshape(NS, D_MODEL))
    xs = jnp.zeros((N_SORTED, XE_W), F32)
    ys = pos = None
    pk, pv, ps, pp, sk, sv, ss, sp = ([] for _ in range(8))
    for l in range(DEPTH):
        if l == 0:
            pa, pg, pu, pag = _in_proj(x, w_in_p, l)
        else:
            pa, pg, pu, pag, x = _in_proj(ys, w_in_p, l, pos)
        row2 = lambda a: a[l].reshape(1, -1)

        oa_p, k_p, v_p = _attn_prompt(pa, attn_sink[l])
        oa_s, nk, nv = _attn_sample(pa, attn_sink[l], cache_k, cache_v, l)
        og_p, s_p = _gla_prompt(pg, pag, wa2_p[l], row2(gla_b_a), row2(gla_norm_g))
        og_s, s_s = _gla_sample(pg, pag, wa2_p[l], row2(gla_b_a), row2(gla_norm_g), state_gla, l)
        op_p, u_last = _pool_prompt(pu, pool_w_b[l], row2(pool_scale))
        u_s = pu[NP:].reshape(DEC_BATCH, DEC_SEQ, POOL_W)
        ext_s = jnp.concatenate(
            [jnp.zeros((DEC_BATCH, POOL_HIST - POOL_STATE, POOL_W), F32), state_pool[l], u_s], axis=1)
        op_s = _pool_sample(ext_s, pool_w_b[l], row2(pool_scale))

        xe, meta = _merge(x, (oa_p, oa_s, og_p, og_s, op_p, op_s), w_gate_b, w_branch_b, w_o_b,
                          ln1_g.reshape(DEPTH, 1, D_MODEL), ln1_b.reshape(DEPTH, 1, D_MODEL), wr_t, rb, l)
        cls = meta[:, 0, :].reshape(N_TOK // LANES, LANES).astype(I32)
        cls2d = jnp.pad(cls, ((0, PLAN_ROWS - N_TOK // LANES), (0, 0)), constant_values=-1)
        pos2d, tcls, nused = _plan(cls2d)
        pos = pos2d.reshape(-1)[:N_TOK].reshape(N_TILES, 1, TM)
        n_used = nused[0, :1]
        tile_cls = tcls[0, :NT_E]
        tile_cls = jnp.where(jnp.arange(NT_E) < n_used, tile_cls, tile_cls[n_used[0] - 1])
        tile_a = (tile_cls // len(PAIRS)) * EXPERTS_PER_GROUP + pair_lo[tile_cls % len(PAIRS)]
        tile_b = (tile_cls // len(PAIRS)) * EXPERTS_PER_GROUP + pair_hi[tile_cls % len(PAIRS)]
        first = jnp.ones((1,), I32)
        new_a = jnp.concatenate([first, (tile_a[1:] != tile_a[:-1]).astype(I32)])
        new_b = jnp.concatenate([first, (tile_b[1:] != tile_b[:-1]).astype(I32)])
        xs = _dispatch(pos, xe, xs)
        ys = _experts(tile_a, tile_b, new_a, new_b, n_used, xs, w_e_gate, w_e_up, w_e_down,
                      ln2_g.reshape(DEPTH, 1, D_MODEL), ln2_b.reshape(DEPTH, 1, D_MODEL), l)

        pk.append(k_p.reshape(BATCH, WINDOW, N_KV_HEADS, HEAD_DIM))
        pv.append(v_p.reshape(BATCH, WINDOW, N_KV_HEADS, HEAD_DIM))
        ps.append(s_p)
        pp.append(u_last[:, POOL_HIST - POOL_STATE:, :])
        sk.append(nk.reshape(DEC_BATCH, WINDOW, N_KV_HEADS, HEAD_DIM))
        sv.append(nv.reshape(DEC_BATCH, WINDOW, N_KV_HEADS, HEAD_DIM))
        ss.append(s_s)
        sp.append(ext_s[:, POOL_HIST + DEC_SEQ - POOL_STATE:, :])
    y_prompt, y_sample = _combine(pos, ys)
    return (y_prompt.reshape(BATCH, SEQ, D_MODEL), y_sample.reshape(DEC_BATCH, DEC_SEQ, D_MODEL),
            jnp.stack(pk), jnp.stack(pv), jnp.stack(ps), jnp.stack(pp),
            jnp.stack(sk), jnp.stack(sv), jnp.stack(ss), jnp.stack(sp))
```

```python
import jax
import jax.numpy as jnp
import numpy as np
from jax import lax
from jax.experimental import pallas as pl
from jax.experimental.pallas import tpu as pltpu

F32, BF16, I32 = jnp.float32, jnp.bfloat16, jnp.int32

D_MODEL = 1024
BATCH = 8
SEQ = 2048
DEPTH = 4
DEC_BATCH = 128
DEC_SEQ = 8
N_HEADS = 8
N_KV_HEADS = 2
HEAD_DIM = 64
WINDOW = 128
ATTN_W = N_HEADS * HEAD_DIM
KV_W = N_KV_HEADS * HEAD_DIM
Q_PER_KV = N_HEADS // N_KV_HEADS
GLA_HEADS = 4
GLA_DK = 64
GLA_DV = 128
GLA_KW = GLA_HEADS * GLA_DK
GLA_VW = GLA_HEADS * GLA_DV
GLA_RANK = 16
GLA_TAU = 16.0
GLA_CHUNK = 64
POOL_WINDOWS = (2, 4, 8, 16)
POOL_GROUPS = 4
POOL_GW = 128
POOL_W = POOL_GROUPS * POOL_GW
POOL_STATE = max(POOL_WINDOWS) - 1
N_BRANCH = 3
BRANCH_W = 512
N_EXPERTS = 16
N_GROUPS = 4
EXPERTS_PER_GROUP = N_EXPERTS // N_GROUPS
D_EXPERT = 512
DN_ALPHA = (2.0 * DEPTH) ** 0.25
LN_EPS = 1e-5
RMS_EPS = 1e-6
NEG_INF = -1e30
ALIBI_SLOPES = tuple(2.0 ** (-8.0 * h / N_HEADS) for h in range(1, N_HEADS + 1))

LANES = 128
SUBLANES = 8

NP = BATCH * SEQ
NS = DEC_BATCH * DEC_SEQ
N_TOK = NP + NS
TM = 512
N_TILES = N_TOK // TM
PT = NP // TM

PA_W = ATTN_W + 2 * KV_W
PG_W = 2 * GLA_KW + 2 * GLA_VW

PAIRS = ((0, 1), (0, 2), (0, 3), (1, 2), (1, 3), (2, 3))
N_CLASSES = N_GROUPS * len(PAIRS)
TM_E = 512
NT_E = -(-(N_TOK + N_CLASSES * (TM_E - 1)) // TM_E)
N_SORTED = NT_E * TM_E
XE_W = D_MODEL + LANES
PLAN_ROWS = 256

VMEM_LIMIT = 56 * 1024 * 1024
DMA_THREADS = 2


def _params(*sem):
    return pltpu.CompilerParams(dimension_semantics=sem, vmem_limit_bytes=VMEM_LIMIT)


def _dot(a, b):
    return jnp.dot(a, b, preferred_element_type=F32)


def _dot_nt(a, b):
    return lax.dot_general(a, b, (((1,), (1,)), ((), ())), preferred_element_type=F32)


def _dot_tn(a, b):
    return lax.dot_general(a, b, (((0,), (0,)), ((), ())), preferred_element_type=F32)


def _bdot_nt(a, b):
    return lax.dot_general(a, b, (((2,), (2,)), ((0,), (0,))), preferred_element_type=F32)


def _bdot(a, b):
    return lax.dot_general(a, b, (((2,), (1,)), ((0,), (0,))), preferred_element_type=F32)


def _layer_norm(h, g, b):
    mu = jnp.mean(h, axis=-1, keepdims=True)
    hc = h - mu
    var = jnp.mean(hc * hc, axis=-1, keepdims=True)
    return hc * lax.rsqrt(var + LN_EPS) * g + b


def _sigmoid(x):
    return 0.5 * jnp.tanh(0.5 * x) + 0.5


def _log_sigmoid(x):
    return jnp.minimum(x, 0.0) - jnp.log1p(jnp.exp(-jnp.abs(x)))


def _pair_specs(width):
    return (pl.BlockSpec((TM, width), lambda i, *_: (jnp.minimum(i, PT - 1), 0)),
            pl.BlockSpec((TM, width), lambda i, *_: (jnp.maximum(i - PT, 0), 0)))


def _pair_tile(p_ref, s_ref):
    return jnp.where(pl.program_id(0) < PT, p_ref[...], s_ref[...])


_IN_PROJ_WIDTHS = (PA_W, PG_W, POOL_W, LANES)


def _in_proj_tile(x, w_refs, pool_refs, out_refs, hist_ref):
    w_main, w_pool, w_rank = w_refs
    pw_ref, ps_ref = pool_refs
    pa_ref, pg_ref, pu_ref, pag_ref, op_ref, last_ref = out_refs
    xb = x.astype(BF16)
    pa_ref[...] = _dot(xb, w_main[:, 0:PA_W])
    pg_ref[...] = _dot(xb, w_main[:, PA_W:PA_W + PG_W])
    u = _dot(xb, w_pool[...])
    pu_ref[...] = u
    pag_ref[...] = _dot(xb, w_rank[...])

    i = pl.program_id(0)
    tiles_per_seq = SEQ // TM

    @pl.when(i < PT)
    def _():
        j = i % tiles_per_seq

        @pl.when(j == 0)
        def _():
            hist_ref[...] = jnp.zeros_like(hist_ref)

        ext = jnp.concatenate([hist_ref[...], u], axis=0)
        pos = j * TM + lax.broadcasted_iota(I32, (TM, 1), 0)
        cnt = [jnp.minimum(pos + 1, w).astype(F32) for w in POOL_WINDOWS]
        for g, y in enumerate(_pool_groups(ext, cnt, pw_ref, ps_ref, TM)):
            op_ref[:, g * POOL_GW:(g + 1) * POOL_GW] = y
        hist_ref[...] = u[TM - POOL_HIST:, :]

        @pl.when(j == tiles_per_seq - 1)
        def _():
            last_ref[...] = u[TM - POOL_HIST:, :]


def _in_proj_kernel(xp_ref, xs_ref, wm_ref, wp_ref, wr_ref, pw_ref, ps_ref,
                    pa_ref, pg_ref, pu_ref, pag_ref, op_ref, last_ref, hist_ref):
    _in_proj_tile(_pair_tile(xp_ref, xs_ref), (wm_ref, wp_ref, wr_ref), (pw_ref, ps_ref),
                  (pa_ref, pg_ref, pu_ref, pag_ref, op_ref, last_ref), hist_ref)


def _in_proj_gather_kernel(pos_ref, pos_next_ref, ys_hbm, wm_ref, wp_ref, wr_ref, pw_ref, ps_ref,
                           pa_ref, pg_ref, pu_ref, pag_ref, op_ref, last_ref, x_ref, buf, sem, hist_ref):
    i = pl.program_id(0)
    slot = i % 2

    def copy(p_ref, r, s):
        return pltpu.make_async_copy(ys_hbm.at[pl.ds(p_ref[0, r], 1)], buf.at[s, pl.ds(r, 1)], sem.at[s])

    def wait_tile(p_ref, s):
        def wait(r, carry):
            copy(p_ref, r, s).wait()
            return carry
        lax.fori_loop(0, TM, wait, 0, unroll=8)

    @pl.when(i == 0)
    def _():
        for r in range(TM):
            copy(pos_ref, r, 0).start(priority=r % DMA_THREADS)

    wait_tile(pos_ref, slot)
    x = buf[slot]
    for r in range(TM):
        copy(pos_next_ref, r, 1 - slot).start(priority=r % DMA_THREADS)
    x_ref[...] = x
    _in_proj_tile(x, (wm_ref, wp_ref, wr_ref), (pw_ref, ps_ref),
                  (pa_ref, pg_ref, pu_ref, pag_ref, op_ref, last_ref), hist_ref)

    @pl.when(i == pl.num_programs(0) - 1)
    def _():
        wait_tile(pos_next_ref, 1 - slot)


def _in_proj(x, w_in_parts, pool_w, pool_scale, layer, pos=None):
    row = lambda w: pl.BlockSpec((TM, w), lambda i: (i, 0))
    w_specs = [pl.BlockSpec((None,) + w.shape[1:], lambda i: (layer, 0, 0)) for w in w_in_parts]
    w_specs += [pl.BlockSpec((None, POOL_GROUPS, POOL_GW, POOL_GW), lambda i: (layer, 0, 0, 0)),
                pl.BlockSpec((None, 1, POOL_W), lambda i: (layer, 0, 0))]
    weights = (*w_in_parts, pool_w, pool_scale)
    out_specs = [row(w) for w in _IN_PROJ_WIDTHS]
    out_specs += [pl.BlockSpec((TM, POOL_W), lambda i: (jnp.minimum(i, PT - 1), 0)),
                  pl.BlockSpec((None, POOL_HIST, POOL_W), lambda i: (jnp.minimum(i // (SEQ // TM), BATCH - 1), 0, 0))]
    outs = [jax.ShapeDtypeStruct((N_TOK, w), F32) for w in _IN_PROJ_WIDTHS]
    outs += [jax.ShapeDtypeStruct((NP, POOL_W), BF16), jax.ShapeDtypeStruct((BATCH, POOL_HIST, POOL_W), F32)]
    hist = pltpu.VMEM((POOL_HIST, POOL_W), F32)
    if pos is None:
        return pl.pallas_call(
            _in_proj_kernel,
            grid=(N_TILES,),
            in_specs=[*_pair_specs(D_MODEL), *w_specs],
            out_specs=out_specs,
            out_shape=outs,
            scratch_shapes=[hist],
            compiler_params=_params("arbitrary"),
            name="in_proj",
        )(*x, *weights)
    pos_next = pl.BlockSpec((None, 1, TM), lambda i: (jnp.minimum(i + 1, N_TILES - 1), 0, 0), memory_space=pltpu.SMEM)
    return pl.pallas_call(
        _in_proj_gather_kernel,
        grid=(N_TILES,),
        in_specs=[_POS_SPEC, pos_next, _ANY, *w_specs],
        out_specs=out_specs + [row(D_MODEL)],
        out_shape=outs + [jax.ShapeDtypeStruct((N_TOK, D_MODEL), F32)],
        scratch_shapes=[pltpu.VMEM((2, TM, D_MODEL), F32), pltpu.SemaphoreType.DMA((2,)), hist],
        compiler_params=_params("arbitrary"),
        name="in_proj_gather",
    )(pos, pos, x, *weights)


def _softmax_sink_pv(parts, sink, pv):
    m = sink
    for s, _ in parts:
        m = jnp.maximum(m, jnp.max(s, axis=-1, keepdims=True))
    den = jnp.exp(sink - m)
    es = []
    for s, _ in parts:
        e = jnp.exp(s - m)
        den = den + jnp.sum(e, axis=-1, keepdims=True)
        es.append(e)
    inv = 1.0 / den
    out = None
    for e, (_, v) in zip(es, parts):
        o = pv((e * inv).astype(BF16), v)
        out = o if out is None else out + o
    return out


def _head_column(sink_ref, kv, rows_per_head):
    g = lax.broadcasted_iota(I32, (Q_PER_KV * rows_per_head, 1), 0) // rows_per_head
    col = jnp.zeros(g.shape, F32)
    for i in range(Q_PER_KV):
        col = jnp.where(g == i, sink_ref[kv * Q_PER_KV + i], col)
    return col


def _alibi_bias(dist, visible):
    out = np.empty((N_KV_HEADS, Q_PER_KV * dist.shape[0], dist.shape[1]), np.float32)
    for h in range(N_HEADS):
        kv, g = divmod(h, Q_PER_KV)
        out[kv, g * dist.shape[0]:(g + 1) * dist.shape[0]] = np.where(visible, -ALIBI_SLOPES[h] * dist, NEG_INF)
    return out


ATTN_QB = 4


def _attn_prompt_kernel(sink_ref, cur_ref, prev_ref, bias0_ref, bias_ref, o_ref, nk_ref, nv_ref):
    k = jnp.concatenate([prev_ref[:, 0:KV_W], cur_ref[:, ATTN_W:ATTN_W + KV_W]], axis=0).astype(BF16)
    vt = jnp.concatenate([prev_ref[:, KV_W:2 * KV_W], cur_ref[:, ATTN_W + KV_W:PA_W]], axis=0).T.astype(BF16)
    g_of_col = lax.broadcasted_iota(I32, (1, Q_PER_KV * WINDOW), 1) // WINDOW
    for kv in range(N_KV_HEADS):
        ks = slice(kv * HEAD_DIM, (kv + 1) * HEAD_DIM)
        heads = range(kv * Q_PER_KV, (kv + 1) * Q_PER_KV)
        sink = jnp.zeros(g_of_col.shape, F32)
        for g, h in enumerate(heads):
            sink = jnp.where(g_of_col == g, sink_ref[h], sink)
        for j in range(ATTN_QB):
            rows = slice(j * WINDOW, (j + 1) * WINDOW)
            keys = slice(j * WINDOW, (j + 2) * WINDOW)
            q = jnp.concatenate([cur_ref[rows, h * HEAD_DIM:(h + 1) * HEAD_DIM] for h in heads], axis=0)
            bias = bias0_ref[kv] if j == 0 else bias_ref[kv]
            st = _dot_nt(k[keys, ks], (q * (HEAD_DIM ** -0.5)).astype(BF16)) + bias
            m = jnp.maximum(sink, jnp.max(st, axis=0, keepdims=True))
            e = jnp.exp(st - m)
            inv = 1.0 / (jnp.exp(sink - m) + jnp.sum(e, axis=0, keepdims=True))
            ot = _dot(vt[ks, keys], (e * inv).astype(BF16))
            for pair in range(Q_PER_KV // 2):
                two = jnp.concatenate(
                    [ot[:, (2 * pair + g) * WINDOW:(2 * pair + g + 1) * WINDOW] for g in range(2)], axis=0)
                lo = (kv * Q_PER_KV + 2 * pair) * HEAD_DIM
                o_ref[rows, lo:lo + 2 * HEAD_DIM] = two.T.astype(BF16)

    @pl.when(pl.program_id(1) == pl.num_programs(1) - 1)
    def _():
        last = slice((ATTN_QB - 1) * WINDOW, ATTN_QB * WINDOW)
        nk_ref[...] = cur_ref[last, ATTN_W:ATTN_W + KV_W]
        nv_ref[...] = cur_ref[last, ATTN_W + KV_W:PA_W]


def _attn_prompt(pa, sink):
    nb = SEQ // WINDOW
    ns = nb // ATTN_QB
    r = np.arange(WINDOW)[:, None]
    c = np.arange(2 * WINDOW)[None, :]
    band = (c > r) & (c <= WINDOW + r)
    bias = np.stack([_alibi_bias(WINDOW + r - c, band & (c >= WINDOW)), _alibi_bias(WINDOW + r - c, band)])
    bias = jnp.asarray(bias.transpose(0, 1, 3, 2))
    bias_spec = lambda variant: pl.BlockSpec((None, N_KV_HEADS, 2 * WINDOW, Q_PER_KV * WINDOW),
                                             lambda b, i, s: (variant(i), 0, 0, 0))
    state = pl.BlockSpec((None, WINDOW, KV_W), lambda b, i, s: (b, 0, 0))
    return pl.pallas_call(
        _attn_prompt_kernel,
        grid_spec=pltpu.PrefetchScalarGridSpec(
            num_scalar_prefetch=1,
            grid=(BATCH, ns),
            in_specs=[
                pl.BlockSpec((ATTN_QB * WINDOW, PA_W), lambda b, i, s: (b * ns + i, 0)),
                pl.BlockSpec((WINDOW, 2 * KV_W),
                             lambda b, i, s: (b * nb + jnp.maximum(ATTN_QB * i - 1, 0), ATTN_W // (2 * KV_W))),
                bias_spec(lambda i: jnp.minimum(i, 1)), bias_spec(lambda i: 1),
            ],
            out_specs=[pl.BlockSpec((ATTN_QB * WINDOW, ATTN_W), lambda b, i, s: (b * ns + i, 0)), state, state],
        ),
        out_shape=[jax.ShapeDtypeStruct((NP, ATTN_W), BF16),
                   jax.ShapeDtypeStruct((BATCH, WINDOW, KV_W), F32),
                   jax.ShapeDtypeStruct((BATCH, WINDOW, KV_W), F32)],
        compiler_params=_params("arbitrary", "arbitrary"),
        name="attn_prompt",
    )(sink, pa, pa, bias, bias)


ATTN_SB = 16


def _attn_sample_kernel(sink_ref, cur_ref, kc_ref, vc_ref, bias_c_ref, bias_n_ref, o_ref, nk_ref, nv_ref):
    cur = cur_ref[...].reshape(ATTN_SB, DEC_SEQ, PA_W)
    for kv in range(N_KV_HEADS):
        ks = slice(kv * HEAD_DIM, (kv + 1) * HEAD_DIM)
        heads = range(kv * Q_PER_KV, (kv + 1) * Q_PER_KV)
        q = jnp.concatenate([cur[:, :, h * HEAD_DIM:(h + 1) * HEAD_DIM] for h in heads], axis=1)
        q = (q * (HEAD_DIM ** -0.5)).astype(BF16)
        kn = cur[:, :, ATTN_W + kv * HEAD_DIM:ATTN_W + (kv + 1) * HEAD_DIM].astype(BF16)
        vn = cur[:, :, ATTN_W + KV_W + kv * HEAD_DIM:ATTN_W + KV_W + (kv + 1) * HEAD_DIM].astype(BF16)
        sc = _bdot_nt(q, kc_ref[:, :, ks].astype(BF16)) + bias_c_ref[kv]
        sn = _bdot_nt(q, kn) + bias_n_ref[kv]
        sink = _head_column(sink_ref, kv, DEC_SEQ)
        o = _softmax_sink_pv([(sc, vc_ref[:, :, ks].astype(BF16)), (sn, vn)], sink, _bdot)
        for g, h in enumerate(heads):
            o_ref[:, h * HEAD_DIM:(h + 1) * HEAD_DIM] = (
                o[:, g * DEC_SEQ:(g + 1) * DEC_SEQ, :].reshape(ATTN_SB * DEC_SEQ, HEAD_DIM).astype(BF16))
    keep = WINDOW - DEC_SEQ
    nk_ref[:, 0:keep, :] = kc_ref[:, DEC_SEQ:WINDOW, :]
    nk_ref[:, keep:WINDOW, :] = cur[:, :, ATTN_W:ATTN_W + KV_W]
    nv_ref[:, 0:keep, :] = vc_ref[:, DEC_SEQ:WINDOW, :]
    nv_ref[:, keep:WINDOW, :] = cur[:, :, ATTN_W + KV_W:PA_W]


def _attn_sample(pa, sink, cache_k, cache_v, layer):
    rows = ATTN_SB * DEC_SEQ
    first = NP // rows
    t = np.arange(DEC_SEQ)[:, None]
    jc = np.arange(WINDOW)[None, :]
    jn = np.arange(DEC_SEQ)[None, :]
    bias_c = jnp.asarray(_alibi_bias(WINDOW + t - jc, jc > t))
    bias_n = jnp.asarray(_alibi_bias(t - jn, jn <= t))
    cache = pl.BlockSpec((None, ATTN_SB, WINDOW, KV_W), lambda j, s: (layer, j, 0, 0))
    new = pl.BlockSpec((ATTN_SB, WINDOW, KV_W), lambda j, s: (j, 0, 0))
    const = lambda a: pl.BlockSpec(a.shape, lambda j, s: (0,) * a.ndim)
    return pl.pallas_call(
        _attn_sample_kernel,
        grid_spec=pltpu.PrefetchScalarGridSpec(
            num_scalar_prefetch=1,
            grid=(DEC_BATCH // ATTN_SB,),
            in_specs=[pl.BlockSpec((rows, PA_W), lambda j, s: (first + j, 0)), cache, cache,
                      const(bias_c), const(bias_n)],
            out_specs=[pl.BlockSpec((rows, ATTN_W), lambda j, s: (j, 0)), new, new],
        ),
        out_shape=[jax.ShapeDtypeStruct((NS, ATTN_W), BF16),
                   jax.ShapeDtypeStruct((DEC_BATCH, WINDOW, KV_W), F32),
                   jax.ShapeDtypeStruct((DEC_BATCH, WINDOW, KV_W), F32)],
        compiler_params=_params("arbitrary"),
        name="attn_sample",
    )(sink, pa, cache_k, cache_v, bias_c, bias_n)


def _chunk_cumsum(x, chunk):
    pos = lax.broadcasted_iota(I32, x.shape, 0) % chunk
    sh = 1
    while sh < chunk:
        x = x + jnp.where(pos >= sh, pltpu.roll(x, sh, 0), 0.0)
        sh *= 2
    return x


def _gla_log_decay(pag_ref, wa2_ref, ba_ref):
    z = _dot(pag_ref[...].astype(BF16), wa2_ref[...]) + ba_ref[...]
    return _log_sigmoid(z) / GLA_TAU


def _gla_finish(o, rg, g):
    o = o * lax.rsqrt(jnp.mean(o * o, axis=-1, keepdims=True) + RMS_EPS)
    return (o * g * (rg * _sigmoid(rg))).astype(BF16)


GLA_TT = 512
GLA_AG = 128


def _gla_prompt_kernel(pg_ref, pag_ref, wa2_ref, ba_ref, g_ref, o_ref, s_ref,
                       qd_ref, kd_ref, kdec_ref, a_ref, sall_ref, st_ref, acc_ref):
    nc = GLA_TT // GLA_CHUNK

    @pl.when(pl.program_id(1) == 0)
    def _():
        st_ref[...] = jnp.zeros_like(st_ref)

    cum = _chunk_cumsum(_gla_log_decay(pag_ref, wa2_ref, ba_ref), GLA_CHUNK)
    cum3 = cum.reshape(nc, GLA_CHUNK, GLA_KW)
    tot3 = cum3[:, GLA_CHUNK - 1:GLA_CHUNK, :]
    k = pg_ref[:, GLA_KW:2 * GLA_KW]
    qd_ref[...] = (pg_ref[:, 0:GLA_KW] * (GLA_DK ** -0.5) * jnp.exp(cum)).astype(BF16)
    kd_ref[...] = (k * jnp.exp(-cum)).astype(BF16)
    kdec_ref[...] = (k * jnp.exp(tot3 - cum3).reshape(GLA_TT, GLA_KW)).astype(BF16)
    etot = jnp.exp(tot3)

    def hs(h):
        return slice(h * GLA_DK, (h + 1) * GLA_DK)

    def vs(h):
        return slice(2 * GLA_KW + h * GLA_DV, 2 * GLA_KW + (h + 1) * GLA_DV)

    for c in range(nc):
        rows = slice(c * GLA_CHUNK, (c + 1) * GLA_CHUNK)
        for h in range(GLA_HEADS):
            a_ref[c, :, hs(h)] = _dot_tn(pg_ref[rows, vs(h)].astype(BF16), kdec_ref[rows, hs(h)])

    st = st_ref[...]
    for c in range(nc):
        sall_ref[c] = st.astype(BF16)
        st = etot[c] * st + a_ref[c]
    st_ref[...] = st

    r = lax.broadcasted_iota(I32, (GLA_AG, GLA_AG), 0)
    col = lax.broadcasted_iota(I32, (GLA_AG, GLA_AG), 1)
    causal = (r // GLA_CHUNK == col // GLA_CHUNK) & (col <= r)
    for h in range(GLA_HEADS):
        out = slice(h * GLA_DV, (h + 1) * GLA_DV)
        for c in range(nc):
            rows = slice(c * GLA_CHUNK, (c + 1) * GLA_CHUNK)
            acc_ref[rows, out] = _dot_nt(qd_ref[rows, hs(h)], sall_ref[c, :, hs(h)])
        for a in range(GLA_TT // GLA_AG):
            rows = slice(a * GLA_AG, (a + 1) * GLA_AG)
            att = jnp.where(causal, _dot_nt(qd_ref[rows, hs(h)], kd_ref[rows, hs(h)]), 0.0)
            acc_ref[rows, out] += _dot(att.astype(BF16), pg_ref[rows, vs(h)].astype(BF16))
        rg = pg_ref[:, 2 * GLA_KW + GLA_VW + h * GLA_DV:2 * GLA_KW + GLA_VW + (h + 1) * GLA_DV]
        o_ref[:, out] = _gla_finish(acc_ref[:, out], rg, g_ref[:, out])

    @pl.when(pl.program_id(1) == pl.num_programs(1) - 1)
    def _():
        for h in range(GLA_HEADS):
            s_ref[h] = st_ref[:, hs(h)].T


def _gla_prompt(pg, pag, wa2, ba, g):
    nt = SEQ // GLA_TT
    nc = GLA_TT // GLA_CHUNK
    const = lambda shape: pl.BlockSpec(shape, lambda b, j: (0,) * len(shape))
    return pl.pallas_call(
        _gla_prompt_kernel,
        grid=(BATCH, nt),
        in_specs=[
            pl.BlockSpec((GLA_TT, PG_W), lambda b, j: (b * nt + j, 0)),
            pl.BlockSpec((GLA_TT, LANES), lambda b, j: (b * nt + j, 0)),
            const((LANES, GLA_KW)), const((1, GLA_KW)), const((1, GLA_VW)),
        ],
        out_specs=[
            pl.BlockSpec((GLA_TT, GLA_VW), lambda b, j: (b * nt + j, 0)),
            pl.BlockSpec((None, GLA_HEADS, GLA_DK, GLA_DV), lambda b, j: (b, 0, 0, 0)),
        ],
        out_shape=[jax.ShapeDtypeStruct((NP, GLA_VW), BF16),
                   jax.ShapeDtypeStruct((BATCH, GLA_HEADS, GLA_DK, GLA_DV), F32)],
        scratch_shapes=[pltpu.VMEM((GLA_TT, GLA_KW), BF16), pltpu.VMEM((GLA_TT, GLA_KW), BF16),
                        pltpu.VMEM((GLA_TT, GLA_KW), BF16),
                        pltpu.VMEM((nc, GLA_DV, GLA_KW), F32), pltpu.VMEM((nc, GLA_DV, GLA_KW), BF16),
                        pltpu.VMEM((GLA_DV, GLA_KW), F32), pltpu.VMEM((GLA_TT, GLA_VW), F32)],
        compiler_params=_params("arbitrary", "arbitrary"),
        name="gla_prompt",
    )(pg, pag, wa2, ba, g)


GLA_SB = 8


def _gla_sample_kernel(pg_ref, pag_ref, wa2_ref, ba_ref, g_ref, s0_ref, o_ref, s_ref):
    cum_all = _chunk_cumsum(_gla_log_decay(pag_ref, wa2_ref, ba_ref), DEC_SEQ)
    tri = (lax.broadcasted_iota(I32, (DEC_SEQ, DEC_SEQ), 0) >= lax.broadcasted_iota(I32, (DEC_SEQ, DEC_SEQ), 1))
    for s in range(GLA_SB):
        rows = slice(s * DEC_SEQ, (s + 1) * DEC_SEQ)
        cum = cum_all[rows, :]
        tot = cum[DEC_SEQ - 1:DEC_SEQ, :]
        q = pg_ref[rows, 0:GLA_KW] * (GLA_DK ** -0.5)
        k = pg_ref[rows, GLA_KW:2 * GLA_KW]
        qd = (q * jnp.exp(cum)).astype(BF16)
        kd = (k * jnp.exp(-cum)).astype(BF16)
        kdec = (k * jnp.exp(tot - cum)).astype(BF16)
        etot = jnp.exp(tot)
        etot_col = [jnp.broadcast_to(etot[:, p * LANES:(p + 1) * LANES], (SUBLANES, LANES)).T[:, 0:1]
                    for p in range(GLA_KW // LANES)]
        for h in range(GLA_HEADS):
            ks = slice(h * GLA_DK, (h + 1) * GLA_DK)
            vs = slice(2 * GLA_KW + h * GLA_DV, 2 * GLA_KW + (h + 1) * GLA_DV)
            rs = slice(2 * GLA_KW + GLA_VW + h * GLA_DV, 2 * GLA_KW + GLA_VW + (h + 1) * GLA_DV)
            v = pg_ref[rows, vs].astype(BF16)
            st = s0_ref[s, h]
            att = jnp.where(tri, _dot_nt(qd[:, ks], kd[:, ks]), 0.0)
            o = _dot(qd[:, ks], st.astype(BF16)) + _dot(att.astype(BF16), v)
            per = LANES // GLA_DK
            col = etot_col[h // per][(h % per) * GLA_DK:(h % per + 1) * GLA_DK, :]
            s_ref[s, h] = col * st + _dot_tn(kdec[:, ks], v)
            o_ref[rows, h * GLA_DV:(h + 1) * GLA_DV] = _gla_finish(o, pg_ref[rows, rs], g_ref[:, h * GLA_DV:(h + 1) * GLA_DV])


def _gla_sample(pg, pag, wa2, ba, g, state, layer):
    rows = GLA_SB * DEC_SEQ
    first = NP // rows
    const = lambda shape: pl.BlockSpec(shape, lambda j: (0,) * len(shape))
    return pl.pallas_call(
        _gla_sample_kernel,
        grid=(DEC_BATCH // GLA_SB,),
        in_specs=[
            pl.BlockSpec((rows, PG_W), lambda j: (first + j, 0)),
            pl.BlockSpec((rows, LANES), lambda j: (first + j, 0)),
            const((LANES, GLA_KW)), const((1, GLA_KW)), const((1, GLA_VW)),
            pl.BlockSpec((None, GLA_SB, GLA_HEADS, GLA_DK, GLA_DV), lambda j: (layer, j, 0, 0, 0)),
        ],
        out_specs=[
            pl.BlockSpec((rows, GLA_VW), lambda j: (j, 0)),
            pl.BlockSpec((GLA_SB, GLA_HEADS, GLA_DK, GLA_DV), lambda j: (j, 0, 0, 0)),
        ],
        out_shape=[jax.ShapeDtypeStruct((NS, GLA_VW), BF16),
                   jax.ShapeDtypeStruct((DEC_BATCH, GLA_HEADS, GLA_DK, GLA_DV), F32)],
        compiler_params=_params("arbitrary"),
        name="gla_sample",
    )(pg, pag, wa2, ba, g, state)


POOL_HIST = 16


def _pool_groups(ext, cnt, pw_ref, ps_ref, out_rows):
    ax = ext.ndim - 2
    outs = []
    for g, w in enumerate(POOL_WINDOWS):
        x = ext[..., g * POOL_GW:(g + 1) * POOL_GW]
        s, sh = x, 1
        while sh < w:
            s = s + pltpu.roll(s, sh, ax)
            sh *= 2
        if ext.ndim == 3:
            d = (s[:, POOL_HIST:, :] / cnt[g] - x[:, POOL_HIST:, :]).reshape(out_rows, POOL_GW)
        else:
            d = s[POOL_HIST:, :] / cnt[g] - x[POOL_HIST:, :]
        y = _dot(d.astype(BF16), pw_ref[g]) * ps_ref[:, g * POOL_GW:(g + 1) * POOL_GW]
        outs.append(y.astype(BF16))
    return outs


POOL_SB = 16


def _pool_sample_kernel(ext_ref, pw_ref, ps_ref, o_ref):
    cnt = [float(w) for w in POOL_WINDOWS]
    for g, y in enumerate(_pool_groups(ext_ref[...], cnt, pw_ref, ps_ref, POOL_SB * DEC_SEQ)):
        o_ref[:, g * POOL_GW:(g + 1) * POOL_GW] = y


def _pool_sample(ext, pw, ps):
    rows = POOL_SB * DEC_SEQ
    return pl.pallas_call(
        _pool_sample_kernel,
        grid=(DEC_BATCH // POOL_SB,),
        in_specs=[
            pl.BlockSpec((POOL_SB, POOL_HIST + DEC_SEQ, POOL_W), lambda j: (j, 0, 0)),
            pl.BlockSpec((POOL_GROUPS, POOL_GW, POOL_GW), lambda j: (0, 0, 0)),
            pl.BlockSpec((1, POOL_W), lambda j: (0, 0)),
        ],
        out_specs=pl.BlockSpec((rows, POOL_W), lambda j: (j, 0)),
        out_shape=jax.ShapeDtypeStruct((NS, POOL_W), BF16),
        compiler_params=_params("arbitrary"),
        name="pool_sample",
    )(ext, pw, ps)


def _route(sc, sel):
    gscore = []
    for g in range(N_GROUPS):
        v = sel[EXPERTS_PER_GROUP * g:EXPERTS_PER_GROUP * (g + 1)]
        best = None
        for a, b in PAIRS:
            pair = v[a] + v[b]
            best = pair if best is None else jnp.maximum(best, pair)
        gscore.append(best)
    gi = jnp.zeros_like(gscore[0], dtype=I32)
    best = gscore[0]
    for g in range(1, N_GROUPS):
        upd = gscore[g] > best
        gi = jnp.where(upd, g, gi)
        best = jnp.where(upd, gscore[g], best)

    def in_group(rows, j):
        out = rows[(N_GROUPS - 1) * EXPERTS_PER_GROUP + j]
        for g in range(N_GROUPS - 2, -1, -1):
            out = jnp.where(gi == g, rows[g * EXPERTS_PER_GROUP + j], out)
        return out

    u = [in_group(sel, j) for j in range(EXPERTS_PER_GROUP)]
    s_in = [in_group(sc, j) for j in range(EXPERTS_PER_GROUP)]

    def argmax4(vals):
        idx = jnp.zeros_like(gi)
        m = vals[0]
        for j in range(1, EXPERTS_PER_GROUP):
            upd = vals[j] > m
            idx = jnp.where(upd, j, idx)
            m = jnp.where(upd, vals[j], m)
        return idx

    def pick(vals, idx):
        out = vals[EXPERTS_PER_GROUP - 1]
        for j in range(EXPERTS_PER_GROUP - 2, -1, -1):
            out = jnp.where(idx == j, vals[j], out)
        return out

    i1 = argmax4(u)
    i2 = argmax4([jnp.where(i1 == j, NEG_INF, u[j]) for j in range(EXPERTS_PER_GROUP)])
    w1, w2 = pick(s_in, i1), pick(s_in, i2)
    tot = w1 + w2
    w1, w2 = w1 / tot, w2 / tot
    lo, hi = jnp.minimum(i1, i2), jnp.maximum(i1, i2)
    first_lo = i1 < i2
    w_lo, w_hi = jnp.where(first_lo, w1, w2), jnp.where(first_lo, w2, w1)
    pair = jnp.where(lo == 0, hi - 1, jnp.where(lo == 1, hi + 1, 5))
    return gi * len(PAIRS) + pair, w_lo, w_hi


def _merge_kernel(*refs):
    (oap_ref, oas_ref, ogp_ref, ogs_ref, opp_ref, ops_ref, wg_ref, wb_ref, wo_ref,
     g1_ref, b1_ref, wr_ref, rb_ref, xe_ref, meta_ref) = refs[-15:]
    x = refs[0][...] if len(refs) == 16 else _pair_tile(refs[0], refs[1])
    xb = x.astype(BF16)
    merged = None
    for n, (brp, brs) in enumerate(((oap_ref, oas_ref), (ogp_ref, ogs_ref), (opp_ref, ops_ref))):
        gate = _sigmoid(_dot(xb, wg_ref[:, n * D_MODEL:(n + 1) * D_MODEL]))
        term = gate * _dot(_pair_tile(brp, brs), wb_ref[n])
        merged = term if merged is None else merged + term
    mix = _dot(merged.astype(BF16), wo_ref[...])
    x1 = _layer_norm(DN_ALPHA * x + mix, g1_ref[...], b1_ref[...])
    xe_ref[:, 0:D_MODEL] = x1
    sc_t = _sigmoid(_dot_nt(wr_ref[...], x1.astype(BF16)))
    sel_t = sc_t + rb_ref[...]
    sc = [sc_t[e:e + 1, :] for e in range(N_EXPERTS)]
    sel = [sel_t[e:e + 1, :] for e in range(N_EXPERTS)]
    cls, w_lo, w_hi = _route(sc, sel)
    cls = cls.astype(F32)

    def rows(n):
        rid = lax.broadcasted_iota(I32, (n, TM), 0)
        return jnp.where(rid == 0, cls, jnp.where(rid == 1, w_lo, jnp.where(rid == 2, w_hi, 0.0)))

    meta_ref[...] = rows(SUBLANES)
    xe_ref[:, D_MODEL:XE_W] = rows(LANES).T


def _merge(x, branches, wg, wb, wo, g1, b1, wr_t, rb, layer):
    row = lambda w: pl.BlockSpec((TM, w), lambda i: (i, 0))
    lay = lambda *shape: pl.BlockSpec((None,) + shape, lambda i: (layer,) + (0,) * len(shape))
    const = lambda *shape: pl.BlockSpec(shape, lambda i: (0,) * len(shape))
    return pl.pallas_call(
        _merge_kernel,
        grid=(N_TILES,),
        in_specs=[*(_pair_specs(D_MODEL) if isinstance(x, tuple) else (row(D_MODEL),)),
                  *(_pair_specs(BRANCH_W) * N_BRANCH),
                  lay(D_MODEL, N_BRANCH * D_MODEL), lay(N_BRANCH, BRANCH_W, D_MODEL), lay(D_MODEL, D_MODEL),
                  lay(1, D_MODEL), lay(1, D_MODEL), const(N_EXPERTS, D_MODEL), const(N_EXPERTS, 1)],
        out_specs=[row(XE_W), pl.BlockSpec((None, SUBLANES, TM), lambda i: (i, 0, 0))],
        out_shape=[jax.ShapeDtypeStruct((N_TOK, XE_W), F32), jax.ShapeDtypeStruct((N_TILES, SUBLANES, TM), F32)],
        compiler_params=_params("arbitrary"),
        name="merge",
    )(*(x if isinstance(x, tuple) else (x,)), *branches, wg, wb, wo, g1, b1, wr_t, rb)


def _plan_kernel(cls_ref, pos_ref, tcls_ref, nused_ref):
    cls = cls_ref[...]
    lane_r = lax.broadcasted_iota(I32, (LANES, 2 * LANES), 0)
    lane_c = lax.broadcasted_iota(I32, (LANES, 2 * LANES), 1)
    lane_mat = ((lane_c >= LANES) | (lane_r < lane_c)).astype(BF16)
    row_r = lax.broadcasted_iota(I32, (2 * PLAN_ROWS, PLAN_ROWS), 0)
    row_c = lax.broadcasted_iota(I32, (2 * PLAN_ROWS, PLAN_ROWS), 1)
    row_mat = ((row_r >= PLAN_ROWS) | (row_c < row_r)).astype(BF16)
    tile_start = (lax.broadcasted_iota(I32, (SUBLANES, LANES), 1) * TM_E).astype(F32)
    pos = jnp.zeros((PLAN_ROWS, LANES), F32)
    off = jnp.zeros((PLAN_ROWS, LANES), F32)
    tcls = jnp.zeros((SUBLANES, LANES), I32)
    for c in range(N_CLASSES):
        m = cls == c
        lanes = _dot(m.astype(BF16), lane_mat)
        rows = _dot(row_mat, lanes[:, LANES:].astype(BF16))
        rank = lanes[:, 0:LANES] + rows[0:PLAN_ROWS]
        count = rows[PLAN_ROWS:]
        pos = jnp.where(m, off + rank, pos)
        off = off + jnp.ceil(count * (1.0 / TM_E)) * TM_E
        tcls = tcls + (off[0:SUBLANES] <= tile_start).astype(I32)
    pos_ref[...] = pos.astype(I32)
    tcls_ref[...] = tcls
    nused_ref[...] = (off[0:SUBLANES] * (1.0 / TM_E)).astype(I32)


def _plan(cls2d):
    return pl.pallas_call(
        _plan_kernel,
        out_shape=[jax.ShapeDtypeStruct((PLAN_ROWS, LANES), I32),
                   jax.ShapeDtypeStruct((SUBLANES, LANES), I32),
                   jax.ShapeDtypeStruct((SUBLANES, LANES), I32)],
        compiler_params=pltpu.CompilerParams(vmem_limit_bytes=VMEM_LIMIT),
        name="plan",
    )(cls2d)


def _row_copies(pos_ref, tile_ref, sorted_hbm, sem, scatter):
    def copy(r):
        row, srt = tile_ref.at[pl.ds(r, 1)], sorted_hbm.at[pl.ds(pos_ref[0, r], 1)]
        return pltpu.make_async_copy(row, srt, sem) if scatter else pltpu.make_async_copy(srt, row, sem)

    def wait(r, carry):
        copy(r).wait()
        return carry

    rows = tile_ref.shape[0]
    for r in range(rows):
        copy(r).start(priority=r % DMA_THREADS)
    lax.fori_loop(0, rows, wait, 0, unroll=8)


def _dispatch_kernel(pos_ref, x_ref, xs_in_hbm, xs_hbm, sem):
    del xs_in_hbm
    _row_copies(pos_ref, x_ref, xs_hbm, sem, scatter=True)


def _combine_kernel(pos_ref, ys_hbm, xp_ref, xs_ref, sem):
    @pl.when(pl.program_id(0) < PT)
    def _():
        _row_copies(pos_ref, xp_ref, ys_hbm, sem, scatter=False)

    @pl.when(pl.program_id(0) >= PT)
    def _():
        _row_copies(pos_ref, xs_ref, ys_hbm, sem, scatter=False)


_POS_SPEC = pl.BlockSpec((None, 1, TM), lambda i: (i, 0, 0), memory_space=pltpu.SMEM)
_ANY = pl.BlockSpec(memory_space=pl.ANY)


TM_D = 1024


def _dispatch(pos, xe, xs_prev):
    pos = pos.reshape(N_TOK // TM_D, 1, TM_D)
    return pl.pallas_call(
        _dispatch_kernel,
        grid=(N_TOK // TM_D,),
        in_specs=[pl.BlockSpec((None, 1, TM_D), lambda i: (i, 0, 0), memory_space=pltpu.SMEM),
                  pl.BlockSpec((TM_D, XE_W), lambda i: (i, 0)), _ANY],
        out_specs=_ANY,
        out_shape=jax.ShapeDtypeStruct((N_SORTED, XE_W), F32),
        scratch_shapes=[pltpu.SemaphoreType.DMA(())],
        input_output_aliases={2: 0},
        compiler_params=_params("arbitrary"),
        name="dispatch",
    )(pos, xe, xs_prev)


def _combine(pos, ys):
    return pl.pallas_call(
        _combine_kernel,
        grid=(N_TILES,),
        in_specs=[_POS_SPEC, _ANY],
        out_specs=list(_pair_specs(D_MODEL)),
        out_shape=[jax.ShapeDtypeStruct((NP, D_MODEL), F32), jax.ShapeDtypeStruct((NS, D_MODEL), F32)],
        scratch_shapes=[pltpu.SemaphoreType.DMA(())],
        compiler_params=_params("arbitrary"),
        name="combine",
    )(pos, ys)


def _experts_kernel(ea_ref, eb_ref, new_a_ref, new_b_ref, nused_ref, xs_ref,
                    wga_ref, wua_ref, wda_ref, wgb_ref, wub_ref, wdb_ref, g2_ref, b2_ref, ys_ref,
                    wga_s, wua_s, wda_s, wgb_s, wub_s, wdb_s):
    del ea_ref, eb_ref
    i = pl.program_id(0)
    used = i < nused_ref[0]

    @pl.when(jnp.logical_not(used))
    def _():
        ys_ref[...] = jnp.zeros_like(ys_ref)

    @pl.when(new_a_ref[i] == 1)
    def _():
        for src, dst in ((wga_ref, wga_s), (wua_ref, wua_s), (wda_ref, wda_s)):
            dst[...] = src[...].astype(BF16)

    @pl.when(new_b_ref[i] == 1)
    def _():
        for src, dst in ((wgb_ref, wgb_s), (wub_ref, wub_s), (wdb_ref, wdb_s)):
            dst[...] = src[...].astype(BF16)

    @pl.when(used)
    def _():
        x1 = xs_ref[:, 0:D_MODEL]
        xb = x1.astype(BF16)

        def expert(wg, wu, wd):
            a = _dot(xb, wg[...])
            h = a * _sigmoid(a) * _dot(xb, wu[...])
            return _dot(h.astype(BF16), wd[...])

        ffn = xs_ref[:, D_MODEL + 1:D_MODEL + 2] * expert(wga_s, wua_s, wda_s)
        ffn = ffn + xs_ref[:, D_MODEL + 2:D_MODEL + 3] * expert(wgb_s, wub_s, wdb_s)
        ys_ref[...] = _layer_norm(DN_ALPHA * x1 + ffn, g2_ref[...], b2_ref[...])


def _experts(tile_a, tile_b, new_a, new_b, nused, xs, w_gate, w_up, w_down, g2, b2, layer):
    tile = lambda w: pl.BlockSpec((TM_E, w), lambda i, ea, eb, na, nb, nu: (jnp.minimum(i, nu[0] - 1), 0))
    wa = lambda *shape: pl.BlockSpec((None, None) + shape, lambda i, ea, eb, na, nb, nu: (layer, ea[i], 0, 0))
    wb = lambda *shape: pl.BlockSpec((None, None) + shape, lambda i, ea, eb, na, nb, nu: (layer, eb[i], 0, 0))
    lay = pl.BlockSpec((None, 1, D_MODEL), lambda i, ea, eb, na, nb, nu: (layer, 0, 0))
    up, down = pltpu.VMEM((D_MODEL, D_EXPERT), BF16), pltpu.VMEM((D_EXPERT, D_MODEL), BF16)
    return pl.pallas_call(
        _experts_kernel,
        grid_spec=pltpu.PrefetchScalarGridSpec(
            num_scalar_prefetch=5,
            grid=(NT_E,),
            in_specs=[tile(XE_W),
                      wa(D_MODEL, D_EXPERT), wa(D_MODEL, D_EXPERT), wa(D_EXPERT, D_MODEL),
                      wb(D_MODEL, D_EXPERT), wb(D_MODEL, D_EXPERT), wb(D_EXPERT, D_MODEL),
                      lay, lay],
            out_specs=pl.BlockSpec((TM_E, D_MODEL), lambda i, ea, eb, na, nb, nu: (i, 0)),
            scratch_shapes=[up, up, down, up, up, down],
        ),
        out_shape=jax.ShapeDtypeStruct((N_SORTED, D_MODEL), F32),
        compiler_params=_params("arbitrary"),
        name="experts",
    )(tile_a, tile_b, new_a, new_b, nused, xs, w_gate, w_up, w_down, w_gate, w_up, w_down, g2, b2)


_PAIR_LO = tuple(a for a, _ in PAIRS)
_PAIR_HI = tuple(b for _, b in PAIRS)


def kernel(x_prompt, x_sample, cache_attn_k, cache_attn_v, state_gla, state_pool, w_in, w_gate, attn_sink,
           gla_w_a2, gla_b_a, gla_norm_g, pool_w, pool_scale, w_branch, w_o, ln1_g, ln1_b, ln2_g, ln2_b,
           w_router, router_bias, w_e_gate, w_e_up, w_e_down):
    ag0 = PA_W + PG_W
    w_in_p = (w_in[:, :, :ag0].astype(BF16), w_in[:, :, ag0 + GLA_RANK:].astype(BF16),
              jnp.pad(w_in[:, :, ag0:ag0 + GLA_RANK], ((0, 0), (0, 0), (0, LANES - GLA_RANK))).astype(BF16))
    w_gate_b, w_branch_b, w_o_b = w_gate.astype(BF16), w_branch.astype(BF16), w_o.astype(BF16)
    wa2_p = jnp.pad(gla_w_a2, ((0, 0), (0, LANES - GLA_RANK), (0, 0))).astype(BF16)
    pool_w_b = pool_w.astype(BF16)
    pool_scale_r = pool_scale.reshape(DEPTH, 1, POOL_W)
    wr_t = w_router.T.astype(BF16)
    rb = router_bias.reshape(N_EXPERTS, 1).astype(F32)
    cache_k = cache_attn_k.reshape(DEPTH, DEC_BATCH, WINDOW, KV_W)
    cache_v = cache_attn_v.reshape(DEPTH, DEC_BATCH, WINDOW, KV_W)
    pair_lo, pair_hi = jnp.array(_PAIR_LO, I32), jnp.array(_PAIR_HI, I32)

    x = (x_prompt.reshape(NP, D_MODEL), x_sample.reshape(NS, D_MODEL))
    xs = jnp.zeros((N_SORTED, XE_W), F32)
    ys = pos = None
    pk, pv, ps, pp, sk, sv, ss, sp = ([] for _ in range(8))
    for l in range(DEPTH):
        if l == 0:
            pa, pg, pu, pag, op_p, u_last = _in_proj(x, w_in_p, pool_w_b, pool_scale_r, l)
        else:
            pa, pg, pu, pag, op_p, u_last, x = _in_proj(ys, w_in_p, pool_w_b, pool_scale_r, l, pos)
        row2 = lambda a: a[l].reshape(1, -1)

        oa_p, k_p, v_p = _attn_prompt(pa, attn_sink[l])
        oa_s, nk, nv = _attn_sample(pa, attn_sink[l], cache_k, cache_v, l)
        og_p, s_p = _gla_prompt(pg, pag, wa2_p[l], row2(gla_b_a), row2(gla_norm_g))
        og_s, s_s = _gla_sample(pg, pag, wa2_p[l], row2(gla_b_a), row2(gla_norm_g), state_gla, l)
        u_s = pu[NP:].reshape(DEC_BATCH, DEC_SEQ, POOL_W)
        ext_s = jnp.concatenate(
            [jnp.zeros((DEC_BATCH, POOL_HIST - POOL_STATE, POOL_W), F32), state_pool[l], u_s], axis=1)
        op_s = _pool_sample(ext_s, pool_w_b[l], row2(pool_scale))

        xe, meta = _merge(x, (oa_p, oa_s, og_p, og_s, op_p, op_s), w_gate_b, w_branch_b, w_o_b,
                          ln1_g.reshape(DEPTH, 1, D_MODEL), ln1_b.reshape(DEPTH, 1, D_MODEL), wr_t, rb, l)
        cls = meta[:, 0, :].reshape(N_TOK // LANES, LANES).astype(I32)
        cls2d = jnp.pad(cls, ((0, PLAN_ROWS - N_TOK // LANES), (0, 0)), constant_values=-1)
        pos2d, tcls, nused = _plan(cls2d)
        pos = pos2d.reshape(-1)[:N_TOK].reshape(N_TILES, 1, TM)
        n_used = nused[0, :1]
        tile_cls = tcls[0, :NT_E]
        tile_cls = jnp.where(jnp.arange(NT_E) < n_used, tile_cls, tile_cls[n_used[0] - 1])
        tile_a = (tile_cls // len(PAIRS)) * EXPERTS_PER_GROUP + pair_lo[tile_cls % len(PAIRS)]
        tile_b = (tile_cls // len(PAIRS)) * EXPERTS_PER_GROUP + pair_hi[tile_cls % len(PAIRS)]
        first = jnp.ones((1,), I32)
        new_a = jnp.concatenate([first, (tile_a[1:] != tile_a[:-1]).astype(I32)])
        new_b = jnp.concatenate([first, (tile_b[1:] != tile_b[:-1]).astype(I32)])
        xs = _dispatch(pos, xe, xs)
        ys = _experts(tile_a, tile_b, new_a, new_b, n_used, xs, w_e_gate, w_e_up, w_e_down,
                      ln2_g.reshape(DEPTH, 1, D_MODEL), ln2_b.reshape(DEPTH, 1, D_MODEL), l)

        pk.append(k_p.reshape(BATCH, WINDOW, N_KV_HEADS, HEAD_DIM))
        pv.append(v_p.reshape(BATCH, WINDOW, N_KV_HEADS, HEAD_DIM))
        ps.append(s_p)
        pp.append(u_last[:, POOL_HIST - POOL_STATE:, :])
        sk.append(nk.reshape(DEC_BATCH, WINDOW, N_KV_HEADS, HEAD_DIM))
        sv.append(nv.reshape(DEC_BATCH, WINDOW, N_KV_HEADS, HEAD_DIM))
        ss.append(s_s)
        sp.append(ext_s[:, POOL_HIST + DEC_SEQ - POOL_STATE:, :])
    y_prompt, y_sample = _combine(pos, ys)
    return (y_prompt.reshape(BATCH, SEQ, D_MODEL), y_sample.reshape(DEC_BATCH, DEC_SEQ, D_MODEL),
            jnp.stack(pk), jnp.stack(pv), jnp.stack(ps), jnp.stack(pp),
            jnp.stack(sk), jnp.stack(sv), jnp.stack(ss), jnp.stack(sp))
```

```python
import jax
import jax.numpy as jnp
import numpy as np
from jax import lax
from jax.experimental import pallas as pl
from jax.experimental.pallas import tpu as pltpu

F32, BF16, I32 = jnp.float32, jnp.bfloat16, jnp.int32

D_MODEL = 1024
BATCH = 8
SEQ = 2048
DEPTH = 4
DEC_BATCH = 128
DEC_SEQ = 8
N_HEADS = 8
N_KV_HEADS = 2
HEAD_DIM = 64
WINDOW = 128
ATTN_W = N_HEADS * HEAD_DIM
KV_W = N_KV_HEADS * HEAD_DIM
Q_PER_KV = N_HEADS // N_KV_HEADS
GLA_HEADS = 4
GLA_DK = 64
GLA_DV = 128
GLA_KW = GLA_HEADS * GLA_DK
GLA_VW = GLA_HEADS * GLA_DV
GLA_RANK = 16
GLA_TAU = 16.0
GLA_CHUNK = 64
POOL_WINDOWS = (2, 4, 8, 16)
POOL_GROUPS = 4
POOL_GW = 128
POOL_W = POOL_GROUPS * POOL_GW
POOL_STATE = max(POOL_WINDOWS) - 1
N_BRANCH = 3
BRANCH_W = 512
N_EXPERTS = 16
N_GROUPS = 4
EXPERTS_PER_GROUP = N_EXPERTS // N_GROUPS
D_EXPERT = 512
DN_ALPHA = (2.0 * DEPTH) ** 0.25
LN_EPS = 1e-5
RMS_EPS = 1e-6
NEG_INF = -1e30
ALIBI_SLOPES = tuple(2.0 ** (-8.0 * h / N_HEADS) for h in range(1, N_HEADS + 1))

LANES = 128
SUBLANES = 8

NP = BATCH * SEQ
NS = DEC_BATCH * DEC_SEQ
N_TOK = NP + NS
TM = 512
N_TILES = N_TOK // TM
PT = NP // TM

PA_W = ATTN_W + 2 * KV_W
PG_W = 2 * GLA_KW + 2 * GLA_VW

PAIRS = ((0, 1), (0, 2), (0, 3), (1, 3), (2, 3), (2, 1))
N_CLASSES = N_GROUPS * len(PAIRS)
TM_E = 512
NT_E = -(-(N_TOK + N_CLASSES * (TM_E - 1)) // TM_E)
N_SORTED = NT_E * TM_E
XE_W = D_MODEL + LANES
PLAN_ROWS = 256

VMEM_LIMIT = 56 * 1024 * 1024
DMA_THREADS = 2


def _params(*sem):
    return pltpu.CompilerParams(dimension_semantics=sem, vmem_limit_bytes=VMEM_LIMIT)


def _dot(a, b):
    return jnp.dot(a, b, preferred_element_type=F32)


def _dot_nt(a, b):
    return lax.dot_general(a, b, (((1,), (1,)), ((), ())), preferred_element_type=F32)


def _dot_tn(a, b):
    return lax.dot_general(a, b, (((0,), (0,)), ((), ())), preferred_element_type=F32)


def _bdot_nt(a, b):
    return lax.dot_general(a, b, (((2,), (2,)), ((0,), (0,))), preferred_element_type=F32)


def _bdot(a, b):
    return lax.dot_general(a, b, (((2,), (1,)), ((0,), (0,))), preferred_element_type=F32)


def _layer_norm(h, g, b):
    mu = jnp.mean(h, axis=-1, keepdims=True)
    hc = h - mu
    var = jnp.mean(hc * hc, axis=-1, keepdims=True)
    return hc * lax.rsqrt(var + LN_EPS) * g + b


def _sigmoid(x):
    return 0.5 * jnp.tanh(0.5 * x) + 0.5


def _log_sigmoid(x):
    return jnp.minimum(x, 0.0) - jnp.log1p(jnp.exp(-jnp.abs(x)))


def _pair_specs(width):
    return (pl.BlockSpec((TM, width), lambda i, *_: (jnp.minimum(i, PT - 1), 0)),
            pl.BlockSpec((TM, width), lambda i, *_: (jnp.maximum(i - PT, 0), 0)))


def _pair_tile(p_ref, s_ref):
    return jnp.where(pl.program_id(0) < PT, p_ref[...], s_ref[...])


_IN_PROJ_WIDTHS = (PA_W, PG_W, POOL_W, LANES)


def _in_proj_tile(x, w_refs, pool_refs, out_refs, hist_ref):
    w_main, w_pool, w_rank = w_refs
    pw_ref, ps_ref = pool_refs
    pa_ref, pg_ref, pu_ref, pag_ref, op_ref, last_ref = out_refs
    xb = x.astype(BF16)
    pa_ref[...] = _dot(xb, w_main[:, 0:PA_W])
    pg_ref[...] = _dot(xb, w_main[:, PA_W:PA_W + PG_W])
    u = _dot(xb, w_pool[...])
    pu_ref[...] = u
    pag_ref[...] = _dot(xb, w_rank[...])

    i = pl.program_id(0)
    tiles_per_seq = SEQ // TM

    @pl.when(i < PT)
    def _():
        j = i % tiles_per_seq

        @pl.when(j == 0)
        def _():
            hist_ref[...] = jnp.zeros_like(hist_ref)

        ext = jnp.concatenate([hist_ref[...], u], axis=0)
        pos = j * TM + lax.broadcasted_iota(I32, (TM, 1), 0)
        cnt = [jnp.minimum(pos + 1, w).astype(F32) for w in POOL_WINDOWS]
        for g, y in enumerate(_pool_groups(ext, cnt, pw_ref, ps_ref, TM)):
            op_ref[:, g * POOL_GW:(g + 1) * POOL_GW] = y
        hist_ref[...] = u[TM - POOL_HIST:, :]

        @pl.when(j == tiles_per_seq - 1)
        def _():
            last_ref[...] = u[TM - POOL_HIST:, :]


def _in_proj_kernel(xp_ref, xs_ref, wm_ref, wp_ref, wr_ref, pw_ref, ps_ref,
                    pa_ref, pg_ref, pu_ref, pag_ref, op_ref, last_ref, hist_ref):
    _in_proj_tile(_pair_tile(xp_ref, xs_ref), (wm_ref, wp_ref, wr_ref), (pw_ref, ps_ref),
                  (pa_ref, pg_ref, pu_ref, pag_ref, op_ref, last_ref), hist_ref)


def _in_proj_gather_kernel(pos_ref, pos_next_ref, ys_hbm, wm_ref, wp_ref, wr_ref, pw_ref, ps_ref,
                           pa_ref, pg_ref, pu_ref, pag_ref, op_ref, last_ref, x_ref, buf, sem, hist_ref):
    i = pl.program_id(0)
    slot = i % 2

    def copy(p_ref, r, s):
        return pltpu.make_async_copy(ys_hbm.at[pl.ds(p_ref[0, r], 1)], buf.at[s, pl.ds(r, 1)], sem.at[s])

    def wait_tile(p_ref, s):
        def wait(r, carry):
            copy(p_ref, r, s).wait()
            return carry
        lax.fori_loop(0, TM, wait, 0, unroll=8)

    @pl.when(i == 0)
    def _():
        for r in range(TM):
            copy(pos_ref, r, 0).start(priority=r % DMA_THREADS)

    wait_tile(pos_ref, slot)
    x = buf[slot]
    for r in range(TM):
        copy(pos_next_ref, r, 1 - slot).start(priority=r % DMA_THREADS)
    x_ref[...] = x
    _in_proj_tile(x, (wm_ref, wp_ref, wr_ref), (pw_ref, ps_ref),
                  (pa_ref, pg_ref, pu_ref, pag_ref, op_ref, last_ref), hist_ref)

    @pl.when(i == pl.num_programs(0) - 1)
    def _():
        wait_tile(pos_next_ref, 1 - slot)


def _in_proj(x, w_in_parts, pool_w, pool_scale, layer, pos=None):
    row = lambda w: pl.BlockSpec((TM, w), lambda i: (i, 0))
    w_specs = [pl.BlockSpec((None,) + w.shape[1:], lambda i: (layer, 0, 0)) for w in w_in_parts]
    w_specs += [pl.BlockSpec((None, POOL_GROUPS, POOL_GW, POOL_GW), lambda i: (layer, 0, 0, 0)),
                pl.BlockSpec((None, 1, POOL_W), lambda i: (layer, 0, 0))]
    weights = (*w_in_parts, pool_w, pool_scale)
    out_specs = [row(w) for w in _IN_PROJ_WIDTHS]
    out_specs += [pl.BlockSpec((TM, POOL_W), lambda i: (jnp.minimum(i, PT - 1), 0)),
                  pl.BlockSpec((None, POOL_HIST, POOL_W), lambda i: (jnp.minimum(i // (SEQ // TM), BATCH - 1), 0, 0))]
    outs = [jax.ShapeDtypeStruct((N_TOK, w), F32) for w in _IN_PROJ_WIDTHS]
    outs += [jax.ShapeDtypeStruct((NP, POOL_W), BF16), jax.ShapeDtypeStruct((BATCH, POOL_HIST, POOL_W), F32)]
    hist = pltpu.VMEM((POOL_HIST, POOL_W), F32)
    if pos is None:
        return pl.pallas_call(
            _in_proj_kernel,
            grid=(N_TILES,),
            in_specs=[*_pair_specs(D_MODEL), *w_specs],
            out_specs=out_specs,
            out_shape=outs,
            scratch_shapes=[hist],
            compiler_params=_params("arbitrary"),
            name="in_proj",
        )(*x, *weights)
    pos_next = pl.BlockSpec((None, 1, TM), lambda i: (jnp.minimum(i + 1, N_TILES - 1), 0, 0), memory_space=pltpu.SMEM)
    return pl.pallas_call(
        _in_proj_gather_kernel,
        grid=(N_TILES,),
        in_specs=[_POS_SPEC, pos_next, _ANY, *w_specs],
        out_specs=out_specs + [row(D_MODEL)],
        out_shape=outs + [jax.ShapeDtypeStruct((N_TOK, D_MODEL), F32)],
        scratch_shapes=[pltpu.VMEM((2, TM, D_MODEL), F32), pltpu.SemaphoreType.DMA((2,)), hist],
        compiler_params=_params("arbitrary"),
        name="in_proj_gather",
    )(pos, pos, x, *weights)


def _softmax_sink_pv(parts, sink, pv):
    m = sink
    for s, _ in parts:
        m = jnp.maximum(m, jnp.max(s, axis=-1, keepdims=True))
    den = jnp.exp(sink - m)
    es = []
    for s, _ in parts:
        e = jnp.exp(s - m)
        den = den + jnp.sum(e, axis=-1, keepdims=True)
        es.append(e)
    inv = 1.0 / den
    out = None
    for e, (_, v) in zip(es, parts):
        o = pv((e * inv).astype(BF16), v)
        out = o if out is None else out + o
    return out


def _head_column(sink_ref, kv, rows_per_head):
    g = lax.broadcasted_iota(I32, (Q_PER_KV * rows_per_head, 1), 0) // rows_per_head
    col = jnp.zeros(g.shape, F32)
    for i in range(Q_PER_KV):
        col = jnp.where(g == i, sink_ref[kv * Q_PER_KV + i], col)
    return col


def _alibi_bias(dist, visible):
    out = np.empty((N_KV_HEADS, Q_PER_KV * dist.shape[0], dist.shape[1]), np.float32)
    for h in range(N_HEADS):
        kv, g = divmod(h, Q_PER_KV)
        out[kv, g * dist.shape[0]:(g + 1) * dist.shape[0]] = np.where(visible, -ALIBI_SLOPES[h] * dist, NEG_INF)
    return out


ATTN_QB = 4


def _attn_prompt_kernel(sink_ref, cur_ref, prev_ref, bias0_ref, bias_ref, o_ref, nk_ref, nv_ref):
    k = jnp.concatenate([prev_ref[:, 0:KV_W], cur_ref[:, ATTN_W:ATTN_W + KV_W]], axis=0).astype(BF16)
    vt = jnp.concatenate([prev_ref[:, KV_W:2 * KV_W], cur_ref[:, ATTN_W + KV_W:PA_W]], axis=0).T.astype(BF16)
    g_of_col = lax.broadcasted_iota(I32, (1, Q_PER_KV * WINDOW), 1) // WINDOW
    for kv in range(N_KV_HEADS):
        ks = slice(kv * HEAD_DIM, (kv + 1) * HEAD_DIM)
        heads = range(kv * Q_PER_KV, (kv + 1) * Q_PER_KV)
        sink = jnp.zeros(g_of_col.shape, F32)
        for g, h in enumerate(heads):
            sink = jnp.where(g_of_col == g, sink_ref[h], sink)
        for j in range(ATTN_QB):
            rows = slice(j * WINDOW, (j + 1) * WINDOW)
            keys = slice(j * WINDOW, (j + 2) * WINDOW)
            q = jnp.concatenate([cur_ref[rows, h * HEAD_DIM:(h + 1) * HEAD_DIM] for h in heads], axis=0)
            bias = bias0_ref[kv] if j == 0 else bias_ref[kv]
            st = _dot_nt(k[keys, ks], (q * (HEAD_DIM ** -0.5)).astype(BF16)) + bias
            m = jnp.maximum(sink, jnp.max(st, axis=0, keepdims=True))
            e = jnp.exp(st - m)
            inv = 1.0 / (jnp.exp(sink - m) + jnp.sum(e, axis=0, keepdims=True))
            ot = _dot(vt[ks, keys], (e * inv).astype(BF16))
            for pair in range(Q_PER_KV // 2):
                two = jnp.concatenate(
                    [ot[:, (2 * pair + g) * WINDOW:(2 * pair + g + 1) * WINDOW] for g in range(2)], axis=0)
                lo = (kv * Q_PER_KV + 2 * pair) * HEAD_DIM
                o_ref[rows, lo:lo + 2 * HEAD_DIM] = two.T.astype(BF16)

    @pl.when(pl.program_id(1) == pl.num_programs(1) - 1)
    def _():
        last = slice((ATTN_QB - 1) * WINDOW, ATTN_QB * WINDOW)
        nk_ref[...] = cur_ref[last, ATTN_W:ATTN_W + KV_W]
        nv_ref[...] = cur_ref[last, ATTN_W + KV_W:PA_W]


def _attn_prompt(pa, sink):
    nb = SEQ // WINDOW
    ns = nb // ATTN_QB
    r = np.arange(WINDOW)[:, None]
    c = np.arange(2 * WINDOW)[None, :]
    band = (c > r) & (c <= WINDOW + r)
    bias = np.stack([_alibi_bias(WINDOW + r - c, band & (c >= WINDOW)), _alibi_bias(WINDOW + r - c, band)])
    bias = jnp.asarray(bias.transpose(0, 1, 3, 2))
    bias_spec = lambda variant: pl.BlockSpec((None, N_KV_HEADS, 2 * WINDOW, Q_PER_KV * WINDOW),
                                             lambda b, i, s: (variant(i), 0, 0, 0))
    state = pl.BlockSpec((None, WINDOW, KV_W), lambda b, i, s: (b, 0, 0))
    return pl.pallas_call(
        _attn_prompt_kernel,
        grid_spec=pltpu.PrefetchScalarGridSpec(
            num_scalar_prefetch=1,
            grid=(BATCH, ns),
            in_specs=[
                pl.BlockSpec((ATTN_QB * WINDOW, PA_W), lambda b, i, s: (b * ns + i, 0)),
                pl.BlockSpec((WINDOW, 2 * KV_W),
                             lambda b, i, s: (b * nb + jnp.maximum(ATTN_QB * i - 1, 0), ATTN_W // (2 * KV_W))),
                bias_spec(lambda i: jnp.minimum(i, 1)), bias_spec(lambda i: 1),
            ],
            out_specs=[pl.BlockSpec((ATTN_QB * WINDOW, ATTN_W), lambda b, i, s: (b * ns + i, 0)), state, state],
        ),
        out_shape=[jax.ShapeDtypeStruct((NP, ATTN_W), BF16),
                   jax.ShapeDtypeStruct((BATCH, WINDOW, KV_W), F32),
                   jax.ShapeDtypeStruct((BATCH, WINDOW, KV_W), F32)],
        compiler_params=_params("arbitrary", "arbitrary"),
        name="attn_prompt",
    )(sink, pa, pa, bias, bias)


ATTN_SB = 16


def _attn_sample_kernel(sink_ref, cur_ref, kc_ref, vc_ref, bias_c_ref, bias_n_ref, o_ref, nk_ref, nv_ref):
    cur = cur_ref[...].reshape(ATTN_SB, DEC_SEQ, PA_W)
    for kv in range(N_KV_HEADS):
        ks = slice(kv * HEAD_DIM, (kv + 1) * HEAD_DIM)
        heads = range(kv * Q_PER_KV, (kv + 1) * Q_PER_KV)
        q = jnp.concatenate([cur[:, :, h * HEAD_DIM:(h + 1) * HEAD_DIM] for h in heads], axis=1)
        q = (q * (HEAD_DIM ** -0.5)).astype(BF16)
        kn = cur[:, :, ATTN_W + kv * HEAD_DIM:ATTN_W + (kv + 1) * HEAD_DIM].astype(BF16)
        vn = cur[:, :, ATTN_W + KV_W + kv * HEAD_DIM:ATTN_W + KV_W + (kv + 1) * HEAD_DIM].astype(BF16)
        sc = _bdot_nt(q, kc_ref[:, :, ks].astype(BF16)) + bias_c_ref[kv]
        sn = _bdot_nt(q, kn) + bias_n_ref[kv]
        sink = _head_column(sink_ref, kv, DEC_SEQ)
        o = _softmax_sink_pv([(sc, vc_ref[:, :, ks].astype(BF16)), (sn, vn)], sink, _bdot)
        for g, h in enumerate(heads):
            o_ref[:, h * HEAD_DIM:(h + 1) * HEAD_DIM] = (
                o[:, g * DEC_SEQ:(g + 1) * DEC_SEQ, :].reshape(ATTN_SB * DEC_SEQ, HEAD_DIM).astype(BF16))
    keep = WINDOW - DEC_SEQ
    nk_ref[:, 0:keep, :] = kc_ref[:, DEC_SEQ:WINDOW, :]
    nk_ref[:, keep:WINDOW, :] = cur[:, :, ATTN_W:ATTN_W + KV_W]
    nv_ref[:, 0:keep, :] = vc_ref[:, DEC_SEQ:WINDOW, :]
    nv_ref[:, keep:WINDOW, :] = cur[:, :, ATTN_W + KV_W:PA_W]


def _attn_sample(pa, sink, cache_k, cache_v, layer):
    rows = ATTN_SB * DEC_SEQ
    first = NP // rows
    t = np.arange(DEC_SEQ)[:, None]
    jc = np.arange(WINDOW)[None, :]
    jn = np.arange(DEC_SEQ)[None, :]
    bias_c = jnp.asarray(_alibi_bias(WINDOW + t - jc, jc > t))
    bias_n = jnp.asarray(_alibi_bias(t - jn, jn <= t))
    cache = pl.BlockSpec((None, ATTN_SB, WINDOW, KV_W), lambda j, s: (layer, j, 0, 0))
    new = pl.BlockSpec((ATTN_SB, WINDOW, KV_W), lambda j, s: (j, 0, 0))
    const = lambda a: pl.BlockSpec(a.shape, lambda j, s: (0,) * a.ndim)
    return pl.pallas_call(
        _attn_sample_kernel,
        grid_spec=pltpu.PrefetchScalarGridSpec(
            num_scalar_prefetch=1,
            grid=(DEC_BATCH // ATTN_SB,),
            in_specs=[pl.BlockSpec((rows, PA_W), lambda j, s: (first + j, 0)), cache, cache,
                      const(bias_c), const(bias_n)],
            out_specs=[pl.BlockSpec((rows, ATTN_W), lambda j, s: (j, 0)), new, new],
        ),
        out_shape=[jax.ShapeDtypeStruct((NS, ATTN_W), BF16),
                   jax.ShapeDtypeStruct((DEC_BATCH, WINDOW, KV_W), F32),
                   jax.ShapeDtypeStruct((DEC_BATCH, WINDOW, KV_W), F32)],
        compiler_params=_params("arbitrary"),
        name="attn_sample",
    )(sink, pa, cache_k, cache_v, bias_c, bias_n)


def _chunk_cumsum(x, chunk):
    pos = lax.broadcasted_iota(I32, x.shape, 0) % chunk
    sh = 1
    while sh < chunk:
        x = x + jnp.where(pos >= sh, pltpu.roll(x, sh, 0), 0.0)
        sh *= 2
    return x


def _gla_log_decay(pag_ref, wa2_ref, ba_ref):
    z = _dot(pag_ref[...].astype(BF16), wa2_ref[...]) + ba_ref[...]
    return _log_sigmoid(z) / GLA_TAU


def _gla_finish(o, rg, g):
    o = o * lax.rsqrt(jnp.mean(o * o, axis=-1, keepdims=True) + RMS_EPS)
    return (o * g * (rg * _sigmoid(rg))).astype(BF16)


GLA_TT = 512
GLA_AG = 128


def _gla_prompt_kernel(pg_ref, pag_ref, wa2_ref, ba_ref, g_ref, o_ref, s_ref,
                       qd_ref, kd_ref, kdec_ref, a_ref, sall_ref, st_ref, acc_ref):
    nc = GLA_TT // GLA_CHUNK

    @pl.when(pl.program_id(1) == 0)
    def _():
        st_ref[...] = jnp.zeros_like(st_ref)

    cum = _chunk_cumsum(_gla_log_decay(pag_ref, wa2_ref, ba_ref), GLA_CHUNK)
    cum3 = cum.reshape(nc, GLA_CHUNK, GLA_KW)
    tot3 = cum3[:, GLA_CHUNK - 1:GLA_CHUNK, :]
    k = pg_ref[:, GLA_KW:2 * GLA_KW]
    qd_ref[...] = (pg_ref[:, 0:GLA_KW] * (GLA_DK ** -0.5) * jnp.exp(cum)).astype(BF16)
    kd_ref[...] = (k * jnp.exp(-cum)).astype(BF16)
    kdec_ref[...] = (k * jnp.exp(tot3 - cum3).reshape(GLA_TT, GLA_KW)).astype(BF16)
    etot = jnp.exp(tot3)

    def hs(h):
        return slice(h * GLA_DK, (h + 1) * GLA_DK)

    def vs(h):
        return slice(2 * GLA_KW + h * GLA_DV, 2 * GLA_KW + (h + 1) * GLA_DV)

    for c in range(nc):
        rows = slice(c * GLA_CHUNK, (c + 1) * GLA_CHUNK)
        for h in range(GLA_HEADS):
            a_ref[c, :, hs(h)] = _dot_tn(pg_ref[rows, vs(h)].astype(BF16), kdec_ref[rows, hs(h)])

    st = st_ref[...]
    for c in range(nc):
        sall_ref[c] = st.astype(BF16)
        st = etot[c] * st + a_ref[c]
    st_ref[...] = st

    r = lax.broadcasted_iota(I32, (GLA_AG, GLA_AG), 0)
    col = lax.broadcasted_iota(I32, (GLA_AG, GLA_AG), 1)
    causal = (r // GLA_CHUNK == col // GLA_CHUNK) & (col <= r)
    for h in range(GLA_HEADS):
        out = slice(h * GLA_DV, (h + 1) * GLA_DV)
        for c in range(nc):
            rows = slice(c * GLA_CHUNK, (c + 1) * GLA_CHUNK)
            acc_ref[rows, out] = _dot_nt(qd_ref[rows, hs(h)], sall_ref[c, :, hs(h)])
        for a in range(GLA_TT // GLA_AG):
            rows = slice(a * GLA_AG, (a + 1) * GLA_AG)
            att = jnp.where(causal, _dot_nt(qd_ref[rows, hs(h)], kd_ref[rows, hs(h)]), 0.0)
            acc_ref[rows, out] += _dot(att.astype(BF16), pg_ref[rows, vs(h)].astype(BF16))
        rg = pg_ref[:, 2 * GLA_KW + GLA_VW + h * GLA_DV:2 * GLA_KW + GLA_VW + (h + 1) * GLA_DV]
        o_ref[:, out] = _gla_finish(acc_ref[:, out], rg, g_ref[:, out])

    @pl.when(pl.program_id(1) == pl.num_programs(1) - 1)
    def _():
        for h in range(GLA_HEADS):
            s_ref[h] = st_ref[:, hs(h)].T


def _gla_prompt(pg, pag, wa2, ba, g):
    nt = SEQ // GLA_TT
    nc = GLA_TT // GLA_CHUNK
    const = lambda shape: pl.BlockSpec(shape, lambda b, j: (0,) * len(shape))
    return pl.pallas_call(
        _gla_prompt_kernel,
        grid=(BATCH, nt),
        in_specs=[
            pl.BlockSpec((GLA_TT, PG_W), lambda b, j: (b * nt + j, 0)),
            pl.BlockSpec((GLA_TT, LANES), lambda b, j: (b * nt + j, 0)),
            const((LANES, GLA_KW)), const((1, GLA_KW)), const((1, GLA_VW)),
        ],
        out_specs=[
            pl.BlockSpec((GLA_TT, GLA_VW), lambda b, j: (b * nt + j, 0)),
            pl.BlockSpec((None, GLA_HEADS, GLA_DK, GLA_DV), lambda b, j: (b, 0, 0, 0)),
        ],
        out_shape=[jax.ShapeDtypeStruct((NP, GLA_VW), BF16),
                   jax.ShapeDtypeStruct((BATCH, GLA_HEADS, GLA_DK, GLA_DV), F32)],
        scratch_shapes=[pltpu.VMEM((GLA_TT, GLA_KW), BF16), pltpu.VMEM((GLA_TT, GLA_KW), BF16),
                        pltpu.VMEM((GLA_TT, GLA_KW), BF16),
                        pltpu.VMEM((nc, GLA_DV, GLA_KW), F32), pltpu.VMEM((nc, GLA_DV, GLA_KW), BF16),
                        pltpu.VMEM((GLA_DV, GLA_KW), F32), pltpu.VMEM((GLA_TT, GLA_VW), F32)],
        compiler_params=_params("arbitrary", "arbitrary"),
        name="gla_prompt",
    )(pg, pag, wa2, ba, g)


GLA_SB = 8


def _gla_sample_kernel(pg_ref, pag_ref, wa2_ref, ba_ref, g_ref, s0_ref, o_ref, s_ref):
    cum_all = _chunk_cumsum(_gla_log_decay(pag_ref, wa2_ref, ba_ref), DEC_SEQ)
    tri = (lax.broadcasted_iota(I32, (DEC_SEQ, DEC_SEQ), 0) >= lax.broadcasted_iota(I32, (DEC_SEQ, DEC_SEQ), 1))
    for s in range(GLA_SB):
        rows = slice(s * DEC_SEQ, (s + 1) * DEC_SEQ)
        cum = cum_all[rows, :]
        tot = cum[DEC_SEQ - 1:DEC_SEQ, :]
        q = pg_ref[rows, 0:GLA_KW] * (GLA_DK ** -0.5)
        k = pg_ref[rows, GLA_KW:2 * GLA_KW]
        qd = (q * jnp.exp(cum)).astype(BF16)
        kd = (k * jnp.exp(-cum)).astype(BF16)
        kdec = (k * jnp.exp(tot - cum)).astype(BF16)
        etot = jnp.exp(tot)
        etot_col = [jnp.broadcast_to(etot[:, p * LANES:(p + 1) * LANES], (SUBLANES, LANES)).T[:, 0:1]
                    for p in range(GLA_KW // LANES)]
        for h in range(GLA_HEADS):
            ks = slice(h * GLA_DK, (h + 1) * GLA_DK)
            vs = slice(2 * GLA_KW + h * GLA_DV, 2 * GLA_KW + (h + 1) * GLA_DV)
            rs = slice(2 * GLA_KW + GLA_VW + h * GLA_DV, 2 * GLA_KW + GLA_VW + (h + 1) * GLA_DV)
            v = pg_ref[rows, vs].astype(BF16)
            st = s0_ref[s, h]
            att = jnp.where(tri, _dot_nt(qd[:, ks], kd[:, ks]), 0.0)
            o = _dot(qd[:, ks], st.astype(BF16)) + _dot(att.astype(BF16), v)
            per = LANES // GLA_DK
            col = etot_col[h // per][(h % per) * GLA_DK:(h % per + 1) * GLA_DK, :]
            s_ref[s, h] = col * st + _dot_tn(kdec[:, ks], v)
            o_ref[rows, h * GLA_DV:(h + 1) * GLA_DV] = _gla_finish(o, pg_ref[rows, rs], g_ref[:, h * GLA_DV:(h + 1) * GLA_DV])


def _gla_sample(pg, pag, wa2, ba, g, state, layer):
    rows = GLA_SB * DEC_SEQ
    first = NP // rows
    const = lambda shape: pl.BlockSpec(shape, lambda j: (0,) * len(shape))
    return pl.pallas_call(
        _gla_sample_kernel,
        grid=(DEC_BATCH // GLA_SB,),
        in_specs=[
            pl.BlockSpec((rows, PG_W), lambda j: (first + j, 0)),
            pl.BlockSpec((rows, LANES), lambda j: (first + j, 0)),
            const((LANES, GLA_KW)), const((1, GLA_KW)), const((1, GLA_VW)),
            pl.BlockSpec((None, GLA_SB, GLA_HEADS, GLA_DK, GLA_DV), lambda j: (layer, j, 0, 0, 0)),
        ],
        out_specs=[
            pl.BlockSpec((rows, GLA_VW), lambda j: (j, 0)),
            pl.BlockSpec((GLA_SB, GLA_HEADS, GLA_DK, GLA_DV), lambda j: (j, 0, 0, 0)),
        ],
        out_shape=[jax.ShapeDtypeStruct((NS, GLA_VW), BF16),
                   jax.ShapeDtypeStruct((DEC_BATCH, GLA_HEADS, GLA_DK, GLA_DV), F32)],
        compiler_params=_params("arbitrary"),
        name="gla_sample",
    )(pg, pag, wa2, ba, g, state)


POOL_HIST = 16


def _pool_groups(ext, cnt, pw_ref, ps_ref, out_rows):
    ax = ext.ndim - 2
    outs = []
    for g, w in enumerate(POOL_WINDOWS):
        x = ext[..., g * POOL_GW:(g + 1) * POOL_GW]
        s, sh = x, 1
        while sh < w:
            s = s + pltpu.roll(s, sh, ax)
            sh *= 2
        if ext.ndim == 3:
            d = (s[:, POOL_HIST:, :] / cnt[g] - x[:, POOL_HIST:, :]).reshape(out_rows, POOL_GW)
        else:
            d = s[POOL_HIST:, :] / cnt[g] - x[POOL_HIST:, :]
        y = _dot(d.astype(BF16), pw_ref[g]) * ps_ref[:, g * POOL_GW:(g + 1) * POOL_GW]
        outs.append(y.astype(BF16))
    return outs


POOL_SB = 16


def _pool_sample_kernel(ext_ref, pw_ref, ps_ref, o_ref):
    cnt = [float(w) for w in POOL_WINDOWS]
    for g, y in enumerate(_pool_groups(ext_ref[...], cnt, pw_ref, ps_ref, POOL_SB * DEC_SEQ)):
        o_ref[:, g * POOL_GW:(g + 1) * POOL_GW] = y


def _pool_sample(ext, pw, ps):
    rows = POOL_SB * DEC_SEQ
    return pl.pallas_call(
        _pool_sample_kernel,
        grid=(DEC_BATCH // POOL_SB,),
        in_specs=[
            pl.BlockSpec((POOL_SB, POOL_HIST + DEC_SEQ, POOL_W), lambda j: (j, 0, 0)),
            pl.BlockSpec((POOL_GROUPS, POOL_GW, POOL_GW), lambda j: (0, 0, 0)),
            pl.BlockSpec((1, POOL_W), lambda j: (0, 0)),
        ],
        out_specs=pl.BlockSpec((rows, POOL_W), lambda j: (j, 0)),
        out_shape=jax.ShapeDtypeStruct((NS, POOL_W), BF16),
        compiler_params=_params("arbitrary"),
        name="pool_sample",
    )(ext, pw, ps)


def _route(sc, sel):
    gscore = []
    for g in range(N_GROUPS):
        v = sel[EXPERTS_PER_GROUP * g:EXPERTS_PER_GROUP * (g + 1)]
        best = None
        for a, b in PAIRS:
            pair = v[a] + v[b]
            best = pair if best is None else jnp.maximum(best, pair)
        gscore.append(best)
    gi = jnp.zeros_like(gscore[0], dtype=I32)
    best = gscore[0]
    for g in range(1, N_GROUPS):
        upd = gscore[g] > best
        gi = jnp.where(upd, g, gi)
        best = jnp.where(upd, gscore[g], best)

    def in_group(rows, j):
        out = rows[(N_GROUPS - 1) * EXPERTS_PER_GROUP + j]
        for g in range(N_GROUPS - 2, -1, -1):
            out = jnp.where(gi == g, rows[g * EXPERTS_PER_GROUP + j], out)
        return out

    u = [in_group(sel, j) for j in range(EXPERTS_PER_GROUP)]
    s_in = [in_group(sc, j) for j in range(EXPERTS_PER_GROUP)]

    def argmax4(vals):
        idx = jnp.zeros_like(gi)
        m = vals[0]
        for j in range(1, EXPERTS_PER_GROUP):
            upd = vals[j] > m
            idx = jnp.where(upd, j, idx)
            m = jnp.where(upd, vals[j], m)
        return idx

    def pick(vals, idx):
        out = vals[EXPERTS_PER_GROUP - 1]
        for j in range(EXPERTS_PER_GROUP - 2, -1, -1):
            out = jnp.where(idx == j, vals[j], out)
        return out

    i1 = argmax4(u)
    i2 = argmax4([jnp.where(i1 == j, NEG_INF, u[j]) for j in range(EXPERTS_PER_GROUP)])
    w1, w2 = pick(s_in, i1), pick(s_in, i2)
    tot = w1 + w2
    w1, w2 = w1 / tot, w2 / tot
    lo, hi = jnp.minimum(i1, i2), jnp.maximum(i1, i2)
    first_lo = i1 < i2
    w_lo, w_hi = jnp.where(first_lo, w1, w2), jnp.where(first_lo, w2, w1)
    pair = jnp.where(lo == 0, hi - 1, jnp.where(hi == EXPERTS_PER_GROUP - 1, lo + 2, len(PAIRS) - 1))
    swapped = pair == len(PAIRS) - 1
    return gi * len(PAIRS) + pair, jnp.where(swapped, w_hi, w_lo), jnp.where(swapped, w_lo, w_hi)


def _merge_kernel(*refs):
    (oap_ref, oas_ref, ogp_ref, ogs_ref, opp_ref, ops_ref, wg_ref, wb_ref, wo_ref,
     g1_ref, b1_ref, wr_ref, rb_ref, xe_ref, meta_ref) = refs[-15:]
    x = refs[0][...] if len(refs) == 16 else _pair_tile(refs[0], refs[1])
    xb = x.astype(BF16)
    merged = None
    for n, (brp, brs) in enumerate(((oap_ref, oas_ref), (ogp_ref, ogs_ref), (opp_ref, ops_ref))):
        gate = _sigmoid(_dot(xb, wg_ref[:, n * D_MODEL:(n + 1) * D_MODEL]))
        term = gate * _dot(_pair_tile(brp, brs), wb_ref[n])
        merged = term if merged is None else merged + term
    mix = _dot(merged.astype(BF16), wo_ref[...])
    x1 = _layer_norm(DN_ALPHA * x + mix, g1_ref[...], b1_ref[...])
    xe_ref[:, 0:D_MODEL] = x1
    sc_t = _sigmoid(_dot_nt(wr_ref[...], x1.astype(BF16)))
    sel_t = sc_t + rb_ref[...]
    sc = [sc_t[e:e + 1, :] for e in range(N_EXPERTS)]
    sel = [sel_t[e:e + 1, :] for e in range(N_EXPERTS)]
    cls, w_first, w_second = _route(sc, sel)
    cls = cls.astype(F32)

    def rows(n):
        rid = lax.broadcasted_iota(I32, (n, TM), 0)
        return jnp.where(rid == 0, cls, jnp.where(rid == 1, w_first, jnp.where(rid == 2, w_second, 0.0)))

    meta_ref[...] = rows(SUBLANES)
    xe_ref[:, D_MODEL:XE_W] = rows(LANES).T


def _merge(x, branches, wg, wb, wo, g1, b1, wr_t, rb, layer):
    row = lambda w: pl.BlockSpec((TM, w), lambda i: (i, 0))
    lay = lambda *shape: pl.BlockSpec((None,) + shape, lambda i: (layer,) + (0,) * len(shape))
    const = lambda *shape: pl.BlockSpec(shape, lambda i: (0,) * len(shape))
    return pl.pallas_call(
        _merge_kernel,
        grid=(N_TILES,),
        in_specs=[*(_pair_specs(D_MODEL) if isinstance(x, tuple) else (row(D_MODEL),)),
                  *(_pair_specs(BRANCH_W) * N_BRANCH),
                  lay(D_MODEL, N_BRANCH * D_MODEL), lay(N_BRANCH, BRANCH_W, D_MODEL), lay(D_MODEL, D_MODEL),
                  lay(1, D_MODEL), lay(1, D_MODEL), const(N_EXPERTS, D_MODEL), const(N_EXPERTS, 1)],
        out_specs=[row(XE_W), pl.BlockSpec((None, SUBLANES, TM), lambda i: (i, 0, 0))],
        out_shape=[jax.ShapeDtypeStruct((N_TOK, XE_W), F32), jax.ShapeDtypeStruct((N_TILES, SUBLANES, TM), F32)],
        compiler_params=_params("arbitrary"),
        name="merge",
    )(*(x if isinstance(x, tuple) else (x,)), *branches, wg, wb, wo, g1, b1, wr_t, rb)


def _plan_kernel(cls_ref, pos_ref, tcls_ref, nused_ref):
    cls = cls_ref[...]
    lane_r = lax.broadcasted_iota(I32, (LANES, 2 * LANES), 0)
    lane_c = lax.broadcasted_iota(I32, (LANES, 2 * LANES), 1)
    lane_mat = ((lane_c >= LANES) | (lane_r < lane_c)).astype(BF16)
    row_r = lax.broadcasted_iota(I32, (2 * PLAN_ROWS, PLAN_ROWS), 0)
    row_c = lax.broadcasted_iota(I32, (2 * PLAN_ROWS, PLAN_ROWS), 1)
    row_mat = ((row_r >= PLAN_ROWS) | (row_c < row_r)).astype(BF16)
    tile_start = (lax.broadcasted_iota(I32, (SUBLANES, LANES), 1) * TM_E).astype(F32)
    pos = jnp.zeros((PLAN_ROWS, LANES), F32)
    off = jnp.zeros((PLAN_ROWS, LANES), F32)
    tcls = jnp.zeros((SUBLANES, LANES), I32)
    for c in range(N_CLASSES):
        m = cls == c
        lanes = _dot(m.astype(BF16), lane_mat)
        rows = _dot(row_mat, lanes[:, LANES:].astype(BF16))
        rank = lanes[:, 0:LANES] + rows[0:PLAN_ROWS]
        count = rows[PLAN_ROWS:]
        pos = jnp.where(m, off + rank, pos)
        off = off + jnp.ceil(count * (1.0 / TM_E)) * TM_E
        tcls = tcls + (off[0:SUBLANES] <= tile_start).astype(I32)
    pos_ref[...] = pos.astype(I32)
    tcls_ref[...] = tcls
    nused_ref[...] = (off[0:SUBLANES] * (1.0 / TM_E)).astype(I32)


def _plan(cls2d):
    return pl.pallas_call(
        _plan_kernel,
        out_shape=[jax.ShapeDtypeStruct((PLAN_ROWS, LANES), I32),
                   jax.ShapeDtypeStruct((SUBLANES, LANES), I32),
                   jax.ShapeDtypeStruct((SUBLANES, LANES), I32)],
        compiler_params=pltpu.CompilerParams(vmem_limit_bytes=VMEM_LIMIT),
        name="plan",
    )(cls2d)


def _row_copies(pos_ref, tile_ref, sorted_hbm, sem, scatter):
    def copy(r):
        row, srt = tile_ref.at[pl.ds(r, 1)], sorted_hbm.at[pl.ds(pos_ref[0, r], 1)]
        return pltpu.make_async_copy(row, srt, sem) if scatter else pltpu.make_async_copy(srt, row, sem)

    def wait(r, carry):
        copy(r).wait()
        return carry

    rows = tile_ref.shape[0]
    for r in range(rows):
        copy(r).start(priority=r % DMA_THREADS)
    lax.fori_loop(0, rows, wait, 0, unroll=8)


def _dispatch_kernel(pos_ref, x_ref, xs_in_hbm, xs_hbm, sem):
    del xs_in_hbm
    _row_copies(pos_ref, x_ref, xs_hbm, sem, scatter=True)


def _combine_kernel(pos_ref, ys_hbm, xp_ref, xs_ref, sem):
    @pl.when(pl.program_id(0) < PT)
    def _():
        _row_copies(pos_ref, xp_ref, ys_hbm, sem, scatter=False)

    @pl.when(pl.program_id(0) >= PT)
    def _():
        _row_copies(pos_ref, xs_ref, ys_hbm, sem, scatter=False)


_POS_SPEC = pl.BlockSpec((None, 1, TM), lambda i: (i, 0, 0), memory_space=pltpu.SMEM)
_ANY = pl.BlockSpec(memory_space=pl.ANY)


TM_D = 1024


def _dispatch(pos, xe, xs_prev):
    pos = pos.reshape(N_TOK // TM_D, 1, TM_D)
    return pl.pallas_call(
        _dispatch_kernel,
        grid=(N_TOK // TM_D,),
        in_specs=[pl.BlockSpec((None, 1, TM_D), lambda i: (i, 0, 0), memory_space=pltpu.SMEM),
                  pl.BlockSpec((TM_D, XE_W), lambda i: (i, 0)), _ANY],
        out_specs=_ANY,
        out_shape=jax.ShapeDtypeStruct((N_SORTED, XE_W), F32),
        scratch_shapes=[pltpu.SemaphoreType.DMA(())],
        input_output_aliases={2: 0},
        compiler_params=_params("arbitrary"),
        name="dispatch",
    )(pos, xe, xs_prev)


def _combine(pos, ys):
    return pl.pallas_call(
        _combine_kernel,
        grid=(N_TILES,),
        in_specs=[_POS_SPEC, _ANY],
        out_specs=list(_pair_specs(D_MODEL)),
        out_shape=[jax.ShapeDtypeStruct((NP, D_MODEL), F32), jax.ShapeDtypeStruct((NS, D_MODEL), F32)],
        scratch_shapes=[pltpu.SemaphoreType.DMA(())],
        compiler_params=_params("arbitrary"),
        name="combine",
    )(pos, ys)


def _experts_kernel(ea_ref, eb_ref, new_a_ref, new_b_ref, nused_ref, xs_ref,
                    wga_ref, wua_ref, wda_ref, wgb_ref, wub_ref, wdb_ref, g2_ref, b2_ref, ys_ref,
                    wga_s, wua_s, wda_s, wgb_s, wub_s, wdb_s, pre_ref):
    del ea_ref, eb_ref
    i = pl.program_id(0)
    nused = nused_ref[0]

    @pl.when(new_a_ref[i] == 1)
    def _():
        for src, dst in ((wga_ref, wga_s), (wua_ref, wua_s), (wda_ref, wda_s)):
            dst[...] = src[...].astype(BF16)

    @pl.when(new_b_ref[i] == 1)
    def _():
        for src, dst in ((wgb_ref, wgb_s), (wub_ref, wub_s), (wdb_ref, wdb_s)):
            dst[...] = src[...].astype(BF16)

    def mlp():
        x1 = xs_ref[:, 0:D_MODEL]
        xb = x1.astype(BF16)

        def expert(wg, wu, wd):
            a = _dot(xb, wg[...])
            h = a * _sigmoid(a) * _dot(xb, wu[...])
            return _dot(h.astype(BF16), wd[...])

        ffn = xs_ref[:, D_MODEL + 1:D_MODEL + 2] * expert(wga_s, wua_s, wda_s)
        ffn = ffn + xs_ref[:, D_MODEL + 2:D_MODEL + 3] * expert(wgb_s, wub_s, wdb_s)
        pre_ref[i % 2] = DN_ALPHA * x1 + ffn

    def norm():
        ys_ref[...] = _layer_norm(pre_ref[(i + 1) % 2], g2_ref[...], b2_ref[...])

    @pl.when(i == 0)
    def _():
        mlp()

    @pl.when((i > 0) & (i < nused))
    def _():
        norm()
        mlp()

    @pl.when(i == nused)
    def _():
        norm()

    @pl.when(i > nused)
    def _():
        ys_ref[...] = jnp.zeros_like(ys_ref)


def _experts(tile_a, tile_b, new_a, new_b, nused, xs, w_gate, w_up, w_down, g2, b2, layer):
    tile = lambda w: pl.BlockSpec((TM_E, w), lambda i, ea, eb, na, nb, nu: (jnp.minimum(i, nu[0] - 1), 0))
    wa = lambda *shape: pl.BlockSpec((None, None) + shape, lambda i, ea, eb, na, nb, nu: (layer, ea[i], 0, 0))
    wb = lambda *shape: pl.BlockSpec((None, None) + shape, lambda i, ea, eb, na, nb, nu: (layer, eb[i], 0, 0))
    lay = pl.BlockSpec((None, 1, D_MODEL), lambda i, ea, eb, na, nb, nu: (layer, 0, 0))
    up, down = pltpu.VMEM((D_MODEL, D_EXPERT), BF16), pltpu.VMEM((D_EXPERT, D_MODEL), BF16)
    return pl.pallas_call(
        _experts_kernel,
        grid_spec=pltpu.PrefetchScalarGridSpec(
            num_scalar_prefetch=5,
            grid=(NT_E + 1,),
            in_specs=[tile(XE_W),
                      wa(D_MODEL, D_EXPERT), wa(D_MODEL, D_EXPERT), wa(D_EXPERT, D_MODEL),
                      wb(D_MODEL, D_EXPERT), wb(D_MODEL, D_EXPERT), wb(D_EXPERT, D_MODEL),
                      lay, lay],
            out_specs=pl.BlockSpec((TM_E, D_MODEL), lambda i, ea, eb, na, nb, nu: (jnp.maximum(i - 1, 0), 0)),
            scratch_shapes=[up, up, down, up, up, down, pltpu.VMEM((2, TM_E, D_MODEL), F32)],
        ),
        out_shape=jax.ShapeDtypeStruct((N_SORTED, D_MODEL), F32),
        compiler_params=_params("arbitrary"),
        name="experts",
    )(tile_a, tile_b, new_a, new_b, nused, xs, w_gate, w_up, w_down, w_gate, w_up, w_down, g2, b2)


_PAIR_LO = tuple(a for a, _ in PAIRS)
_PAIR_HI = tuple(b for _, b in PAIRS)


def kernel(x_prompt, x_sample, cache_attn_k, cache_attn_v, state_gla, state_pool, w_in, w_gate, attn_sink,
           gla_w_a2, gla_b_a, gla_norm_g, pool_w, pool_scale, w_branch, w_o, ln1_g, ln1_b, ln2_g, ln2_b,
           w_router, router_bias, w_e_gate, w_e_up, w_e_down):
    ag0 = PA_W + PG_W
    w_in_p = (w_in[:, :, :ag0].astype(BF16), w_in[:, :, ag0 + GLA_RANK:].astype(BF16),
              jnp.pad(w_in[:, :, ag0:ag0 + GLA_RANK], ((0, 0), (0, 0), (0, LANES - GLA_RANK))).astype(BF16))
    w_gate_b, w_branch_b, w_o_b = w_gate.astype(BF16), w_branch.astype(BF16), w_o.astype(BF16)
    wa2_p = jnp.pad(gla_w_a2, ((0, 0), (0, LANES - GLA_RANK), (0, 0))).astype(BF16)
    pool_w_b = pool_w.astype(BF16)
    pool_scale_r = pool_scale.reshape(DEPTH, 1, POOL_W)
    wr_t = w_router.T.astype(BF16)
    rb = router_bias.reshape(N_EXPERTS, 1).astype(F32)
    cache_k = cache_attn_k.reshape(DEPTH, DEC_BATCH, WINDOW, KV_W)
    cache_v = cache_attn_v.reshape(DEPTH, DEC_BATCH, WINDOW, KV_W)
    pair_lo, pair_hi = jnp.array(_PAIR_LO, I32), jnp.array(_PAIR_HI, I32)

    x = (x_prompt.reshape(NP, D_MODEL), x_sample.reshape(NS, D_MODEL))
    xs = jnp.zeros((N_SORTED, XE_W), F32)
    ys = pos = None
    pk, pv, ps, pp, sk, sv, ss, sp = ([] for _ in range(8))
    for l in range(DEPTH):
        if l == 0:
            pa, pg, pu, pag, op_p, u_last = _in_proj(x, w_in_p, pool_w_b, pool_scale_r, l)
        else:
            pa, pg, pu, pag, op_p, u_last, x = _in_proj(ys, w_in_p, pool_w_b, pool_scale_r, l, pos)
        row2 = lambda a: a[l].reshape(1, -1)

        oa_p, k_p, v_p = _attn_prompt(pa, attn_sink[l])
        oa_s, nk, nv = _attn_sample(pa, attn_sink[l], cache_k, cache_v, l)
        og_p, s_p = _gla_prompt(pg, pag, wa2_p[l], row2(gla_b_a), row2(gla_norm_g))
        og_s, s_s = _gla_sample(pg, pag, wa2_p[l], row2(gla_b_a), row2(gla_norm_g), state_gla, l)
        u_s = pu[NP:].reshape(DEC_BATCH, DEC_SEQ, POOL_W)
        ext_s = jnp.concatenate(
            [jnp.zeros((DEC_BATCH, POOL_HIST - POOL_STATE, POOL_W), F32), state_pool[l], u_s], axis=1)
        op_s = _pool_sample(ext_s, pool_w_b[l], row2(pool_scale))

        xe, meta = _merge(x, (oa_p, oa_s, og_p, og_s, op_p, op_s), w_gate_b, w_branch_b, w_o_b,
                          ln1_g.reshape(DEPTH, 1, D_MODEL), ln1_b.reshape(DEPTH, 1, D_MODEL), wr_t, rb, l)
        cls = meta[:, 0, :].reshape(N_TOK // LANES, LANES).astype(I32)
        cls2d = jnp.pad(cls, ((0, PLAN_ROWS - N_TOK // LANES), (0, 0)), constant_values=-1)
        pos2d, tcls, nused = _plan(cls2d)
        pos = pos2d.reshape(-1)[:N_TOK].reshape(N_TILES, 1, TM)
        n_used = nused[0, :1]
        tile_cls = tcls[0, :NT_E + 1]
        tile_cls = jnp.where(jnp.arange(NT_E + 1) < n_used, tile_cls, tile_cls[n_used[0] - 1])
        tile_a = (tile_cls // len(PAIRS)) * EXPERTS_PER_GROUP + pair_lo[tile_cls % len(PAIRS)]
        tile_b = (tile_cls // len(PAIRS)) * EXPERTS_PER_GROUP + pair_hi[tile_cls % len(PAIRS)]
        first = jnp.ones((1,), I32)
        new_a = jnp.concatenate([first, (tile_a[1:] != tile_a[:-1]).astype(I32)])
        new_b = jnp.concatenate([first, (tile_b[1:] != tile_b[:-1]).astype(I32)])
        xs = _dispatch(pos, xe, xs)
        ys = _experts(tile_a, tile_b, new_a, new_b, n_used, xs, w_e_gate, w_e_up, w_e_down,
                      ln2_g.reshape(DEPTH, 1, D_MODEL), ln2_b.reshape(DEPTH, 1, D_MODEL), l)

        pk.append(k_p.reshape(BATCH, WINDOW, N_KV_HEADS, HEAD_DIM))
        pv.append(v_p.reshape(BATCH, WINDOW, N_KV_HEADS, HEAD_DIM))
        ps.append(s_p)
        pp.append(u_last[:, POOL_HIST - POOL_STATE:, :])
        sk.append(nk.reshape(DEC_BATCH, WINDOW, N_KV_HEADS, HEAD_DIM))
        sv.append(nv.reshape(DEC_BATCH, WINDOW, N_KV_HEADS, HEAD_DIM))
        ss.append(s_s)
        sp.append(ext_s[:, POOL_HIST + DEC_SEQ - POOL_STATE:, :])
    y_prompt, y_sample = _combine(pos, ys)
    return (y_prompt.reshape(BATCH, SEQ, D_MODEL), y_sample.reshape(DEC_BATCH, DEC_SEQ, D_MODEL),
            jnp.stack(pk), jnp.stack(pv), jnp.stack(ps), jnp.stack(pp),
            jnp.stack(sk), jnp.stack(sv), jnp.stack(ss), jnp.stack(sp))
```

```python
import jax
import jax.numpy as jnp
import numpy as np
from jax import lax
from jax.experimental import pallas as pl
from jax.experimental.pallas import tpu as pltpu

F32, BF16, I32 = jnp.float32, jnp.bfloat16, jnp.int32

D_MODEL = 1024
BATCH = 8
SEQ = 2048
DEPTH = 4
DEC_BATCH = 128
DEC_SEQ = 8
N_HEADS = 8
N_KV_HEADS = 2
HEAD_DIM = 64
WINDOW = 128
ATTN_W = N_HEADS * HEAD_DIM
KV_W = N_KV_HEADS * HEAD_DIM
Q_PER_KV = N_HEADS // N_KV_HEADS
GLA_HEADS = 4
GLA_DK = 64
GLA_DV = 128
GLA_KW = GLA_HEADS * GLA_DK
GLA_VW = GLA_HEADS * GLA_DV
GLA_RANK = 16
GLA_TAU = 16.0
GLA_CHUNK = 64
POOL_WINDOWS = (2, 4, 8, 16)
POOL_GROUPS = 4
POOL_GW = 128
POOL_W = POOL_GROUPS * POOL_GW
POOL_STATE = max(POOL_WINDOWS) - 1
N_BRANCH = 3
BRANCH_W = 512
N_EXPERTS = 16
N_GROUPS = 4
EXPERTS_PER_GROUP = N_EXPERTS // N_GROUPS
D_EXPERT = 512
DN_ALPHA = (2.0 * DEPTH) ** 0.25
LN_EPS = 1e-5
RMS_EPS = 1e-6
NEG_INF = -1e30
ALIBI_SLOPES = tuple(2.0 ** (-8.0 * h / N_HEADS) for h in range(1, N_HEADS + 1))

LANES = 128
SUBLANES = 8

NP = BATCH * SEQ
NS = DEC_BATCH * DEC_SEQ
N_TOK = NP + NS
TM = 512
N_TILES = N_TOK // TM
PT = NP // TM

PA_W = ATTN_W + 2 * KV_W
PG_W = 2 * GLA_KW + 2 * GLA_VW

PAIRS = ((0, 1), (0, 2), (0, 3), (1, 3), (2, 3), (2, 1))
N_CLASSES = N_GROUPS * len(PAIRS)
TM_E = 512
NT_E = -(-(N_TOK + N_CLASSES * (TM_E - 1)) // TM_E)
N_SORTED = NT_E * TM_E
XE_W = D_MODEL + LANES
PLAN_ROWS = 256

VMEM_LIMIT = 56 * 1024 * 1024
DMA_THREADS = 2


def _params(*sem):
    return pltpu.CompilerParams(dimension_semantics=sem, vmem_limit_bytes=VMEM_LIMIT)


def _dot(a, b):
    return jnp.dot(a, b, preferred_element_type=F32)


def _dot_nt(a, b):
    return lax.dot_general(a, b, (((1,), (1,)), ((), ())), preferred_element_type=F32)


def _dot_tn(a, b):
    return lax.dot_general(a, b, (((0,), (0,)), ((), ())), preferred_element_type=F32)


def _bdot_nt(a, b):
    return lax.dot_general(a, b, (((2,), (2,)), ((0,), (0,))), preferred_element_type=F32)


def _bdot(a, b):
    return lax.dot_general(a, b, (((2,), (1,)), ((0,), (0,))), preferred_element_type=F32)


def _layer_norm(h, g, b):
    mu = jnp.mean(h, axis=-1, keepdims=True)
    hc = h - mu
    var = jnp.mean(hc * hc, axis=-1, keepdims=True)
    return hc * lax.rsqrt(var + LN_EPS) * g + b


def _sigmoid(x):
    return 0.5 * jnp.tanh(0.5 * x) + 0.5


def _log_sigmoid(x):
    return jnp.minimum(x, 0.0) - jnp.log1p(jnp.exp(-jnp.abs(x)))


def _pair_specs(width):
    return (pl.BlockSpec((TM, width), lambda i, *_: (jnp.minimum(i, PT - 1), 0)),
            pl.BlockSpec((TM, width), lambda i, *_: (jnp.maximum(i - PT, 0), 0)))


def _pair_tile(p_ref, s_ref):
    return jnp.where(pl.program_id(0) < PT, p_ref[...], s_ref[...])


_IN_PROJ_WIDTHS = (PA_W, PG_W, POOL_W, LANES)


def _in_proj_tile(x, w_refs, pool_refs, out_refs, hist_ref):
    w_main, w_pool, w_rank = w_refs
    pw_ref, ps_ref = pool_refs
    pa_ref, pg_ref, pu_ref, pag_ref, op_ref, last_ref = out_refs
    xb = x.astype(BF16)
    pa_ref[...] = _dot(xb, w_main[:, 0:PA_W])
    pg_ref[...] = _dot(xb, w_main[:, PA_W:PA_W + PG_W])
    u = _dot(xb, w_pool[...])
    pu_ref[...] = u
    pag_ref[...] = _dot(xb, w_rank[...])

    i = pl.program_id(0)
    tiles_per_seq = SEQ // TM

    @pl.when(i < PT)
    def _():
        j = i % tiles_per_seq

        @pl.when(j == 0)
        def _():
            hist_ref[...] = jnp.zeros_like(hist_ref)

        ext = jnp.concatenate([hist_ref[...], u], axis=0)
        pos = j * TM + lax.broadcasted_iota(I32, (TM, 1), 0)
        cnt = [jnp.minimum(pos + 1, w).astype(F32) for w in POOL_WINDOWS]
        for g, y in enumerate(_pool_groups(ext, cnt, pw_ref, ps_ref, TM)):
            op_ref[:, g * POOL_GW:(g + 1) * POOL_GW] = y
        hist_ref[...] = u[TM - POOL_HIST:, :]

        @pl.when(j == tiles_per_seq - 1)
        def _():
            last_ref[...] = u[TM - POOL_HIST:, :]


def _in_proj_kernel(xp_ref, xs_ref, wm_ref, wp_ref, wr_ref, pw_ref, ps_ref,
                    pa_ref, pg_ref, pu_ref, pag_ref, op_ref, last_ref, hist_ref):
    _in_proj_tile(_pair_tile(xp_ref, xs_ref), (wm_ref, wp_ref, wr_ref), (pw_ref, ps_ref),
                  (pa_ref, pg_ref, pu_ref, pag_ref, op_ref, last_ref), hist_ref)


def _in_proj_gather_kernel(pos_ref, pos_next_ref, ys_hbm, wm_ref, wp_ref, wr_ref, pw_ref, ps_ref,
                           pa_ref, pg_ref, pu_ref, pag_ref, op_ref, last_ref, x_ref, buf, sem, hist_ref):
    i = pl.program_id(0)
    slot = i % 2

    def copy(p_ref, r, s):
        return pltpu.make_async_copy(ys_hbm.at[pl.ds(p_ref[0, r], 1)], buf.at[s, pl.ds(r, 1)], sem.at[s])

    def wait_tile(p_ref, s):
        def wait(r, carry):
            copy(p_ref, r, s).wait()
            return carry
        lax.fori_loop(0, TM, wait, 0, unroll=8)

    @pl.when(i == 0)
    def _():
        for r in range(TM):
            copy(pos_ref, r, 0).start(priority=r % DMA_THREADS)

    wait_tile(pos_ref, slot)
    x = buf[slot]
    for r in range(TM):
        copy(pos_next_ref, r, 1 - slot).start(priority=r % DMA_THREADS)
    x_ref[...] = x
    _in_proj_tile(x, (wm_ref, wp_ref, wr_ref), (pw_ref, ps_ref),
                  (pa_ref, pg_ref, pu_ref, pag_ref, op_ref, last_ref), hist_ref)

    @pl.when(i == pl.num_programs(0) - 1)
    def _():
        wait_tile(pos_next_ref, 1 - slot)


def _in_proj(x, w_in_parts, pool_w, pool_scale, layer, pos=None):
    row = lambda w: pl.BlockSpec((TM, w), lambda i: (i, 0))
    w_specs = [pl.BlockSpec((None,) + w.shape[1:], lambda i: (layer, 0, 0)) for w in w_in_parts]
    w_specs += [pl.BlockSpec((None, POOL_GROUPS, POOL_GW, POOL_GW), lambda i: (layer, 0, 0, 0)),
                pl.BlockSpec((None, 1, POOL_W), lambda i: (layer, 0, 0))]
    weights = (*w_in_parts, pool_w, pool_scale)
    out_specs = [row(w) for w in _IN_PROJ_WIDTHS]
    out_specs += [pl.BlockSpec((TM, POOL_W), lambda i: (jnp.minimum(i, PT - 1), 0)),
                  pl.BlockSpec((None, POOL_HIST, POOL_W), lambda i: (jnp.minimum(i // (SEQ // TM), BATCH - 1), 0, 0))]
    outs = [jax.ShapeDtypeStruct((N_TOK, w), F32) for w in _IN_PROJ_WIDTHS]
    outs += [jax.ShapeDtypeStruct((NP, POOL_W), BF16), jax.ShapeDtypeStruct((BATCH, POOL_HIST, POOL_W), F32)]
    hist = pltpu.VMEM((POOL_HIST, POOL_W), F32)
    if pos is None:
        return pl.pallas_call(
            _in_proj_kernel,
            grid=(N_TILES,),
            in_specs=[*_pair_specs(D_MODEL), *w_specs],
            out_specs=out_specs,
            out_shape=outs,
            scratch_shapes=[hist],
            compiler_params=_params("arbitrary"),
            name="in_proj",
        )(*x, *weights)
    pos_next = pl.BlockSpec((None, 1, TM), lambda i: (jnp.minimum(i + 1, N_TILES - 1), 0, 0), memory_space=pltpu.SMEM)
    return pl.pallas_call(
        _in_proj_gather_kernel,
        grid=(N_TILES,),
        in_specs=[_POS_SPEC, pos_next, _ANY, *w_specs],
        out_specs=out_specs + [row(D_MODEL)],
        out_shape=outs + [jax.ShapeDtypeStruct((N_TOK, D_MODEL), F32)],
        scratch_shapes=[pltpu.VMEM((2, TM, D_MODEL), F32), pltpu.SemaphoreType.DMA((2,)), hist],
        compiler_params=_params("arbitrary"),
        name="in_proj_gather",
    )(pos, pos, x, *weights)


def _softmax_sink_pv(parts, sink, pv):
    m = sink
    for s, _ in parts:
        m = jnp.maximum(m, jnp.max(s, axis=-1, keepdims=True))
    den = jnp.exp(sink - m)
    es = []
    for s, _ in parts:
        e = jnp.exp(s - m)
        den = den + jnp.sum(e, axis=-1, keepdims=True)
        es.append(e)
    inv = 1.0 / den
    out = None
    for e, (_, v) in zip(es, parts):
        o = pv((e * inv).astype(BF16), v)
        out = o if out is None else out + o
    return out


def _head_column(sink_ref, kv, rows_per_head):
    g = lax.broadcasted_iota(I32, (Q_PER_KV * rows_per_head, 1), 0) // rows_per_head
    col = jnp.zeros(g.shape, F32)
    for i in range(Q_PER_KV):
        col = jnp.where(g == i, sink_ref[kv * Q_PER_KV + i], col)
    return col


def _alibi_bias(dist, visible):
    out = np.empty((N_KV_HEADS, Q_PER_KV * dist.shape[0], dist.shape[1]), np.float32)
    for h in range(N_HEADS):
        kv, g = divmod(h, Q_PER_KV)
        out[kv, g * dist.shape[0]:(g + 1) * dist.shape[0]] = np.where(visible, -ALIBI_SLOPES[h] * dist, NEG_INF)
    return out


ATTN_QB = 16


def _attn_prompt_kernel(sink_ref, cur_ref, prev_ref, bias0_ref, bias_ref, o_ref, nk_ref, nv_ref):
    k = jnp.concatenate([prev_ref[:, 0:KV_W], cur_ref[:, ATTN_W:ATTN_W + KV_W]], axis=0).astype(BF16)
    vt = jnp.concatenate([prev_ref[:, KV_W:2 * KV_W], cur_ref[:, ATTN_W + KV_W:PA_W]], axis=0).T.astype(BF16)
    g_of_col = lax.broadcasted_iota(I32, (1, Q_PER_KV * WINDOW), 1) // WINDOW
    for kv in range(N_KV_HEADS):
        ks = slice(kv * HEAD_DIM, (kv + 1) * HEAD_DIM)
        heads = range(kv * Q_PER_KV, (kv + 1) * Q_PER_KV)
        sink = jnp.zeros(g_of_col.shape, F32)
        for g, h in enumerate(heads):
            sink = jnp.where(g_of_col == g, sink_ref[h], sink)
        for j in range(ATTN_QB):
            rows = slice(j * WINDOW, (j + 1) * WINDOW)
            keys = slice(j * WINDOW, (j + 2) * WINDOW)
            q = jnp.concatenate([cur_ref[rows, h * HEAD_DIM:(h + 1) * HEAD_DIM] for h in heads], axis=0)
            bias = bias0_ref[kv] if j == 0 else bias_ref[kv]
            st = _dot_nt(k[keys, ks], (q * (HEAD_DIM ** -0.5)).astype(BF16)) + bias
            m = jnp.maximum(sink, jnp.max(st, axis=0, keepdims=True))
            e = jnp.exp(st - m)
            inv = 1.0 / (jnp.exp(sink - m) + jnp.sum(e, axis=0, keepdims=True))
            ot = _dot(vt[ks, keys], (e * inv).astype(BF16))
            for pair in range(Q_PER_KV // 2):
                two = jnp.concatenate(
                    [ot[:, (2 * pair + g) * WINDOW:(2 * pair + g + 1) * WINDOW] for g in range(2)], axis=0)
                lo = (kv * Q_PER_KV + 2 * pair) * HEAD_DIM
                o_ref[rows, lo:lo + 2 * HEAD_DIM] = two.T.astype(BF16)

    @pl.when(pl.program_id(1) == pl.num_programs(1) - 1)
    def _():
        last = slice((ATTN_QB - 1) * WINDOW, ATTN_QB * WINDOW)
        nk_ref[...] = cur_ref[last, ATTN_W:ATTN_W + KV_W]
        nv_ref[...] = cur_ref[last, ATTN_W + KV_W:PA_W]


def _attn_prompt(pa, sink):
    nb = SEQ // WINDOW
    ns = nb // ATTN_QB
    r = np.arange(WINDOW)[:, None]
    c = np.arange(2 * WINDOW)[None, :]
    band = (c > r) & (c <= WINDOW + r)
    bias = np.stack([_alibi_bias(WINDOW + r - c, band & (c >= WINDOW)), _alibi_bias(WINDOW + r - c, band)])
    bias = jnp.asarray(bias.transpose(0, 1, 3, 2))
    bias_spec = lambda variant: pl.BlockSpec((None, N_KV_HEADS, 2 * WINDOW, Q_PER_KV * WINDOW),
                                             lambda b, i, s: (variant(i), 0, 0, 0))
    state = pl.BlockSpec((None, WINDOW, KV_W), lambda b, i, s: (b, 0, 0))
    return pl.pallas_call(
        _attn_prompt_kernel,
        grid_spec=pltpu.PrefetchScalarGridSpec(
            num_scalar_prefetch=1,
            grid=(BATCH, ns),
            in_specs=[
                pl.BlockSpec((ATTN_QB * WINDOW, PA_W), lambda b, i, s: (b * ns + i, 0)),
                pl.BlockSpec((WINDOW, 2 * KV_W),
                             lambda b, i, s: (b * nb + jnp.maximum(ATTN_QB * i - 1, 0), ATTN_W // (2 * KV_W))),
                bias_spec(lambda i: jnp.minimum(i, 1)), bias_spec(lambda i: 1),
            ],
            out_specs=[pl.BlockSpec((ATTN_QB * WINDOW, ATTN_W), lambda b, i, s: (b * ns + i, 0)), state, state],
        ),
        out_shape=[jax.ShapeDtypeStruct((NP, ATTN_W), BF16),
                   jax.ShapeDtypeStruct((BATCH, WINDOW, KV_W), F32),
                   jax.ShapeDtypeStruct((BATCH, WINDOW, KV_W), F32)],
        compiler_params=_params("arbitrary", "arbitrary"),
        name="attn_prompt",
    )(sink, pa, pa, bias, bias)


ATTN_SB = 16


def _attn_sample_kernel(sink_ref, cur_ref, kc_ref, vc_ref, bias_c_ref, bias_n_ref, o_ref, nk_ref, nv_ref):
    cur = cur_ref[...].reshape(ATTN_SB, DEC_SEQ, PA_W)
    for kv in range(N_KV_HEADS):
        ks = slice(kv * HEAD_DIM, (kv + 1) * HEAD_DIM)
        heads = range(kv * Q_PER_KV, (kv + 1) * Q_PER_KV)
        q = jnp.concatenate([cur[:, :, h * HEAD_DIM:(h + 1) * HEAD_DIM] for h in heads], axis=1)
        q = (q * (HEAD_DIM ** -0.5)).astype(BF16)
        kn = cur[:, :, ATTN_W + kv * HEAD_DIM:ATTN_W + (kv + 1) * HEAD_DIM].astype(BF16)
        vn = cur[:, :, ATTN_W + KV_W + kv * HEAD_DIM:ATTN_W + KV_W + (kv + 1) * HEAD_DIM].astype(BF16)
        sc = _bdot_nt(q, kc_ref[:, :, ks].astype(BF16)) + bias_c_ref[kv]
        sn = _bdot_nt(q, kn) + bias_n_ref[kv]
        sink = _head_column(sink_ref, kv, DEC_SEQ)
        o = _softmax_sink_pv([(sc, vc_ref[:, :, ks].astype(BF16)), (sn, vn)], sink, _bdot)
        for g, h in enumerate(heads):
            o_ref[:, h * HEAD_DIM:(h + 1) * HEAD_DIM] = (
                o[:, g * DEC_SEQ:(g + 1) * DEC_SEQ, :].reshape(ATTN_SB * DEC_SEQ, HEAD_DIM).astype(BF16))
    keep = WINDOW - DEC_SEQ
    nk_ref[:, 0:keep, :] = kc_ref[:, DEC_SEQ:WINDOW, :]
    nk_ref[:, keep:WINDOW, :] = cur[:, :, ATTN_W:ATTN_W + KV_W]
    nv_ref[:, 0:keep, :] = vc_ref[:, DEC_SEQ:WINDOW, :]
    nv_ref[:, keep:WINDOW, :] = cur[:, :, ATTN_W + KV_W:PA_W]


def _attn_sample(pa, sink, cache_k, cache_v, layer):
    rows = ATTN_SB * DEC_SEQ
    first = NP // rows
    t = np.arange(DEC_SEQ)[:, None]
    jc = np.arange(WINDOW)[None, :]
    jn = np.arange(DEC_SEQ)[None, :]
    bias_c = jnp.asarray(_alibi_bias(WINDOW + t - jc, jc > t))
    bias_n = jnp.asarray(_alibi_bias(t - jn, jn <= t))
    cache = pl.BlockSpec((None, ATTN_SB, WINDOW, KV_W), lambda j, s: (layer, j, 0, 0))
    new = pl.BlockSpec((ATTN_SB, WINDOW, KV_W), lambda j, s: (j, 0, 0))
    const = lambda a: pl.BlockSpec(a.shape, lambda j, s: (0,) * a.ndim)
    return pl.pallas_call(
        _attn_sample_kernel,
        grid_spec=pltpu.PrefetchScalarGridSpec(
            num_scalar_prefetch=1,
            grid=(DEC_BATCH // ATTN_SB,),
            in_specs=[pl.BlockSpec((rows, PA_W), lambda j, s: (first + j, 0)), cache, cache,
                      const(bias_c), const(bias_n)],
            out_specs=[pl.BlockSpec((rows, ATTN_W), lambda j, s: (j, 0)), new, new],
        ),
        out_shape=[jax.ShapeDtypeStruct((NS, ATTN_W), BF16),
                   jax.ShapeDtypeStruct((DEC_BATCH, WINDOW, KV_W), F32),
                   jax.ShapeDtypeStruct((DEC_BATCH, WINDOW, KV_W), F32)],
        compiler_params=_params("arbitrary"),
        name="attn_sample",
    )(sink, pa, cache_k, cache_v, bias_c, bias_n)


def _chunk_cumsum(x, chunk):
    pos = lax.broadcasted_iota(I32, x.shape, 0) % chunk
    sh = 1
    while sh < chunk:
        x = x + jnp.where(pos >= sh, pltpu.roll(x, sh, 0), 0.0)
        sh *= 2
    return x


def _gla_log_decay(pag_ref, wa2_ref, ba_ref):
    z = _dot(pag_ref[...].astype(BF16), wa2_ref[...]) + ba_ref[...]
    return _log_sigmoid(z) / GLA_TAU


def _gla_finish(o, rg, g):
    o = o * lax.rsqrt(jnp.mean(o * o, axis=-1, keepdims=True) + RMS_EPS)
    return (o * g * (rg * _sigmoid(rg))).astype(BF16)


GLA_TT = 2048
GLA_AG = 128


def _gla_prompt_kernel(pg_ref, pag_ref, wa2_ref, ba_ref, g_ref, o_ref, s_ref,
                       qd_ref, kd_ref, kdec_ref, a_ref, sall_ref, st_ref, acc_ref):
    nc = GLA_TT // GLA_CHUNK

    @pl.when(pl.program_id(1) == 0)
    def _():
        st_ref[...] = jnp.zeros_like(st_ref)

    cum = _chunk_cumsum(_gla_log_decay(pag_ref, wa2_ref, ba_ref), GLA_CHUNK)
    cum3 = cum.reshape(nc, GLA_CHUNK, GLA_KW)
    tot3 = cum3[:, GLA_CHUNK - 1:GLA_CHUNK, :]
    k = pg_ref[:, GLA_KW:2 * GLA_KW]
    qd_ref[...] = (pg_ref[:, 0:GLA_KW] * (GLA_DK ** -0.5) * jnp.exp(cum)).astype(BF16)
    kd_ref[...] = (k * jnp.exp(-cum)).astype(BF16)
    kdec_ref[...] = (k * jnp.exp(tot3 - cum3).reshape(GLA_TT, GLA_KW)).astype(BF16)
    etot = jnp.exp(tot3)

    def hs(h):
        return slice(h * GLA_DK, (h + 1) * GLA_DK)

    def vs(h):
        return slice(2 * GLA_KW + h * GLA_DV, 2 * GLA_KW + (h + 1) * GLA_DV)

    for c in range(nc):
        rows = slice(c * GLA_CHUNK, (c + 1) * GLA_CHUNK)
        for h in range(GLA_HEADS):
            a_ref[c, :, hs(h)] = _dot_tn(pg_ref[rows, vs(h)].astype(BF16), kdec_ref[rows, hs(h)])

    st = st_ref[...]
    for c in range(nc):
        sall_ref[c] = st.astype(BF16)
        st = etot[c] * st + a_ref[c]
    st_ref[...] = st

    r = lax.broadcasted_iota(I32, (GLA_AG, GLA_AG), 0)
    col = lax.broadcasted_iota(I32, (GLA_AG, GLA_AG), 1)
    causal = (r // GLA_CHUNK == col // GLA_CHUNK) & (col <= r)
    for h in range(GLA_HEADS):
        out = slice(h * GLA_DV, (h + 1) * GLA_DV)
        for c in range(nc):
            rows = slice(c * GLA_CHUNK, (c + 1) * GLA_CHUNK)
            acc_ref[rows, out] = _dot_nt(qd_ref[rows, hs(h)], sall_ref[c, :, hs(h)])
        for a in range(GLA_TT // GLA_AG):
            rows = slice(a * GLA_AG, (a + 1) * GLA_AG)
            att = jnp.where(causal, _dot_nt(qd_ref[rows, hs(h)], kd_ref[rows, hs(h)]), 0.0)
            acc_ref[rows, out] += _dot(att.astype(BF16), pg_ref[rows, vs(h)].astype(BF16))
        rg = pg_ref[:, 2 * GLA_KW + GLA_VW + h * GLA_DV:2 * GLA_KW + GLA_VW + (h + 1) * GLA_DV]
        o_ref[:, out] = _gla_finish(acc_ref[:, out], rg, g_ref[:, out])

    @pl.when(pl.program_id(1) == pl.num_programs(1) - 1)
    def _():
        for h in range(GLA_HEADS):
            s_ref[h] = st_ref[:, hs(h)].T


def _gla_prompt(pg, pag, wa2, ba, g):
    nt = SEQ // GLA_TT
    nc = GLA_TT // GLA_CHUNK
    const = lambda shape: pl.BlockSpec(shape, lambda b, j: (0,) * len(shape))
    return pl.pallas_call(
        _gla_prompt_kernel,
        grid=(BATCH, nt),
        in_specs=[
            pl.BlockSpec((GLA_TT, PG_W), lambda b, j: (b * nt + j, 0)),
            pl.BlockSpec((GLA_TT, LANES), lambda b, j: (b * nt + j, 0)),
            const((LANES, GLA_KW)), const((1, GLA_KW)), const((1, GLA_VW)),
        ],
        out_specs=[
            pl.BlockSpec((GLA_TT, GLA_VW), lambda b, j: (b * nt + j, 0)),
            pl.BlockSpec((None, GLA_HEADS, GLA_DK, GLA_DV), lambda b, j: (b, 0, 0, 0)),
        ],
        out_shape=[jax.ShapeDtypeStruct((NP, GLA_VW), BF16),
                   jax.ShapeDtypeStruct((BATCH, GLA_HEADS, GLA_DK, GLA_DV), F32)],
        scratch_shapes=[pltpu.VMEM((GLA_TT, GLA_KW), BF16), pltpu.VMEM((GLA_TT, GLA_KW), BF16),
                        pltpu.VMEM((GLA_TT, GLA_KW), BF16),
                        pltpu.VMEM((nc, GLA_DV, GLA_KW), F32), pltpu.VMEM((nc, GLA_DV, GLA_KW), BF16),
                        pltpu.VMEM((GLA_DV, GLA_KW), F32), pltpu.VMEM((GLA_TT, GLA_VW), F32)],
        compiler_params=_params("arbitrary", "arbitrary"),
        name="gla_prompt",
    )(pg, pag, wa2, ba, g)


GLA_SB = 8


def _gla_sample_kernel(pg_ref, pag_ref, wa2_ref, ba_ref, g_ref, s0_ref, o_ref, s_ref):
    cum_all = _chunk_cumsum(_gla_log_decay(pag_ref, wa2_ref, ba_ref), DEC_SEQ)
    tri = (lax.broadcasted_iota(I32, (DEC_SEQ, DEC_SEQ), 0) >= lax.broadcasted_iota(I32, (DEC_SEQ, DEC_SEQ), 1))
    for s in range(GLA_SB):
        rows = slice(s * DEC_SEQ, (s + 1) * DEC_SEQ)
        cum = cum_all[rows, :]
        tot = cum[DEC_SEQ - 1:DEC_SEQ, :]
        q = pg_ref[rows, 0:GLA_KW] * (GLA_DK ** -0.5)
        k = pg_ref[rows, GLA_KW:2 * GLA_KW]
        qd = (q * jnp.exp(cum)).astype(BF16)
        kd = (k * jnp.exp(-cum)).astype(BF16)
        kdec = (k * jnp.exp(tot - cum)).astype(BF16)
        etot = jnp.exp(tot)
        etot_col = [jnp.broadcast_to(etot[:, p * LANES:(p + 1) * LANES], (SUBLANES, LANES)).T[:, 0:1]
                    for p in range(GLA_KW // LANES)]
        for h in range(GLA_HEADS):
            ks = slice(h * GLA_DK, (h + 1) * GLA_DK)
            vs = slice(2 * GLA_KW + h * GLA_DV, 2 * GLA_KW + (h + 1) * GLA_DV)
            rs = slice(2 * GLA_KW + GLA_VW + h * GLA_DV, 2 * GLA_KW + GLA_VW + (h + 1) * GLA_DV)
            v = pg_ref[rows, vs].astype(BF16)
            st = s0_ref[s, h]
            att = jnp.where(tri, _dot_nt(qd[:, ks], kd[:, ks]), 0.0)
            o = _dot(qd[:, ks], st.astype(BF16)) + _dot(att.astype(BF16), v)
            per = LANES // GLA_DK
            col = etot_col[h // per][(h % per) * GLA_DK:(h % per + 1) * GLA_DK, :]
            s_ref[s, h] = col * st + _dot_tn(kdec[:, ks], v)
            o_ref[rows, h * GLA_DV:(h + 1) * GLA_DV] = _gla_finish(o, pg_ref[rows, rs], g_ref[:, h * GLA_DV:(h + 1) * GLA_DV])


def _gla_sample(pg, pag, wa2, ba, g, state, layer):
    rows = GLA_SB * DEC_SEQ
    first = NP // rows
    const = lambda shape: pl.BlockSpec(shape, lambda j: (0,) * len(shape))
    return pl.pallas_call(
        _gla_sample_kernel,
        grid=(DEC_BATCH // GLA_SB,),
        in_specs=[
            pl.BlockSpec((rows, PG_W), lambda j: (first + j, 0)),
            pl.BlockSpec((rows, LANES), lambda j: (first + j, 0)),
            const((LANES, GLA_KW)), const((1, GLA_KW)), const((1, GLA_VW)),
            pl.BlockSpec((None, GLA_SB, GLA_HEADS, GLA_DK, GLA_DV), lambda j: (layer, j, 0, 0, 0)),
        ],
        out_specs=[
            pl.BlockSpec((rows, GLA_VW), lambda j: (j, 0)),
            pl.BlockSpec((GLA_SB, GLA_HEADS, GLA_DK, GLA_DV), lambda j: (j, 0, 0, 0)),
        ],
        out_shape=[jax.ShapeDtypeStruct((NS, GLA_VW), BF16),
                   jax.ShapeDtypeStruct((DEC_BATCH, GLA_HEADS, GLA_DK, GLA_DV), F32)],
        compiler_params=_params("arbitrary"),
        name="gla_sample",
    )(pg, pag, wa2, ba, g, state)


POOL_HIST = 16


def _pool_groups(ext, cnt, pw_ref, ps_ref, out_rows):
    ax = ext.ndim - 2
    outs = []
    for g, w in enumerate(POOL_WINDOWS):
        x = ext[..., g * POOL_GW:(g + 1) * POOL_GW]
        s, sh = x, 1
        while sh < w:
            s = s + pltpu.roll(s, sh, ax)
            sh *= 2
        inv = 1.0 / cnt[g]
        if ext.ndim == 3:
            d = (s[:, POOL_HIST:, :] * inv - x[:, POOL_HIST:, :]).reshape(out_rows, POOL_GW)
        else:
            d = s[POOL_HIST:, :] * inv - x[POOL_HIST:, :]
        y = _dot(d.astype(BF16), pw_ref[g]) * ps_ref[:, g * POOL_GW:(g + 1) * POOL_GW]
        outs.append(y.astype(BF16))
    return outs


POOL_SB = 16


def _pool_sample_kernel(ext_ref, pw_ref, ps_ref, o_ref):
    cnt = [float(w) for w in POOL_WINDOWS]
    for g, y in enumerate(_pool_groups(ext_ref[...], cnt, pw_ref, ps_ref, POOL_SB * DEC_SEQ)):
        o_ref[:, g * POOL_GW:(g + 1) * POOL_GW] = y


def _pool_sample(ext, pw, ps):
    rows = POOL_SB * DEC_SEQ
    return pl.pallas_call(
        _pool_sample_kernel,
        grid=(DEC_BATCH // POOL_SB,),
        in_specs=[
            pl.BlockSpec((POOL_SB, POOL_HIST + DEC_SEQ, POOL_W), lambda j: (j, 0, 0)),
            pl.BlockSpec((POOL_GROUPS, POOL_GW, POOL_GW), lambda j: (0, 0, 0)),
            pl.BlockSpec((1, POOL_W), lambda j: (0, 0)),
        ],
        out_specs=pl.BlockSpec((rows, POOL_W), lambda j: (j, 0)),
        out_shape=jax.ShapeDtypeStruct((NS, POOL_W), BF16),
        compiler_params=_params("arbitrary"),
        name="pool_sample",
    )(ext, pw, ps)


def _route(sc, sel):
    gscore = []
    for g in range(N_GROUPS):
        v = sel[EXPERTS_PER_GROUP * g:EXPERTS_PER_GROUP * (g + 1)]
        best = None
        for a, b in PAIRS:
            pair = v[a] + v[b]
            best = pair if best is None else jnp.maximum(best, pair)
        gscore.append(best)
    gi = jnp.zeros_like(gscore[0], dtype=I32)
    best = gscore[0]
    for g in range(1, N_GROUPS):
        upd = gscore[g] > best
        gi = jnp.where(upd, g, gi)
        best = jnp.where(upd, gscore[g], best)

    def in_group(rows, j):
        out = rows[(N_GROUPS - 1) * EXPERTS_PER_GROUP + j]
        for g in range(N_GROUPS - 2, -1, -1):
            out = jnp.where(gi == g, rows[g * EXPERTS_PER_GROUP + j], out)
        return out

    u = [in_group(sel, j) for j in range(EXPERTS_PER_GROUP)]
    s_in = [in_group(sc, j) for j in range(EXPERTS_PER_GROUP)]

    def argmax4(vals):
        idx = jnp.zeros_like(gi)
        m = vals[0]
        for j in range(1, EXPERTS_PER_GROUP):
            upd = vals[j] > m
            idx = jnp.where(upd, j, idx)
            m = jnp.where(upd, vals[j], m)
        return idx

    def pick(vals, idx):
        out = vals[EXPERTS_PER_GROUP - 1]
        for j in range(EXPERTS_PER_GROUP - 2, -1, -1):
            out = jnp.where(idx == j, vals[j], out)
        return out

    i1 = argmax4(u)
    i2 = argmax4([jnp.where(i1 == j, NEG_INF, u[j]) for j in range(EXPERTS_PER_GROUP)])
    w1, w2 = pick(s_in, i1), pick(s_in, i2)
    tot = w1 + w2
    w1, w2 = w1 / tot, w2 / tot
    lo, hi = jnp.minimum(i1, i2), jnp.maximum(i1, i2)
    first_lo = i1 < i2
    w_lo, w_hi = jnp.where(first_lo, w1, w2), jnp.where(first_lo, w2, w1)
    pair = jnp.where(lo == 0, hi - 1, jnp.where(hi == EXPERTS_PER_GROUP - 1, lo + 2, len(PAIRS) - 1))
    swapped = pair == len(PAIRS) - 1
    return gi * len(PAIRS) + pair, jnp.where(swapped, w_hi, w_lo), jnp.where(swapped, w_lo, w_hi)


def _merge_kernel(*refs):
    (oap_ref, oas_ref, ogp_ref, ogs_ref, opp_ref, ops_ref, wg_ref, wb_ref, wo_ref,
     g1_ref, b1_ref, wr_ref, rb_ref, xe_ref, meta_ref) = refs[-15:]
    x = refs[0][...] if len(refs) == 16 else _pair_tile(refs[0], refs[1])
    xb = x.astype(BF16)
    merged = None
    for n, (brp, brs) in enumerate(((oap_ref, oas_ref), (ogp_ref, ogs_ref), (opp_ref, ops_ref))):
        gate = _sigmoid(_dot(xb, wg_ref[:, n * D_MODEL:(n + 1) * D_MODEL]))
        term = gate * _dot(_pair_tile(brp, brs), wb_ref[n])
        merged = term if merged is None else merged + term
    mix = _dot(merged.astype(BF16), wo_ref[...])
    x1 = _layer_norm(DN_ALPHA * x + mix, g1_ref[...], b1_ref[...])
    xe_ref[:, 0:D_MODEL] = x1
    sc_t = _sigmoid(_dot_nt(wr_ref[...], x1.astype(BF16)))
    sel_t = sc_t + rb_ref[...]
    sc = [sc_t[e:e + 1, :] for e in range(N_EXPERTS)]
    sel = [sel_t[e:e + 1, :] for e in range(N_EXPERTS)]
    cls, w_first, w_second = _route(sc, sel)
    cls = cls.astype(F32)

    def rows(n):
        rid = lax.broadcasted_iota(I32, (n, TM), 0)
        return jnp.where(rid == 0, cls, jnp.where(rid == 1, w_first, jnp.where(rid == 2, w_second, 0.0)))

    meta_ref[...] = rows(SUBLANES)
    xe_ref[:, D_MODEL:XE_W] = rows(LANES).T


def _merge(x, branches, wg, wb, wo, g1, b1, wr_t, rb, layer):
    row = lambda w: pl.BlockSpec((TM, w), lambda i: (i, 0))
    lay = lambda *shape: pl.BlockSpec((None,) + shape, lambda i: (layer,) + (0,) * len(shape))
    const = lambda *shape: pl.BlockSpec(shape, lambda i: (0,) * len(shape))
    return pl.pallas_call(
        _merge_kernel,
        grid=(N_TILES,),
        in_specs=[*(_pair_specs(D_MODEL) if isinstance(x, tuple) else (row(D_MODEL),)),
                  *(_pair_specs(BRANCH_W) * N_BRANCH),
                  lay(D_MODEL, N_BRANCH * D_MODEL), lay(N_BRANCH, BRANCH_W, D_MODEL), lay(D_MODEL, D_MODEL),
                  lay(1, D_MODEL), lay(1, D_MODEL), const(N_EXPERTS, D_MODEL), const(N_EXPERTS, 1)],
        out_specs=[row(XE_W), pl.BlockSpec((None, SUBLANES, TM), lambda i: (i, 0, 0))],
        out_shape=[jax.ShapeDtypeStruct((N_TOK, XE_W), F32), jax.ShapeDtypeStruct((N_TILES, SUBLANES, TM), F32)],
        compiler_params=_params("arbitrary"),
        name="merge",
    )(*(x if isinstance(x, tuple) else (x,)), *branches, wg, wb, wo, g1, b1, wr_t, rb)


def _plan_kernel(cls_ref, pos_ref, tcls_ref, nused_ref):
    cls = cls_ref[...]
    lane_r = lax.broadcasted_iota(I32, (LANES, 2 * LANES), 0)
    lane_c = lax.broadcasted_iota(I32, (LANES, 2 * LANES), 1)
    lane_mat = ((lane_c >= LANES) | (lane_r < lane_c)).astype(BF16)
    row_r = lax.broadcasted_iota(I32, (2 * PLAN_ROWS, PLAN_ROWS), 0)
    row_c = lax.broadcasted_iota(I32, (2 * PLAN_ROWS, PLAN_ROWS), 1)
    row_mat = ((row_r >= PLAN_ROWS) | (row_c < row_r)).astype(BF16)
    tile_start = (lax.broadcasted_iota(I32, (SUBLANES, LANES), 1) * TM_E).astype(F32)
    pos = jnp.zeros((PLAN_ROWS, LANES), F32)
    off = jnp.zeros((PLAN_ROWS, LANES), F32)
    tcls = jnp.zeros((SUBLANES, LANES), I32)
    for c in range(N_CLASSES):
        m = cls == c
        lanes = _dot(m.astype(BF16), lane_mat)
        rows = _dot(row_mat, lanes[:, LANES:].astype(BF16))
        rank = lanes[:, 0:LANES] + rows[0:PLAN_ROWS]
        count = rows[PLAN_ROWS:]
        pos = jnp.where(m, off + rank, pos)
        off = off + jnp.ceil(count * (1.0 / TM_E)) * TM_E
        tcls = tcls + (off[0:SUBLANES] <= tile_start).astype(I32)
    pos_ref[...] = pos.astype(I32)
    tcls_ref[...] = tcls
    nused_ref[...] = (off[0:SUBLANES] * (1.0 / TM_E)).astype(I32)


def _plan(cls2d):
    return pl.pallas_call(
        _plan_kernel,
        out_shape=[jax.ShapeDtypeStruct((PLAN_ROWS, LANES), I32),
                   jax.ShapeDtypeStruct((SUBLANES, LANES), I32),
                   jax.ShapeDtypeStruct((SUBLANES, LANES), I32)],
        compiler_params=pltpu.CompilerParams(vmem_limit_bytes=VMEM_LIMIT),
        name="plan",
    )(cls2d)


def _row_copies(pos_ref, tile_ref, sorted_hbm, sem, scatter):
    def copy(r):
        row, srt = tile_ref.at[pl.ds(r, 1)], sorted_hbm.at[pl.ds(pos_ref[0, r], 1)]
        return pltpu.make_async_copy(row, srt, sem) if scatter else pltpu.make_async_copy(srt, row, sem)

    def wait(r, carry):
        copy(r).wait()
        return carry

    rows = tile_ref.shape[0]
    for r in range(rows):
        copy(r).start(priority=r % DMA_THREADS)
    lax.fori_loop(0, rows, wait, 0, unroll=8)


def _dispatch_kernel(pos_ref, x_ref, xs_in_hbm, xs_hbm, sem):
    del xs_in_hbm
    _row_copies(pos_ref, x_ref, xs_hbm, sem, scatter=True)


def _combine_kernel(pos_ref, ys_hbm, xp_ref, xs_ref, sem):
    @pl.when(pl.program_id(0) < PT)
    def _():
        _row_copies(pos_ref, xp_ref, ys_hbm, sem, scatter=False)

    @pl.when(pl.program_id(0) >= PT)
    def _():
        _row_copies(pos_ref, xs_ref, ys_hbm, sem, scatter=False)


_POS_SPEC = pl.BlockSpec((None, 1, TM), lambda i: (i, 0, 0), memory_space=pltpu.SMEM)
_ANY = pl.BlockSpec(memory_space=pl.ANY)


TM_D = 1024


def _dispatch(pos, xe, xs_prev):
    pos = pos.reshape(N_TOK // TM_D, 1, TM_D)
    return pl.pallas_call(
        _dispatch_kernel,
        grid=(N_TOK // TM_D,),
        in_specs=[pl.BlockSpec((None, 1, TM_D), lambda i: (i, 0, 0), memory_space=pltpu.SMEM),
                  pl.BlockSpec((TM_D, XE_W), lambda i: (i, 0)), _ANY],
        out_specs=_ANY,
        out_shape=jax.ShapeDtypeStruct((N_SORTED, XE_W), F32),
        scratch_shapes=[pltpu.SemaphoreType.DMA(())],
        input_output_aliases={2: 0},
        compiler_params=_params("arbitrary"),
        name="dispatch",
    )(pos, xe, xs_prev)


def _combine(pos, ys):
    return pl.pallas_call(
        _combine_kernel,
        grid=(N_TILES,),
        in_specs=[_POS_SPEC, _ANY],
        out_specs=list(_pair_specs(D_MODEL)),
        out_shape=[jax.ShapeDtypeStruct((NP, D_MODEL), F32), jax.ShapeDtypeStruct((NS, D_MODEL), F32)],
        scratch_shapes=[pltpu.SemaphoreType.DMA(())],
        compiler_params=_params("arbitrary"),
        name="combine",
    )(pos, ys)


def _experts_kernel(ea_ref, eb_ref, new_a_ref, new_b_ref, nused_ref, xs_ref,
                    wga_ref, wua_ref, wda_ref, wgb_ref, wub_ref, wdb_ref, g2_ref, b2_ref, ys_ref,
                    wga_s, wua_s, wda_s, wgb_s, wub_s, wdb_s, pre_ref):
    del ea_ref, eb_ref
    i = pl.program_id(0)
    nused = nused_ref[0]

    @pl.when(new_a_ref[i] == 1)
    def _():
        for src, dst in ((wga_ref, wga_s), (wua_ref, wua_s), (wda_ref, wda_s)):
            dst[...] = src[...].astype(BF16)

    @pl.when(new_b_ref[i] == 1)
    def _():
        for src, dst in ((wgb_ref, wgb_s), (wub_ref, wub_s), (wdb_ref, wdb_s)):
            dst[...] = src[...].astype(BF16)

    def mlp():
        x1 = xs_ref[:, 0:D_MODEL]
        xb = x1.astype(BF16)

        def expert(wg, wu, wd):
            a = _dot(xb, wg[...])
            h = a * _sigmoid(a) * _dot(xb, wu[...])
            return _dot(h.astype(BF16), wd[...])

        ffn = xs_ref[:, D_MODEL + 1:D_MODEL + 2] * expert(wga_s, wua_s, wda_s)
        ffn = ffn + xs_ref[:, D_MODEL + 2:D_MODEL + 3] * expert(wgb_s, wub_s, wdb_s)
        pre_ref[i % 2] = DN_ALPHA * x1 + ffn

    def norm():
        ys_ref[...] = _layer_norm(pre_ref[(i + 1) % 2], g2_ref[...], b2_ref[...])

    @pl.when(i == 0)
    def _():
        mlp()

    @pl.when((i > 0) & (i < nused))
    def _():
        norm()
        mlp()

    @pl.when(i == nused)
    def _():
        norm()

    @pl.when(i > nused)
    def _():
        ys_ref[...] = jnp.zeros_like(ys_ref)


def _experts(tile_a, tile_b, new_a, new_b, nused, xs, w_gate, w_up, w_down, g2, b2, layer):
    tile = lambda w: pl.BlockSpec((TM_E, w), lambda i, ea, eb, na, nb, nu: (jnp.minimum(i, nu[0] - 1), 0))
    wa = lambda *shape: pl.BlockSpec((None, None) + shape, lambda i, ea, eb, na, nb, nu: (layer, ea[i], 0, 0))
    wb = lambda *shape: pl.BlockSpec((None, None) + shape, lambda i, ea, eb, na, nb, nu: (layer, eb[i], 0, 0))
    lay = pl.BlockSpec((None, 1, D_MODEL), lambda i, ea, eb, na, nb, nu: (layer, 0, 0))
    up, down = pltpu.VMEM((D_MODEL, D_EXPERT), BF16), pltpu.VMEM((D_EXPERT, D_MODEL), BF16)
    return pl.pallas_call(
        _experts_kernel,
        grid_spec=pltpu.PrefetchScalarGridSpec(
            num_scalar_prefetch=5,
            grid=(NT_E + 1,),
            in_specs=[tile(XE_W),
                      wa(D_MODEL, D_EXPERT), wa(D_MODEL, D_EXPERT), wa(D_EXPERT, D_MODEL),
                      wb(D_MODEL, D_EXPERT), wb(D_MODEL, D_EXPERT), wb(D_EXPERT, D_MODEL),
                      lay, lay],
            out_specs=pl.BlockSpec((TM_E, D_MODEL), lambda i, ea, eb, na, nb, nu: (jnp.maximum(i - 1, 0), 0)),
            scratch_shapes=[up, up, down, up, up, down, pltpu.VMEM((2, TM_E, D_MODEL), F32)],
        ),
        out_shape=jax.ShapeDtypeStruct((N_SORTED, D_MODEL), F32),
        compiler_params=_params("arbitrary"),
        name="experts",
    )(tile_a, tile_b, new_a, new_b, nused, xs, w_gate, w_up, w_down, w_gate, w_up, w_down, g2, b2)


_PAIR_LO = tuple(a for a, _ in PAIRS)
_PAIR_HI = tuple(b for _, b in PAIRS)


def kernel(x_prompt, x_sample, cache_attn_k, cache_attn_v, state_gla, state_pool, w_in, w_gate, attn_sink,
           gla_w_a2, gla_b_a, gla_norm_g, pool_w, pool_scale, w_branch, w_o, ln1_g, ln1_b, ln2_g, ln2_b,
           w_router, router_bias, w_e_gate, w_e_up, w_e_down):
    ag0 = PA_W + PG_W
    w_in_p = (w_in[:, :, :ag0].astype(BF16), w_in[:, :, ag0 + GLA_RANK:].astype(BF16),
              jnp.pad(w_in[:, :, ag0:ag0 + GLA_RANK], ((0, 0), (0, 0), (0, LANES - GLA_RANK))).astype(BF16))
    w_gate_b, w_branch_b, w_o_b = w_gate.astype(BF16), w_branch.astype(BF16), w_o.astype(BF16)
    wa2_p = jnp.pad(gla_w_a2, ((0, 0), (0, LANES - GLA_RANK), (0, 0))).astype(BF16)
    pool_w_b = pool_w.astype(BF16)
    pool_scale_r = pool_scale.reshape(DEPTH, 1, POOL_W)
    wr_t = w_router.T.astype(BF16)
    rb = router_bias.reshape(N_EXPERTS, 1).astype(F32)
    cache_k = cache_attn_k.reshape(DEPTH, DEC_BATCH, WINDOW, KV_W)
    cache_v = cache_attn_v.reshape(DEPTH, DEC_BATCH, WINDOW, KV_W)
    pair_lo, pair_hi = jnp.array(_PAIR_LO, I32), jnp.array(_PAIR_HI, I32)

    x = (x_prompt.reshape(NP, D_MODEL), x_sample.reshape(NS, D_MODEL))
    xs = jnp.zeros((N_SORTED, XE_W), F32)
    ys = pos = None
    pk, pv, ps, pp, sk, sv, ss, sp = ([] for _ in range(8))
    for l in range(DEPTH):
        if l == 0:
            pa, pg, pu, pag, op_p, u_last = _in_proj(x, w_in_p, pool_w_b, pool_scale_r, l)
        else:
            pa, pg, pu, pag, op_p, u_last, x = _in_proj(ys, w_in_p, pool_w_b, pool_scale_r, l, pos)
        row2 = lambda a: a[l].reshape(1, -1)

        oa_p, k_p, v_p = _attn_prompt(pa, attn_sink[l])
        oa_s, nk, nv = _attn_sample(pa, attn_sink[l], cache_k, cache_v, l)
        og_p, s_p = _gla_prompt(pg, pag, wa2_p[l], row2(gla_b_a), row2(gla_norm_g))
        og_s, s_s = _gla_sample(pg, pag, wa2_p[l], row2(gla_b_a), row2(gla_norm_g), state_gla, l)
        u_s = pu[NP:].reshape(DEC_BATCH, DEC_SEQ, POOL_W)
        ext_s = jnp.concatenate(
            [jnp.zeros((DEC_BATCH, POOL_HIST - POOL_STATE, POOL_W), F32), state_pool[l], u_s], axis=1)
        op_s = _pool_sample(ext_s, pool_w_b[l], row2(pool_scale))

        xe, meta = _merge(x, (oa_p, oa_s, og_p, og_s, op_p, op_s), w_gate_b, w_branch_b, w_o_b,
                          ln1_g.reshape(DEPTH, 1, D_MODEL), ln1_b.reshape(DEPTH, 1, D_MODEL), wr_t, rb, l)
        cls = meta[:, 0, :].reshape(N_TOK // LANES, LANES).astype(I32)
        cls2d = jnp.pad(cls, ((0, PLAN_ROWS - N_TOK // LANES), (0, 0)), constant_values=-1)
        pos2d, tcls, nused = _plan(cls2d)
        pos = pos2d.reshape(-1)[:N_TOK].reshape(N_TILES, 1, TM)
        n_used = nused[0, :1]
        tile_cls = tcls[0, :NT_E + 1]
        tile_cls = jnp.where(jnp.arange(NT_E + 1) < n_used, tile_cls, tile_cls[n_used[0] - 1])
        tile_a = (tile_cls // len(PAIRS)) * EXPERTS_PER_GROUP + pair_lo[tile_cls % len(PAIRS)]
        tile_b = (tile_cls // len(PAIRS)) * EXPERTS_PER_GROUP + pair_hi[tile_cls % len(PAIRS)]
        first = jnp.ones((1,), I32)
        new_a = jnp.concatenate([first, (tile_a[1:] != tile_a[:-1]).astype(I32)])
        new_b = jnp.concatenate([first, (tile_b[1:] != tile_b[:-1]).astype(I32)])
        xs = _dispatch(pos, xe, xs)
        ys = _experts(tile_a, tile_b, new_a, new_b, n_used, xs, w_e_gate, w_e_up, w_e_down,
                      ln2_g.reshape(DEPTH, 1, D_MODEL), ln2_b.reshape(DEPTH, 1, D_MODEL), l)

        pk.append(k_p.reshape(BATCH, WINDOW, N_KV_HEADS, HEAD_DIM))
        pv.append(v_p.reshape(BATCH, WINDOW, N_KV_HEADS, HEAD_DIM))
        ps.append(s_p)
        pp.append(u_last[:, POOL_HIST - POOL_STATE:, :])
        sk.append(nk.reshape(DEC_BATCH, WINDOW, N_KV_HEADS, HEAD_DIM))
        sv.append(nv.reshape(DEC_BATCH, WINDOW, N_KV_HEADS, HEAD_DIM))
        ss.append(s_s)
        sp.append(ext_s[:, POOL_HIST + DEC_SEQ - POOL_STATE:, :])
    y_prompt, y_sample = _combine(pos, ys)
    return (y_prompt.reshape(BATCH, SEQ, D_MODEL), y_sample.reshape(DEC_BATCH, DEC_SEQ, D_MODEL),
            jnp.stack(pk), jnp.stack(pv), jnp.stack(ps), jnp.stack(pp),
            jnp.stack(sk), jnp.stack(sv), jnp.stack(ss), jnp.stack(sp))
```

```python
import jax
import jax.numpy as jnp
import numpy as np
from jax import lax
from jax.experimental import pallas as pl
from jax.experimental.pallas import tpu as pltpu

F32, BF16, I32 = jnp.float32, jnp.bfloat16, jnp.int32

D_MODEL = 1024
BATCH = 8
SEQ = 2048
DEPTH = 4
DEC_BATCH = 128
DEC_SEQ = 8
N_HEADS = 8
N_KV_HEADS = 2
HEAD_DIM = 64
WINDOW = 128
ATTN_W = N_HEADS * HEAD_DIM
KV_W = N_KV_HEADS * HEAD_DIM
Q_PER_KV = N_HEADS // N_KV_HEADS
GLA_HEADS = 4
GLA_DK = 64
GLA_DV = 128
GLA_KW = GLA_HEADS * GLA_DK
GLA_VW = GLA_HEADS * GLA_DV
GLA_RANK = 16
GLA_TAU = 16.0
GLA_CHUNK = 64
POOL_WINDOWS = (2, 4, 8, 16)
POOL_GROUPS = 4
POOL_GW = 128
POOL_W = POOL_GROUPS * POOL_GW
POOL_STATE = max(POOL_WINDOWS) - 1
N_BRANCH = 3
BRANCH_W = 512
N_EXPERTS = 16
N_GROUPS = 4
EXPERTS_PER_GROUP = N_EXPERTS // N_GROUPS
D_EXPERT = 512
DN_ALPHA = (2.0 * DEPTH) ** 0.25
LN_EPS = 1e-5
RMS_EPS = 1e-6
NEG_INF = -1e30
ALIBI_SLOPES = tuple(2.0 ** (-8.0 * h / N_HEADS) for h in range(1, N_HEADS + 1))

LANES = 128
SUBLANES = 8

NP = BATCH * SEQ
NS = DEC_BATCH * DEC_SEQ
N_TOK = NP + NS
TM = 512
N_TILES = N_TOK // TM
PT = NP // TM

PA_W = ATTN_W + 2 * KV_W
PG_W = 2 * GLA_KW + 2 * GLA_VW

PAIRS = ((0, 1), (0, 2), (0, 3), (1, 3), (2, 3), (2, 1))
N_CLASSES = N_GROUPS * len(PAIRS)
TM_E = 512
NT_E = -(-(N_TOK + N_CLASSES * (TM_E - 1)) // TM_E)
N_SORTED = NT_E * TM_E
XE_W = D_MODEL + LANES
PLAN_ROWS = 256

VMEM_LIMIT = 56 * 1024 * 1024
DMA_THREADS = 2


def _params(*sem):
    return pltpu.CompilerParams(dimension_semantics=sem, vmem_limit_bytes=VMEM_LIMIT)


def _dot(a, b):
    return jnp.dot(a, b, preferred_element_type=F32)


def _dot_nt(a, b):
    return lax.dot_general(a, b, (((1,), (1,)), ((), ())), preferred_element_type=F32)


def _dot_tn(a, b):
    return lax.dot_general(a, b, (((0,), (0,)), ((), ())), preferred_element_type=F32)


def _bdot_nt(a, b):
    return lax.dot_general(a, b, (((2,), (2,)), ((0,), (0,))), preferred_element_type=F32)


def _bdot(a, b):
    return lax.dot_general(a, b, (((2,), (1,)), ((0,), (0,))), preferred_element_type=F32)


def _layer_norm(h, g, b):
    mu = jnp.mean(h, axis=-1, keepdims=True)
    hc = h - mu
    var = jnp.mean(hc * hc, axis=-1, keepdims=True)
    return hc * lax.rsqrt(var + LN_EPS) * g + b


def _sigmoid(x):
    return 0.5 * jnp.tanh(0.5 * x) + 0.5


def _log_sigmoid(x):
    return jnp.minimum(x, 0.0) - jnp.log1p(jnp.exp(-jnp.abs(x)))


def _pair_specs(width):
    return (pl.BlockSpec((TM, width), lambda i, *_: (jnp.minimum(i, PT - 1), 0)),
            pl.BlockSpec((TM, width), lambda i, *_: (jnp.maximum(i - PT, 0), 0)))


def _pair_tile(p_ref, s_ref):
    return jnp.where(pl.program_id(0) < PT, p_ref[...], s_ref[...])


_IN_PROJ_WIDTHS = (PA_W, PG_W, POOL_W, LANES)


def _in_proj_tile(x, w_refs, pool_refs, out_refs, hist_ref):
    w_main, w_pool, w_rank = w_refs
    pw_ref, ps_ref = pool_refs
    pa_ref, pg_ref, pu_ref, pag_ref, op_ref, last_ref = out_refs
    xb = x.astype(BF16)
    pa_ref[...] = _dot(xb, w_main[:, 0:PA_W])
    pg_ref[...] = _dot(xb, w_main[:, PA_W:PA_W + PG_W])
    u = _dot(xb, w_pool[...])
    pu_ref[...] = u
    pag_ref[...] = _dot(xb, w_rank[...])

    i = pl.program_id(0)
    tiles_per_seq = SEQ // TM

    @pl.when(i < PT)
    def _():
        j = i % tiles_per_seq

        @pl.when(j == 0)
        def _():
            hist_ref[...] = jnp.zeros_like(hist_ref)

        ext = jnp.concatenate([hist_ref[...], u], axis=0)
        pos = j * TM + lax.broadcasted_iota(I32, (TM, 1), 0)
        cnt = [jnp.minimum(pos + 1, w).astype(F32) for w in POOL_WINDOWS]
        for g, y in enumerate(_pool_groups(ext, cnt, pw_ref, ps_ref, TM)):
            op_ref[:, g * POOL_GW:(g + 1) * POOL_GW] = y
        hist_ref[...] = u[TM - POOL_HIST:, :]

        @pl.when(j == tiles_per_seq - 1)
        def _():
            last_ref[...] = u[TM - POOL_HIST:, :]


def _in_proj_kernel(xp_ref, xs_ref, wm_ref, wp_ref, wr_ref, pw_ref, ps_ref,
                    pa_ref, pg_ref, pu_ref, pag_ref, op_ref, last_ref, hist_ref):
    _in_proj_tile(_pair_tile(xp_ref, xs_ref), (wm_ref, wp_ref, wr_ref), (pw_ref, ps_ref),
                  (pa_ref, pg_ref, pu_ref, pag_ref, op_ref, last_ref), hist_ref)


def _in_proj_gather_kernel(pos_ref, pos_next_ref, ys_hbm, wm_ref, wp_ref, wr_ref, pw_ref, ps_ref,
                           pa_ref, pg_ref, pu_ref, pag_ref, op_ref, last_ref, x_ref, buf, sem, hist_ref):
    i = pl.program_id(0)
    slot = i % 2

    def copy(p_ref, r, s):
        return pltpu.make_async_copy(ys_hbm.at[pl.ds(p_ref[0, r], 1)], buf.at[s, pl.ds(r, 1)], sem.at[s])

    def wait_tile(p_ref, s):
        def wait(r, carry):
            copy(p_ref, r, s).wait()
            return carry
        lax.fori_loop(0, TM, wait, 0, unroll=8)

    @pl.when(i == 0)
    def _():
        for r in range(TM):
            copy(pos_ref, r, 0).start(priority=r % DMA_THREADS)

    wait_tile(pos_ref, slot)
    x = buf[slot]
    for r in range(TM):
        copy(pos_next_ref, r, 1 - slot).start(priority=r % DMA_THREADS)
    x_ref[...] = x
    _in_proj_tile(x, (wm_ref, wp_ref, wr_ref), (pw_ref, ps_ref),
                  (pa_ref, pg_ref, pu_ref, pag_ref, op_ref, last_ref), hist_ref)

    @pl.when(i == pl.num_programs(0) - 1)
    def _():
        wait_tile(pos_next_ref, 1 - slot)


def _in_proj(x, w_in_parts, pool_w, pool_scale, layer, pos=None):
    row = lambda w: pl.BlockSpec((TM, w), lambda i: (i, 0))
    w_specs = [pl.BlockSpec((None,) + w.shape[1:], lambda i: (layer, 0, 0)) for w in w_in_parts]
    w_specs += [pl.BlockSpec((None, POOL_GROUPS, POOL_GW, POOL_GW), lambda i: (layer, 0, 0, 0)),
                pl.BlockSpec((None, 1, POOL_W), lambda i: (layer, 0, 0))]
    weights = (*w_in_parts, pool_w, pool_scale)
    out_specs = [row(w) for w in _IN_PROJ_WIDTHS]
    out_specs += [pl.BlockSpec((TM, POOL_W), lambda i: (jnp.minimum(i, PT - 1), 0)),
                  pl.BlockSpec((None, POOL_HIST, POOL_W), lambda i: (jnp.minimum(i // (SEQ // TM), BATCH - 1), 0, 0))]
    outs = [jax.ShapeDtypeStruct((N_TOK, w), F32) for w in _IN_PROJ_WIDTHS]
    outs += [jax.ShapeDtypeStruct((NP, POOL_W), BF16), jax.ShapeDtypeStruct((BATCH, POOL_HIST, POOL_W), F32)]
    hist = pltpu.VMEM((POOL_HIST, POOL_W), F32)
    if pos is None:
        return pl.pallas_call(
            _in_proj_kernel,
            grid=(N_TILES,),
            in_specs=[*_pair_specs(D_MODEL), *w_specs],
            out_specs=out_specs,
            out_shape=outs,
            scratch_shapes=[hist],
            compiler_params=_params("arbitrary"),
            name="in_proj",
        )(*x, *weights)
    pos_next = pl.BlockSpec((None, 1, TM), lambda i: (jnp.minimum(i + 1, N_TILES - 1), 0, 0), memory_space=pltpu.SMEM)
    return pl.pallas_call(
        _in_proj_gather_kernel,
        grid=(N_TILES,),
        in_specs=[_POS_SPEC, pos_next, _ANY, *w_specs],
        out_specs=out_specs + [row(D_MODEL)],
        out_shape=outs + [jax.ShapeDtypeStruct((N_TOK, D_MODEL), F32)],
        scratch_shapes=[pltpu.VMEM((2, TM, D_MODEL), F32), pltpu.SemaphoreType.DMA((2,)), hist],
        compiler_params=_params("arbitrary"),
        name="in_proj_gather",
    )(pos, pos, x, *weights)


def _softmax_sink_pv(parts, sink, pv):
    m = sink
    for s, _ in parts:
        m = jnp.maximum(m, jnp.max(s, axis=-1, keepdims=True))
    den = jnp.exp(sink - m)
    es = []
    for s, _ in parts:
        e = jnp.exp(s - m)
        den = den + jnp.sum(e, axis=-1, keepdims=True)
        es.append(e)
    inv = 1.0 / den
    out = None
    for e, (_, v) in zip(es, parts):
        o = pv((e * inv).astype(BF16), v)
        out = o if out is None else out + o
    return out


def _head_column(sink_ref, kv, rows_per_head):
    g = lax.broadcasted_iota(I32, (Q_PER_KV * rows_per_head, 1), 0) // rows_per_head
    col = jnp.zeros(g.shape, F32)
    for i in range(Q_PER_KV):
        col = jnp.where(g == i, sink_ref[kv * Q_PER_KV + i], col)
    return col


def _alibi_bias(dist, visible):
    out = np.empty((N_KV_HEADS, Q_PER_KV * dist.shape[0], dist.shape[1]), np.float32)
    for h in range(N_HEADS):
        kv, g = divmod(h, Q_PER_KV)
        out[kv, g * dist.shape[0]:(g + 1) * dist.shape[0]] = np.where(visible, -ALIBI_SLOPES[h] * dist, NEG_INF)
    return out


ATTN_QB = 16


def _attn_prompt_kernel(sink_ref, cur_ref, prev_ref, bias0_ref, bias_ref, o_ref, nk_ref, nv_ref):
    k = jnp.concatenate([prev_ref[:, 0:KV_W], cur_ref[:, ATTN_W:ATTN_W + KV_W]], axis=0).astype(BF16)
    vt = jnp.concatenate([prev_ref[:, KV_W:2 * KV_W], cur_ref[:, ATTN_W + KV_W:PA_W]], axis=0).T.astype(BF16)
    g_of_col = lax.broadcasted_iota(I32, (1, Q_PER_KV * WINDOW), 1) // WINDOW
    for kv in range(N_KV_HEADS):
        ks = slice(kv * HEAD_DIM, (kv + 1) * HEAD_DIM)
        heads = range(kv * Q_PER_KV, (kv + 1) * Q_PER_KV)
        sink = jnp.zeros(g_of_col.shape, F32)
        for g, h in enumerate(heads):
            sink = jnp.where(g_of_col == g, sink_ref[h], sink)
        for j in range(ATTN_QB):
            rows = slice(j * WINDOW, (j + 1) * WINDOW)
            keys = slice(j * WINDOW, (j + 2) * WINDOW)
            q = jnp.concatenate([cur_ref[rows, h * HEAD_DIM:(h + 1) * HEAD_DIM] for h in heads], axis=0)
            bias = bias0_ref[kv] if j == 0 else bias_ref[kv]
            st = _dot_nt(k[keys, ks], (q * (HEAD_DIM ** -0.5)).astype(BF16)) + bias
            m = jnp.maximum(sink, jnp.max(st, axis=0, keepdims=True))
            e = jnp.exp(st - m)
            inv = 1.0 / (jnp.exp(sink - m) + jnp.sum(e, axis=0, keepdims=True))
            ot = _dot(vt[ks, keys], (e * inv).astype(BF16))
            for pair in range(Q_PER_KV // 2):
                two = jnp.concatenate(
                    [ot[:, (2 * pair + g) * WINDOW:(2 * pair + g + 1) * WINDOW] for g in range(2)], axis=0)
                lo = (kv * Q_PER_KV + 2 * pair) * HEAD_DIM
                o_ref[rows, lo:lo + 2 * HEAD_DIM] = two.T.astype(BF16)

    @pl.when(pl.program_id(1) == pl.num_programs(1) - 1)
    def _():
        last = slice((ATTN_QB - 1) * WINDOW, ATTN_QB * WINDOW)
        nk_ref[...] = cur_ref[last, ATTN_W:ATTN_W + KV_W]
        nv_ref[...] = cur_ref[last, ATTN_W + KV_W:PA_W]


def _attn_prompt(pa, sink):
    nb = SEQ // WINDOW
    ns = nb // ATTN_QB
    r = np.arange(WINDOW)[:, None]
    c = np.arange(2 * WINDOW)[None, :]
    band = (c > r) & (c <= WINDOW + r)
    bias = np.stack([_alibi_bias(WINDOW + r - c, band & (c >= WINDOW)), _alibi_bias(WINDOW + r - c, band)])
    bias = jnp.asarray(bias.transpose(0, 1, 3, 2))
    bias_spec = lambda variant: pl.BlockSpec((None, N_KV_HEADS, 2 * WINDOW, Q_PER_KV * WINDOW),
                                             lambda b, i, s: (variant(i), 0, 0, 0))
    state = pl.BlockSpec((None, WINDOW, KV_W), lambda b, i, s: (b, 0, 0))
    return pl.pallas_call(
        _attn_prompt_kernel,
        grid_spec=pltpu.PrefetchScalarGridSpec(
            num_scalar_prefetch=1,
            grid=(BATCH, ns),
            in_specs=[
                pl.BlockSpec((ATTN_QB * WINDOW, PA_W), lambda b, i, s: (b * ns + i, 0)),
                pl.BlockSpec((WINDOW, 2 * KV_W),
                             lambda b, i, s: (b * nb + jnp.maximum(ATTN_QB * i - 1, 0), ATTN_W // (2 * KV_W))),
                bias_spec(lambda i: jnp.minimum(i, 1)), bias_spec(lambda i: 1),
            ],
            out_specs=[pl.BlockSpec((ATTN_QB * WINDOW, ATTN_W), lambda b, i, s: (b * ns + i, 0)), state, state],
        ),
        out_shape=[jax.ShapeDtypeStruct((NP, ATTN_W), BF16),
                   jax.ShapeDtypeStruct((BATCH, WINDOW, KV_W), F32),
                   jax.ShapeDtypeStruct((BATCH, WINDOW, KV_W), F32)],
        compiler_params=_params("arbitrary", "arbitrary"),
        name="attn_prompt",
    )(sink, pa, pa, bias, bias)


ATTN_SB = 16


def _attn_sample_kernel(sink_ref, cur_ref, kc_ref, vc_ref, bias_c_ref, bias_n_ref, o_ref, nk_ref, nv_ref):
    cur = cur_ref[...].reshape(ATTN_SB, DEC_SEQ, PA_W)
    for kv in range(N_KV_HEADS):
        ks = slice(kv * HEAD_DIM, (kv + 1) * HEAD_DIM)
        heads = range(kv * Q_PER_KV, (kv + 1) * Q_PER_KV)
        q = jnp.concatenate([cur[:, :, h * HEAD_DIM:(h + 1) * HEAD_DIM] for h in heads], axis=1)
        q = (q * (HEAD_DIM ** -0.5)).astype(BF16)
        kn = cur[:, :, ATTN_W + kv * HEAD_DIM:ATTN_W + (kv + 1) * HEAD_DIM].astype(BF16)
        vn = cur[:, :, ATTN_W + KV_W + kv * HEAD_DIM:ATTN_W + KV_W + (kv + 1) * HEAD_DIM].astype(BF16)
        sc = _bdot_nt(q, kc_ref[:, :, ks].astype(BF16)) + bias_c_ref[kv]
        sn = _bdot_nt(q, kn) + bias_n_ref[kv]
        sink = _head_column(sink_ref, kv, DEC_SEQ)
        o = _softmax_sink_pv([(sc, vc_ref[:, :, ks].astype(BF16)), (sn, vn)], sink, _bdot)
        for g, h in enumerate(heads):
            o_ref[:, h * HEAD_DIM:(h + 1) * HEAD_DIM] = (
                o[:, g * DEC_SEQ:(g + 1) * DEC_SEQ, :].reshape(ATTN_SB * DEC_SEQ, HEAD_DIM).astype(BF16))
    keep = WINDOW - DEC_SEQ
    nk_ref[:, 0:keep, :] = kc_ref[:, DEC_SEQ:WINDOW, :]
    nk_ref[:, keep:WINDOW, :] = cur[:, :, ATTN_W:ATTN_W + KV_W]
    nv_ref[:, 0:keep, :] = vc_ref[:, DEC_SEQ:WINDOW, :]
    nv_ref[:, keep:WINDOW, :] = cur[:, :, ATTN_W + KV_W:PA_W]


def _attn_sample(pa, sink, cache_k, cache_v, layer):
    rows = ATTN_SB * DEC_SEQ
    first = NP // rows
    t = np.arange(DEC_SEQ)[:, None]
    jc = np.arange(WINDOW)[None, :]
    jn = np.arange(DEC_SEQ)[None, :]
    bias_c = jnp.asarray(_alibi_bias(WINDOW + t - jc, jc > t))
    bias_n = jnp.asarray(_alibi_bias(t - jn, jn <= t))
    cache = pl.BlockSpec((None, ATTN_SB, WINDOW, KV_W), lambda j, s: (layer, j, 0, 0))
    new = pl.BlockSpec((ATTN_SB, WINDOW, KV_W), lambda j, s: (j, 0, 0))
    const = lambda a: pl.BlockSpec(a.shape, lambda j, s: (0,) * a.ndim)
    return pl.pallas_call(
        _attn_sample_kernel,
        grid_spec=pltpu.PrefetchScalarGridSpec(
            num_scalar_prefetch=1,
            grid=(DEC_BATCH // ATTN_SB,),
            in_specs=[pl.BlockSpec((rows, PA_W), lambda j, s: (first + j, 0)), cache, cache,
                      const(bias_c), const(bias_n)],
            out_specs=[pl.BlockSpec((rows, ATTN_W), lambda j, s: (j, 0)), new, new],
        ),
        out_shape=[jax.ShapeDtypeStruct((NS, ATTN_W), BF16),
                   jax.ShapeDtypeStruct((DEC_BATCH, WINDOW, KV_W), F32),
                   jax.ShapeDtypeStruct((DEC_BATCH, WINDOW, KV_W), F32)],
        compiler_params=_params("arbitrary"),
        name="attn_sample",
    )(sink, pa, cache_k, cache_v, bias_c, bias_n)


def _chunk_cumsum(x, chunk):
    pos = lax.broadcasted_iota(I32, x.shape, 0) % chunk
    sh = 1
    while sh < chunk:
        x = x + jnp.where(pos >= sh, pltpu.roll(x, sh, 0), 0.0)
        sh *= 2
    return x


def _gla_log_decay(pag_ref, wa2_ref, ba_ref):
    z = _dot(pag_ref[...].astype(BF16), wa2_ref[...]) + ba_ref[...]
    return _log_sigmoid(z) / GLA_TAU


def _gla_finish(o, rg, g):
    o = o * lax.rsqrt(jnp.mean(o * o, axis=-1, keepdims=True) + RMS_EPS)
    return (o * g * (rg * _sigmoid(rg))).astype(BF16)


GLA_TT = 2048
GLA_AG = 256


def _gla_prompt_kernel(pg_ref, pag_ref, wa2_ref, ba_ref, g_ref, o_ref, s_ref,
                       qd_ref, kd_ref, kdec_ref, a_ref, sall_ref, st_ref, acc_ref):
    nc = GLA_TT // GLA_CHUNK

    @pl.when(pl.program_id(1) == 0)
    def _():
        st_ref[...] = jnp.zeros_like(st_ref)

    cum = _chunk_cumsum(_gla_log_decay(pag_ref, wa2_ref, ba_ref), GLA_CHUNK)
    cum3 = cum.reshape(nc, GLA_CHUNK, GLA_KW)
    tot3 = cum3[:, GLA_CHUNK - 1:GLA_CHUNK, :]
    k = pg_ref[:, GLA_KW:2 * GLA_KW]
    qd_ref[...] = (pg_ref[:, 0:GLA_KW] * (GLA_DK ** -0.5) * jnp.exp(cum)).astype(BF16)
    kd_ref[...] = (k * jnp.exp(-cum)).astype(BF16)
    kdec_ref[...] = (k * jnp.exp(tot3 - cum3).reshape(GLA_TT, GLA_KW)).astype(BF16)
    etot = jnp.exp(tot3)

    def hs(h):
        return slice(h * GLA_DK, (h + 1) * GLA_DK)

    def vs(h):
        return slice(2 * GLA_KW + h * GLA_DV, 2 * GLA_KW + (h + 1) * GLA_DV)

    for c in range(nc):
        rows = slice(c * GLA_CHUNK, (c + 1) * GLA_CHUNK)
        for h in range(GLA_HEADS):
            a_ref[c, :, hs(h)] = _dot_tn(pg_ref[rows, vs(h)].astype(BF16), kdec_ref[rows, hs(h)])

    st = st_ref[...]
    for c in range(nc):
        sall_ref[c] = st.astype(BF16)
        st = etot[c] * st + a_ref[c]
    st_ref[...] = st

    r = lax.broadcasted_iota(I32, (GLA_AG, GLA_AG), 0)
    col = lax.broadcasted_iota(I32, (GLA_AG, GLA_AG), 1)
    causal = (r // GLA_CHUNK == col // GLA_CHUNK) & (col <= r)
    for h in range(GLA_HEADS):
        out = slice(h * GLA_DV, (h + 1) * GLA_DV)
        for c in range(nc):
            rows = slice(c * GLA_CHUNK, (c + 1) * GLA_CHUNK)
            acc_ref[rows, out] = _dot_nt(qd_ref[rows, hs(h)], sall_ref[c, :, hs(h)])
        for a in range(GLA_TT // GLA_AG):
            rows = slice(a * GLA_AG, (a + 1) * GLA_AG)
            att = jnp.where(causal, _dot_nt(qd_ref[rows, hs(h)], kd_ref[rows, hs(h)]), 0.0)
            acc_ref[rows, out] += _dot(att.astype(BF16), pg_ref[rows, vs(h)].astype(BF16))
        rg = pg_ref[:, 2 * GLA_KW + GLA_VW + h * GLA_DV:2 * GLA_KW + GLA_VW + (h + 1) * GLA_DV]
        o_ref[:, out] = _gla_finish(acc_ref[:, out], rg, g_ref[:, out])

    @pl.when(pl.program_id(1) == pl.num_programs(1) - 1)
    def _():
        for h in range(GLA_HEADS):
            s_ref[h] = st_ref[:, hs(h)].T


def _gla_prompt(pg, pag, wa2, ba, g):
    nt = SEQ // GLA_TT
    nc = GLA_TT // GLA_CHUNK
    const = lambda shape: pl.BlockSpec(shape, lambda b, j: (0,) * len(shape))
    return pl.pallas_call(
        _gla_prompt_kernel,
        grid=(BATCH, nt),
        in_specs=[
            pl.BlockSpec((GLA_TT, PG_W), lambda b, j: (b * nt + j, 0)),
            pl.BlockSpec((GLA_TT, LANES), lambda b, j: (b * nt + j, 0)),
            const((LANES, GLA_KW)), const((1, GLA_KW)), const((1, GLA_VW)),
        ],
        out_specs=[
            pl.BlockSpec((GLA_TT, GLA_VW), lambda b, j: (b * nt + j, 0)),
            pl.BlockSpec((None, GLA_HEADS, GLA_DK, GLA_DV), lambda b, j: (b, 0, 0, 0)),
        ],
        out_shape=[jax.ShapeDtypeStruct((NP, GLA_VW), BF16),
                   jax.ShapeDtypeStruct((BATCH, GLA_HEADS, GLA_DK, GLA_DV), F32)],
        scratch_shapes=[pltpu.VMEM((GLA_TT, GLA_KW), BF16), pltpu.VMEM((GLA_TT, GLA_KW), BF16),
                        pltpu.VMEM((GLA_TT, GLA_KW), BF16),
                        pltpu.VMEM((nc, GLA_DV, GLA_KW), F32), pltpu.VMEM((nc, GLA_DV, GLA_KW), BF16),
                        pltpu.VMEM((GLA_DV, GLA_KW), F32), pltpu.VMEM((GLA_TT, GLA_VW), F32)],
        compiler_params=_params("arbitrary", "arbitrary"),
        name="gla_prompt",
    )(pg, pag, wa2, ba, g)


GLA_SB = 8


def _gla_sample_kernel(pg_ref, pag_ref, wa2_ref, ba_ref, g_ref, s0_ref, o_ref, s_ref):
    cum_all = _chunk_cumsum(_gla_log_decay(pag_ref, wa2_ref, ba_ref), DEC_SEQ)
    tri = (lax.broadcasted_iota(I32, (DEC_SEQ, DEC_SEQ), 0) >= lax.broadcasted_iota(I32, (DEC_SEQ, DEC_SEQ), 1))
    for s in range(GLA_SB):
        rows = slice(s * DEC_SEQ, (s + 1) * DEC_SEQ)
        cum = cum_all[rows, :]
        tot = cum[DEC_SEQ - 1:DEC_SEQ, :]
        q = pg_ref[rows, 0:GLA_KW] * (GLA_DK ** -0.5)
        k = pg_ref[rows, GLA_KW:2 * GLA_KW]
        qd = (q * jnp.exp(cum)).astype(BF16)
        kd = (k * jnp.exp(-cum)).astype(BF16)
        kdec = (k * jnp.exp(tot - cum)).astype(BF16)
        etot = jnp.exp(tot)
        etot_col = [jnp.broadcast_to(etot[:, p * LANES:(p + 1) * LANES], (SUBLANES, LANES)).T[:, 0:1]
                    for p in range(GLA_KW // LANES)]
        for h in range(GLA_HEADS):
            ks = slice(h * GLA_DK, (h + 1) * GLA_DK)
            vs = slice(2 * GLA_KW + h * GLA_DV, 2 * GLA_KW + (h + 1) * GLA_DV)
            rs = slice(2 * GLA_KW + GLA_VW + h * GLA_DV, 2 * GLA_KW + GLA_VW + (h + 1) * GLA_DV)
            v = pg_ref[rows, vs].astype(BF16)
            st = s0_ref[s, h]
            att = jnp.where(tri, _dot_nt(qd[:, ks], kd[:, ks]), 0.0)
            o = _dot(qd[:, ks], st.astype(BF16)) + _dot(att.astype(BF16), v)
            per = LANES // GLA_DK
            col = etot_col[h // per][(h % per) * GLA_DK:(h % per + 1) * GLA_DK, :]
            s_ref[s, h] = col * st + _dot_tn(kdec[:, ks], v)
            o_ref[rows, h * GLA_DV:(h + 1) * GLA_DV] = _gla_finish(o, pg_ref[rows, rs], g_ref[:, h * GLA_DV:(h + 1) * GLA_DV])


def _gla_sample(pg, pag, wa2, ba, g, state, layer):
    rows = GLA_SB * DEC_SEQ
    first = NP // rows
    const = lambda shape: pl.BlockSpec(shape, lambda j: (0,) * len(shape))
    return pl.pallas_call(
        _gla_sample_kernel,
        grid=(DEC_BATCH // GLA_SB,),
        in_specs=[
            pl.BlockSpec((rows, PG_W), lambda j: (first + j, 0)),
            pl.BlockSpec((rows, LANES), lambda j: (first + j, 0)),
            const((LANES, GLA_KW)), const((1, GLA_KW)), const((1, GLA_VW)),
            pl.BlockSpec((None, GLA_SB, GLA_HEADS, GLA_DK, GLA_DV), lambda j: (layer, j, 0, 0, 0)),
        ],
        out_specs=[
            pl.BlockSpec((rows, GLA_VW), lambda j: (j, 0)),
            pl.BlockSpec((GLA_SB, GLA_HEADS, GLA_DK, GLA_DV), lambda j: (j, 0, 0, 0)),
        ],
        out_shape=[jax.ShapeDtypeStruct((NS, GLA_VW), BF16),
                   jax.ShapeDtypeStruct((DEC_BATCH, GLA_HEADS, GLA_DK, GLA_DV), F32)],
        compiler_params=_params("arbitrary"),
        name="gla_sample",
    )(pg, pag, wa2, ba, g, state)


POOL_HIST = 16


def _pool_groups(ext, cnt, pw_ref, ps_ref, out_rows):
    ax = ext.ndim - 2
    outs = []
    for g, w in enumerate(POOL_WINDOWS):
        x = ext[..., g * POOL_GW:(g + 1) * POOL_GW]
        s, sh = x, 1
        while sh < w:
            s = s + pltpu.roll(s, sh, ax)
            sh *= 2
        inv = 1.0 / cnt[g]
        if ext.ndim == 3:
            d = (s[:, POOL_HIST:, :] * inv - x[:, POOL_HIST:, :]).reshape(out_rows, POOL_GW)
        else:
            d = s[POOL_HIST:, :] * inv - x[POOL_HIST:, :]
        y = _dot(d.astype(BF16), pw_ref[g]) * ps_ref[:, g * POOL_GW:(g + 1) * POOL_GW]
        outs.append(y.astype(BF16))
    return outs


POOL_SB = 16


def _pool_sample_kernel(ext_ref, pw_ref, ps_ref, o_ref):
    cnt = [float(w) for w in POOL_WINDOWS]
    for g, y in enumerate(_pool_groups(ext_ref[...], cnt, pw_ref, ps_ref, POOL_SB * DEC_SEQ)):
        o_ref[:, g * POOL_GW:(g + 1) * POOL_GW] = y


def _pool_sample(ext, pw, ps):
    rows = POOL_SB * DEC_SEQ
    return pl.pallas_call(
        _pool_sample_kernel,
        grid=(DEC_BATCH // POOL_SB,),
        in_specs=[
            pl.BlockSpec((POOL_SB, POOL_HIST + DEC_SEQ, POOL_W), lambda j: (j, 0, 0)),
            pl.BlockSpec((POOL_GROUPS, POOL_GW, POOL_GW), lambda j: (0, 0, 0)),
            pl.BlockSpec((1, POOL_W), lambda j: (0, 0)),
        ],
        out_specs=pl.BlockSpec((rows, POOL_W), lambda j: (j, 0)),
        out_shape=jax.ShapeDtypeStruct((NS, POOL_W), BF16),
        compiler_params=_params("arbitrary"),
        name="pool_sample",
    )(ext, pw, ps)


def _route(sc, sel):
    gscore = []
    for g in range(N_GROUPS):
        v = sel[EXPERTS_PER_GROUP * g:EXPERTS_PER_GROUP * (g + 1)]
        best = None
        for a, b in PAIRS:
            pair = v[a] + v[b]
            best = pair if best is None else jnp.maximum(best, pair)
        gscore.append(best)
    gi = jnp.zeros_like(gscore[0], dtype=I32)
    best = gscore[0]
    for g in range(1, N_GROUPS):
        upd = gscore[g] > best
        gi = jnp.where(upd, g, gi)
        best = jnp.where(upd, gscore[g], best)

    def in_group(rows, j):
        out = rows[(N_GROUPS - 1) * EXPERTS_PER_GROUP + j]
        for g in range(N_GROUPS - 2, -1, -1):
            out = jnp.where(gi == g, rows[g * EXPERTS_PER_GROUP + j], out)
        return out

    u = [in_group(sel, j) for j in range(EXPERTS_PER_GROUP)]
    s_in = [in_group(sc, j) for j in range(EXPERTS_PER_GROUP)]

    def argmax4(vals):
        idx = jnp.zeros_like(gi)
        m = vals[0]
        for j in range(1, EXPERTS_PER_GROUP):
            upd = vals[j] > m
            idx = jnp.where(upd, j, idx)
            m = jnp.where(upd, vals[j], m)
        return idx

    def pick(vals, idx):
        out = vals[EXPERTS_PER_GROUP - 1]
        for j in range(EXPERTS_PER_GROUP - 2, -1, -1):
            out = jnp.where(idx == j, vals[j], out)
        return out

    i1 = argmax4(u)
    i2 = argmax4([jnp.where(i1 == j, NEG_INF, u[j]) for j in range(EXPERTS_PER_GROUP)])
    w1, w2 = pick(s_in, i1), pick(s_in, i2)
    tot = w1 + w2
    w1, w2 = w1 / tot, w2 / tot
    lo, hi = jnp.minimum(i1, i2), jnp.maximum(i1, i2)
    first_lo = i1 < i2
    w_lo, w_hi = jnp.where(first_lo, w1, w2), jnp.where(first_lo, w2, w1)
    pair = jnp.where(lo == 0, hi - 1, jnp.where(hi == EXPERTS_PER_GROUP - 1, lo + 2, len(PAIRS) - 1))
    swapped = pair == len(PAIRS) - 1
    return gi * len(PAIRS) + pair, jnp.where(swapped, w_hi, w_lo), jnp.where(swapped, w_lo, w_hi)


def _merge_kernel(*refs):
    (oap_ref, oas_ref, ogp_ref, ogs_ref, opp_ref, ops_ref, wg_ref, wb_ref, wo_ref,
     g1_ref, b1_ref, wr_ref, rb_ref, xe_ref, meta_ref) = refs[-15:]
    x = refs[0][...] if len(refs) == 16 else _pair_tile(refs[0], refs[1])
    xb = x.astype(BF16)
    merged = None
    for n, (brp, brs) in enumerate(((oap_ref, oas_ref), (ogp_ref, ogs_ref), (opp_ref, ops_ref))):
        gate = _sigmoid(_dot(xb, wg_ref[:, n * D_MODEL:(n + 1) * D_MODEL]))
        term = gate * _dot(_pair_tile(brp, brs), wb_ref[n])
        merged = term if merged is None else merged + term
    mix = _dot(merged.astype(BF16), wo_ref[...])
    x1 = _layer_norm(DN_ALPHA * x + mix, g1_ref[...], b1_ref[...])
    xe_ref[:, 0:D_MODEL] = x1
    sc_t = _sigmoid(_dot_nt(wr_ref[...], x1.astype(BF16)))
    sel_t = sc_t + rb_ref[...]
    sc = [sc_t[e:e + 1, :] for e in range(N_EXPERTS)]
    sel = [sel_t[e:e + 1, :] for e in range(N_EXPERTS)]
    cls, w_first, w_second = _route(sc, sel)
    cls = cls.astype(F32)

    def rows(n):
        rid = lax.broadcasted_iota(I32, (n, TM), 0)
        return jnp.where(rid == 0, cls, jnp.where(rid == 1, w_first, jnp.where(rid == 2, w_second, 0.0)))

    meta_ref[...] = rows(SUBLANES)
    xe_ref[:, D_MODEL:XE_W] = rows(LANES).T


def _merge(x, branches, wg, wb, wo, g1, b1, wr_t, rb, layer):
    row = lambda w: pl.BlockSpec((TM, w), lambda i: (i, 0))
    lay = lambda *shape: pl.BlockSpec((None,) + shape, lambda i: (layer,) + (0,) * len(shape))
    const = lambda *shape: pl.BlockSpec(shape, lambda i: (0,) * len(shape))
    return pl.pallas_call(
        _merge_kernel,
        grid=(N_TILES,),
        in_specs=[*(_pair_specs(D_MODEL) if isinstance(x, tuple) else (row(D_MODEL),)),
                  *(_pair_specs(BRANCH_W) * N_BRANCH),
                  lay(D_MODEL, N_BRANCH * D_MODEL), lay(N_BRANCH, BRANCH_W, D_MODEL), lay(D_MODEL, D_MODEL),
                  lay(1, D_MODEL), lay(1, D_MODEL), const(N_EXPERTS, D_MODEL), const(N_EXPERTS, 1)],
        out_specs=[row(XE_W), pl.BlockSpec((None, SUBLANES, TM), lambda i: (i, 0, 0))],
        out_shape=[jax.ShapeDtypeStruct((N_TOK, XE_W), F32), jax.ShapeDtypeStruct((N_TILES, SUBLANES, TM), F32)],
        compiler_params=_params("arbitrary"),
        name="merge",
    )(*(x if isinstance(x, tuple) else (x,)), *branches, wg, wb, wo, g1, b1, wr_t, rb)


def _plan_kernel(cls_ref, pos_ref, tcls_ref, nused_ref):
    cls = cls_ref[...]
    lane_r = lax.broadcasted_iota(I32, (LANES, 2 * LANES), 0)
    lane_c = lax.broadcasted_iota(I32, (LANES, 2 * LANES), 1)
    lane_mat = ((lane_c >= LANES) | (lane_r < lane_c)).astype(BF16)
    row_r = lax.broadcasted_iota(I32, (2 * PLAN_ROWS, PLAN_ROWS), 0)
    row_c = lax.broadcasted_iota(I32, (2 * PLAN_ROWS, PLAN_ROWS), 1)
    row_mat = ((row_r >= PLAN_ROWS) | (row_c < row_r)).astype(BF16)
    tile_start = (lax.broadcasted_iota(I32, (SUBLANES, LANES), 1) * TM_E).astype(F32)
    pos = jnp.zeros((PLAN_ROWS, LANES), F32)
    off = jnp.zeros((PLAN_ROWS, LANES), F32)
    tcls = jnp.zeros((SUBLANES, LANES), I32)
    for c in range(N_CLASSES):
        m = cls == c
        lanes = _dot(m.astype(BF16), lane_mat)
        rows = _dot(row_mat, lanes[:, LANES:].astype(BF16))
        rank = lanes[:, 0:LANES] + rows[0:PLAN_ROWS]
        count = rows[PLAN_ROWS:]
        pos = jnp.where(m, off + rank, pos)
        off = off + jnp.ceil(count * (1.0 / TM_E)) * TM_E
        tcls = tcls + (off[0:SUBLANES] <= tile_start).astype(I32)
    pos_ref[...] = pos.astype(I32)
    tcls_ref[...] = tcls
    nused_ref[...] = (off[0:SUBLANES] * (1.0 / TM_E)).astype(I32)


def _plan(cls2d):
    return pl.pallas_call(
        _plan_kernel,
        out_shape=[jax.ShapeDtypeStruct((PLAN_ROWS, LANES), I32),
                   jax.ShapeDtypeStruct((SUBLANES, LANES), I32),
                   jax.ShapeDtypeStruct((SUBLANES, LANES), I32)],
        compiler_params=pltpu.CompilerParams(vmem_limit_bytes=VMEM_LIMIT),
        name="plan",
    )(cls2d)


def _row_copies(pos_ref, tile_ref, sorted_hbm, sem, scatter):
    def copy(r):
        row, srt = tile_ref.at[pl.ds(r, 1)], sorted_hbm.at[pl.ds(pos_ref[0, r], 1)]
        return pltpu.make_async_copy(row, srt, sem) if scatter else pltpu.make_async_copy(srt, row, sem)

    def wait(r, carry):
        copy(r).wait()
        return carry

    rows = tile_ref.shape[0]
    for r in range(rows):
        copy(r).start(priority=r % DMA_THREADS)
    lax.fori_loop(0, rows, wait, 0, unroll=8)


def _dispatch_kernel(pos_ref, x_ref, xs_in_hbm, xs_hbm, sem):
    del xs_in_hbm
    _row_copies(pos_ref, x_ref, xs_hbm, sem, scatter=True)


def _combine_kernel(pos_ref, ys_hbm, xp_ref, xs_ref, sem):
    @pl.when(pl.program_id(0) < PT)
    def _():
        _row_copies(pos_ref, xp_ref, ys_hbm, sem, scatter=False)

    @pl.when(pl.program_id(0) >= PT)
    def _():
        _row_copies(pos_ref, xs_ref, ys_hbm, sem, scatter=False)


_POS_SPEC = pl.BlockSpec((None, 1, TM), lambda i: (i, 0, 0), memory_space=pltpu.SMEM)
_ANY = pl.BlockSpec(memory_space=pl.ANY)


TM_D = 1024


def _dispatch(pos, xe, xs_prev):
    pos = pos.reshape(N_TOK // TM_D, 1, TM_D)
    return pl.pallas_call(
        _dispatch_kernel,
        grid=(N_TOK // TM_D,),
        in_specs=[pl.BlockSpec((None, 1, TM_D), lambda i: (i, 0, 0), memory_space=pltpu.SMEM),
                  pl.BlockSpec((TM_D, XE_W), lambda i: (i, 0)), _ANY],
        out_specs=_ANY,
        out_shape=jax.ShapeDtypeStruct((N_SORTED, XE_W), F32),
        scratch_shapes=[pltpu.SemaphoreType.DMA(())],
        input_output_aliases={2: 0},
        compiler_params=_params("arbitrary"),
        name="dispatch",
    )(pos, xe, xs_prev)


def _combine(pos, ys):
    return pl.pallas_call(
        _combine_kernel,
        grid=(N_TILES,),
        in_specs=[_POS_SPEC, _ANY],
        out_specs=list(_pair_specs(D_MODEL)),
        out_shape=[jax.ShapeDtypeStruct((NP, D_MODEL), F32), jax.ShapeDtypeStruct((NS, D_MODEL), F32)],
        scratch_shapes=[pltpu.SemaphoreType.DMA(())],
        compiler_params=_params("arbitrary"),
        name="combine",
    )(pos, ys)


def _experts_kernel(ea_ref, eb_ref, new_a_ref, new_b_ref, nused_ref, xs_ref,
                    wga_ref, wua_ref, wda_ref, wgb_ref, wub_ref, wdb_ref, g2_ref, b2_ref, ys_ref,
                    wga_s, wua_s, wda_s, wgb_s, wub_s, wdb_s, pre_ref):
    del ea_ref, eb_ref
    i = pl.program_id(0)
    nused = nused_ref[0]

    @pl.when(new_a_ref[i] == 1)
    def _():
        for src, dst in ((wga_ref, wga_s), (wua_ref, wua_s), (wda_ref, wda_s)):
            dst[...] = src[...].astype(BF16)

    @pl.when(new_b_ref[i] == 1)
    def _():
        for src, dst in ((wgb_ref, wgb_s), (wub_ref, wub_s), (wdb_ref, wdb_s)):
            dst[...] = src[...].astype(BF16)

    def mlp():
        x1 = xs_ref[:, 0:D_MODEL]
        xb = x1.astype(BF16)

        def expert(wg, wu, wd):
            a = _dot(xb, wg[...])
            h = a * _sigmoid(a) * _dot(xb, wu[...])
            return _dot(h.astype(BF16), wd[...])

        ffn = xs_ref[:, D_MODEL + 1:D_MODEL + 2] * expert(wga_s, wua_s, wda_s)
        ffn = ffn + xs_ref[:, D_MODEL + 2:D_MODEL + 3] * expert(wgb_s, wub_s, wdb_s)
        pre_ref[i % 2] = DN_ALPHA * x1 + ffn

    def norm():
        ys_ref[...] = _layer_norm(pre_ref[(i + 1) % 2], g2_ref[...], b2_ref[...])

    @pl.when(i == 0)
    def _():
        mlp()

    @pl.when((i > 0) & (i < nused))
    def _():
        norm()
        mlp()

    @pl.when(i == nused)
    def _():
        norm()

    @pl.when(i > nused)
    def _():
        ys_ref[...] = jnp.zeros_like(ys_ref)


def _experts(tile_a, tile_b, new_a, new_b, nused, xs, w_gate, w_up, w_down, g2, b2, layer):
    tile = lambda w: pl.BlockSpec((TM_E, w), lambda i, ea, eb, na, nb, nu: (jnp.minimum(i, nu[0] - 1), 0))
    wa = lambda *shape: pl.BlockSpec((None, None) + shape, lambda i, ea, eb, na, nb, nu: (layer, ea[i], 0, 0))
    wb = lambda *shape: pl.BlockSpec((None, None) + shape, lambda i, ea, eb, na, nb, nu: (layer, eb[i], 0, 0))
    lay = pl.BlockSpec((None, 1, D_MODEL), lambda i, ea, eb, na, nb, nu: (layer, 0, 0))
    up, down = pltpu.VMEM((D_MODEL, D_EXPERT), BF16), pltpu.VMEM((D_EXPERT, D_MODEL), BF16)
    return pl.pallas_call(
        _experts_kernel,
        grid_spec=pltpu.PrefetchScalarGridSpec(
            num_scalar_prefetch=5,
            grid=(NT_E + 1,),
            in_specs=[tile(XE_W),
                      wa(D_MODEL, D_EXPERT), wa(D_MODEL, D_EXPERT), wa(D_EXPERT, D_MODEL),
                      wb(D_MODEL, D_EXPERT), wb(D_MODEL, D_EXPERT), wb(D_EXPERT, D_MODEL),
                      lay, lay],
            out_specs=pl.BlockSpec((TM_E, D_MODEL), lambda i, ea, eb, na, nb, nu: (jnp.maximum(i - 1, 0), 0)),
            scratch_shapes=[up, up, down, up, up, down, pltpu.VMEM((2, TM_E, D_MODEL), F32)],
        ),
        out_shape=jax.ShapeDtypeStruct((N_SORTED, D_MODEL), F32),
        compiler_params=_params("arbitrary"),
        name="experts",
    )(tile_a, tile_b, new_a, new_b, nused, xs, w_gate, w_up, w_down, w_gate, w_up, w_down, g2, b2)


_PAIR_LO = tuple(a for a, _ in PAIRS)
_PAIR_HI = tuple(b for _, b in PAIRS)


def kernel(x_prompt, x_sample, cache_attn_k, cache_attn_v, state_gla, state_pool, w_in, w_gate, attn_sink,
           gla_w_a2, gla_b_a, gla_norm_g, pool_w, pool_scale, w_branch, w_o, ln1_g, ln1_b, ln2_g, ln2_b,
           w_router, router_bias, w_e_gate, w_e_up, w_e_down):
    ag0 = PA_W + PG_W
    w_in_p = (w_in[:, :, :ag0].astype(BF16), w_in[:, :, ag0 + GLA_RANK:].astype(BF16),
              jnp.pad(w_in[:, :, ag0:ag0 + GLA_RANK], ((0, 0), (0, 0), (0, LANES - GLA_RANK))).astype(BF16))
    w_gate_b, w_branch_b, w_o_b = w_gate.astype(BF16), w_branch.astype(BF16), w_o.astype(BF16)
    wa2_p = jnp.pad(gla_w_a2, ((0, 0), (0, LANES - GLA_RANK), (0, 0))).astype(BF16)
    pool_w_b = pool_w.astype(BF16)
    pool_scale_r = pool_scale.reshape(DEPTH, 1, POOL_W)
    wr_t = w_router.T.astype(BF16)
    rb = router_bias.reshape(N_EXPERTS, 1).astype(F32)
    cache_k = cache_attn_k.reshape(DEPTH, DEC_BATCH, WINDOW, KV_W)
    cache_v = cache_attn_v.reshape(DEPTH, DEC_BATCH, WINDOW, KV_W)
    pair_lo, pair_hi = jnp.array(_PAIR_LO, I32), jnp.array(_PAIR_HI, I32)

    x = (x_prompt.reshape(NP, D_MODEL), x_sample.reshape(NS, D_MODEL))
    xs = jnp.zeros((N_SORTED, XE_W), F32)
    ys = pos = None
    pk, pv, ps, pp, sk, sv, ss, sp = ([] for _ in range(8))
    for l in range(DEPTH):
        if l == 0:
            pa, pg, pu, pag, op_p, u_last = _in_proj(x, w_in_p, pool_w_b, pool_scale_r, l)
        else:
            pa, pg, pu, pag, op_p, u_last, x = _in_proj(ys, w_in_p, pool_w_b, pool_scale_r, l, pos)
        row2 = lambda a: a[l].reshape(1, -1)

        oa_p, k_p, v_p = _attn_prompt(pa, attn_sink[l])
        oa_s, nk, nv = _attn_sample(pa, attn_sink[l], cache_k, cache_v, l)
        og_p, s_p = _gla_prompt(pg, pag, wa2_p[l], row2(gla_b_a), row2(gla_norm_g))
        og_s, s_s = _gla_sample(pg, pag, wa2_p[l], row2(gla_b_a), row2(gla_norm_g), state_gla, l)
        u_s = pu[NP:].reshape(DEC_BATCH, DEC_SEQ, POOL_W)
        ext_s = jnp.concatenate(
            [jnp.zeros((DEC_BATCH, POOL_HIST - POOL_STATE, POOL_W), F32), state_pool[l], u_s], axis=1)
        op_s = _pool_sample(ext_s, pool_w_b[l], row2(pool_scale))

        xe, meta = _merge(x, (oa_p, oa_s, og_p, og_s, op_p, op_s), w_gate_b, w_branch_b, w_o_b,
                          ln1_g.reshape(DEPTH, 1, D_MODEL), ln1_b.reshape(DEPTH, 1, D_MODEL), wr_t, rb, l)
        cls = meta[:, 0, :].reshape(N_TOK // LANES, LANES).astype(I32)
        cls2d = jnp.pad(cls, ((0, PLAN_ROWS - N_TOK // LANES), (0, 0)), constant_values=-1)
        pos2d, tcls, nused = _plan(cls2d)
        pos = pos2d.reshape(-1)[:N_TOK].reshape(N_TILES, 1, TM)
        n_used = nused[0, :1]
        tile_cls = tcls[0, :NT_E + 1]
        tile_cls = jnp.where(jnp.arange(NT_E + 1) < n_used, tile_cls, tile_cls[n_used[0] - 1])
        tile_a = (tile_cls // len(PAIRS)) * EXPERTS_PER_GROUP + pair_lo[tile_cls % len(PAIRS)]
        tile_b = (tile_cls // len(PAIRS)) * EXPERTS_PER_GROUP + pair_hi[tile_cls % len(PAIRS)]
        first = jnp.ones((1,), I32)
        new_a = jnp.concatenate([first, (tile_a[1:] != tile_a[:-1]).astype(I32)])
        new_b = jnp.concatenate([first, (tile_b[1:] != tile_b[:-1]).astype(I32)])
        xs = _dispatch(pos, xe, xs)
        ys = _experts(tile_a, tile_b, new_a, new_b, n_used, xs, w_e_gate, w_e_up, w_e_down,
                      ln2_g.reshape(DEPTH, 1, D_MODEL), ln2_b.reshape(DEPTH, 1, D_MODEL), l)

        pk.append(k_p.reshape(BATCH, WINDOW, N_KV_HEADS, HEAD_DIM))
        pv.append(v_p.reshape(BATCH, WINDOW, N_KV_HEADS, HEAD_DIM))
        ps.append(s_p)
        pp.append(u_last[:, POOL_HIST - POOL_STATE:, :])
        sk.append(nk.reshape(DEC_BATCH, WINDOW, N_KV_HEADS, HEAD_DIM))
        sv.append(nv.reshape(DEC_BATCH, WINDOW, N_KV_HEADS, HEAD_DIM))
        ss.append(s_s)
        sp.append(ext_s[:, POOL_HIST + DEC_SEQ - POOL_STATE:, :])
    y_prompt, y_sample = _combine(pos, ys)
    return (y_prompt.reshape(BATCH, SEQ, D_MODEL), y_sample.reshape(DEC_BATCH, DEC_SEQ, D_MODEL),
            jnp.stack(pk), jnp.stack(pv), jnp.stack(ps), jnp.stack(pp),
            jnp.stack(sk), jnp.stack(sv), jnp.stack(ss), jnp.stack(sp))
```

```python
import jax
import jax.numpy as jnp
import numpy as np
from jax import lax
from jax.experimental import pallas as pl
from jax.experimental.pallas import tpu as pltpu

F32, BF16, I32 = jnp.float32, jnp.bfloat16, jnp.int32

D_MODEL = 1024
BATCH = 8
SEQ = 2048
DEPTH = 4
DEC_BATCH = 128
DEC_SEQ = 8
N_HEADS = 8
N_KV_HEADS = 2
HEAD_DIM = 64
WINDOW = 128
ATTN_W = N_HEADS * HEAD_DIM
KV_W = N_KV_HEADS * HEAD_DIM
Q_PER_KV = N_HEADS // N_KV_HEADS
GLA_HEADS = 4
GLA_DK = 64
GLA_DV = 128
GLA_KW = GLA_HEADS * GLA_DK
GLA_VW = GLA_HEADS * GLA_DV
GLA_RANK = 16
GLA_TAU = 16.0
GLA_CHUNK = 64
POOL_WINDOWS = (2, 4, 8, 16)
POOL_GROUPS = 4
POOL_GW = 128
POOL_W = POOL_GROUPS * POOL_GW
POOL_STATE = max(POOL_WINDOWS) - 1
N_BRANCH = 3
BRANCH_W = 512
N_EXPERTS = 16
N_GROUPS = 4
EXPERTS_PER_GROUP = N_EXPERTS // N_GROUPS
D_EXPERT = 512
DN_ALPHA = (2.0 * DEPTH) ** 0.25
LN_EPS = 1e-5
RMS_EPS = 1e-6
NEG_INF = -1e30
ALIBI_SLOPES = tuple(2.0 ** (-8.0 * h / N_HEADS) for h in range(1, N_HEADS + 1))

LANES = 128
SUBLANES = 8

NP = BATCH * SEQ
NS = DEC_BATCH * DEC_SEQ
N_TOK = NP + NS
TM = 512
N_TILES = N_TOK // TM
PT = NP // TM

PA_W = ATTN_W + 2 * KV_W
PG_W = 2 * GLA_KW + 2 * GLA_VW

PAIRS = ((0, 1), (0, 2), (0, 3), (1, 3), (2, 3), (2, 1))
N_CLASSES = N_GROUPS * len(PAIRS)
TM_E = 512
NT_E = -(-(N_TOK + N_CLASSES * (TM_E - 1)) // TM_E)
N_SORTED = NT_E * TM_E
XE_W = D_MODEL + LANES
PLAN_ROWS = 256

VMEM_LIMIT = 56 * 1024 * 1024
DMA_THREADS = 2


def _params(*sem):
    return pltpu.CompilerParams(dimension_semantics=sem, vmem_limit_bytes=VMEM_LIMIT)


def _dot(a, b):
    return jnp.dot(a, b, preferred_element_type=F32)


def _dot_nt(a, b):
    return lax.dot_general(a, b, (((1,), (1,)), ((), ())), preferred_element_type=F32)


def _dot_tn(a, b):
    return lax.dot_general(a, b, (((0,), (0,)), ((), ())), preferred_element_type=F32)


def _bdot_nt(a, b):
    return lax.dot_general(a, b, (((2,), (2,)), ((0,), (0,))), preferred_element_type=F32)


def _bdot(a, b):
    return lax.dot_general(a, b, (((2,), (1,)), ((0,), (0,))), preferred_element_type=F32)


def _layer_norm(h, g, b):
    mu = jnp.mean(h, axis=-1, keepdims=True)
    hc = h - mu
    var = jnp.mean(hc * hc, axis=-1, keepdims=True)
    return hc * lax.rsqrt(var + LN_EPS) * g + b


def _sigmoid(x):
    return 0.5 * jnp.tanh(0.5 * x) + 0.5


def _log_sigmoid(x):
    return jnp.minimum(x, 0.0) - jnp.log1p(jnp.exp(-jnp.abs(x)))


def _pair_specs(width):
    return (pl.BlockSpec((TM, width), lambda i, *_: (jnp.minimum(i, PT - 1), 0)),
            pl.BlockSpec((TM, width), lambda i, *_: (jnp.maximum(i - PT, 0), 0)))


def _pair_tile(p_ref, s_ref):
    return jnp.where(pl.program_id(0) < PT, p_ref[...], s_ref[...])


_IN_PROJ_WIDTHS = (PA_W, PG_W, POOL_W, LANES)


def _in_proj_tile(x, w_refs, pool_refs, out_refs, hist_ref):
    w_main, w_pool, w_rank = w_refs
    pw_ref, ps_ref = pool_refs
    pa_ref, pg_ref, pu_ref, pag_ref, op_ref, last_ref = out_refs
    xb = x.astype(BF16)
    pa_ref[...] = _dot(xb, w_main[:, 0:PA_W])
    pg_ref[...] = _dot(xb, w_main[:, PA_W:PA_W + PG_W])
    u = _dot(xb, w_pool[...])
    pu_ref[...] = u
    pag_ref[...] = _dot(xb, w_rank[...])

    i = pl.program_id(0)
    tiles_per_seq = SEQ // TM

    @pl.when(i < PT)
    def _():
        j = i % tiles_per_seq

        @pl.when(j == 0)
        def _():
            hist_ref[...] = jnp.zeros_like(hist_ref)

        ext = jnp.concatenate([hist_ref[...], u], axis=0)
        pos = j * TM + lax.broadcasted_iota(I32, (TM, 1), 0)
        cnt = [jnp.minimum(pos + 1, w).astype(F32) for w in POOL_WINDOWS]
        for g, y in enumerate(_pool_groups(ext, cnt, pw_ref, ps_ref, TM)):
            op_ref[:, g * POOL_GW:(g + 1) * POOL_GW] = y
        hist_ref[...] = u[TM - POOL_HIST:, :]

        @pl.when(j == tiles_per_seq - 1)
        def _():
            last_ref[...] = u[TM - POOL_HIST:, :]


def _in_proj_kernel(xp_ref, xs_ref, wm_ref, wp_ref, wr_ref, pw_ref, ps_ref,
                    pa_ref, pg_ref, pu_ref, pag_ref, op_ref, last_ref, hist_ref):
    _in_proj_tile(_pair_tile(xp_ref, xs_ref), (wm_ref, wp_ref, wr_ref), (pw_ref, ps_ref),
                  (pa_ref, pg_ref, pu_ref, pag_ref, op_ref, last_ref), hist_ref)


def _in_proj_gather_kernel(pos_ref, pos_next_ref, ys_hbm, wm_ref, wp_ref, wr_ref, pw_ref, ps_ref,
                           pa_ref, pg_ref, pu_ref, pag_ref, op_ref, last_ref, x_ref, buf, sem, hist_ref):
    i = pl.program_id(0)
    slot = i % 2

    def copy(p_ref, r, s):
        return pltpu.make_async_copy(ys_hbm.at[pl.ds(p_ref[0, r], 1)], buf.at[s, pl.ds(r, 1)], sem.at[s])

    def wait_tile(p_ref, s):
        def wait(r, carry):
            copy(p_ref, r, s).wait()
            return carry
        lax.fori_loop(0, TM, wait, 0, unroll=8)

    @pl.when(i == 0)
    def _():
        for r in range(TM):
            copy(pos_ref, r, 0).start(priority=r % DMA_THREADS)

    wait_tile(pos_ref, slot)
    x = buf[slot]
    for r in range(TM):
        copy(pos_next_ref, r, 1 - slot).start(priority=r % DMA_THREADS)
    x_ref[...] = x
    _in_proj_tile(x, (wm_ref, wp_ref, wr_ref), (pw_ref, ps_ref),
                  (pa_ref, pg_ref, pu_ref, pag_ref, op_ref, last_ref), hist_ref)

    @pl.when(i == pl.num_programs(0) - 1)
    def _():
        wait_tile(pos_next_ref, 1 - slot)


def _in_proj(x, w_in_parts, pool_w, pool_scale, layer, pos=None):
    row = lambda w: pl.BlockSpec((TM, w), lambda i: (i, 0))
    w_specs = [pl.BlockSpec((None,) + w.shape[1:], lambda i: (layer, 0, 0)) for w in w_in_parts]
    w_specs += [pl.BlockSpec((None, POOL_GROUPS, POOL_GW, POOL_GW), lambda i: (layer, 0, 0, 0)),
                pl.BlockSpec((None, 1, POOL_W), lambda i: (layer, 0, 0))]
    weights = (*w_in_parts, pool_w, pool_scale)
    out_specs = [row(w) for w in _IN_PROJ_WIDTHS]
    out_specs += [pl.BlockSpec((TM, POOL_W), lambda i: (jnp.minimum(i, PT - 1), 0)),
                  pl.BlockSpec((None, POOL_HIST, POOL_W), lambda i: (jnp.minimum(i // (SEQ // TM), BATCH - 1), 0, 0))]
    outs = [jax.ShapeDtypeStruct((N_TOK, w), F32) for w in _IN_PROJ_WIDTHS]
    outs += [jax.ShapeDtypeStruct((NP, POOL_W), BF16), jax.ShapeDtypeStruct((BATCH, POOL_HIST, POOL_W), F32)]
    hist = pltpu.VMEM((POOL_HIST, POOL_W), F32)
    if pos is None:
        return pl.pallas_call(
            _in_proj_kernel,
            grid=(N_TILES,),
            in_specs=[*_pair_specs(D_MODEL), *w_specs],
            out_specs=out_specs,
            out_shape=outs,
            scratch_shapes=[hist],
            compiler_params=_params("arbitrary"),
            name="in_proj",
        )(*x, *weights)
    pos_next = pl.BlockSpec((None, 1, TM), lambda i: (jnp.minimum(i + 1, N_TILES - 1), 0, 0), memory_space=pltpu.SMEM)
    return pl.pallas_call(
        _in_proj_gather_kernel,
        grid=(N_TILES,),
        in_specs=[_POS_SPEC, pos_next, _ANY, *w_specs],
        out_specs=out_specs + [row(D_MODEL)],
        out_shape=outs + [jax.ShapeDtypeStruct((N_TOK, D_MODEL), F32)],
        scratch_shapes=[pltpu.VMEM((2, TM, D_MODEL), F32), pltpu.SemaphoreType.DMA((2,)), hist],
        compiler_params=_params("arbitrary"),
        name="in_proj_gather",
    )(pos, pos, x, *weights)


def _softmax_sink_pv(parts, sink, pv):
    m = sink
    for s, _ in parts:
        m = jnp.maximum(m, jnp.max(s, axis=-1, keepdims=True))
    den = jnp.exp(sink - m)
    es = []
    for s, _ in parts:
        e = jnp.exp(s - m)
        den = den + jnp.sum(e, axis=-1, keepdims=True)
        es.append(e)
    inv = 1.0 / den
    out = None
    for e, (_, v) in zip(es, parts):
        o = pv((e * inv).astype(BF16), v)
        out = o if out is None else out + o
    return out


def _head_column(sink_ref, kv, rows_per_head):
    g = lax.broadcasted_iota(I32, (Q_PER_KV * rows_per_head, 1), 0) // rows_per_head
    col = jnp.zeros(g.shape, F32)
    for i in range(Q_PER_KV):
        col = jnp.where(g == i, sink_ref[kv * Q_PER_KV + i], col)
    return col


def _alibi_bias(dist, visible):
    out = np.empty((N_KV_HEADS, Q_PER_KV * dist.shape[0], dist.shape[1]), np.float32)
    for h in range(N_HEADS):
        kv, g = divmod(h, Q_PER_KV)
        out[kv, g * dist.shape[0]:(g + 1) * dist.shape[0]] = np.where(visible, -ALIBI_SLOPES[h] * dist, NEG_INF)
    return out


ATTN_QB = 16


def _attn_prompt_kernel(sink_ref, cur_ref, prev_ref, bias0_ref, bias_ref, o_ref, nk_ref, nv_ref):
    k = jnp.concatenate([prev_ref[:, 0:KV_W], cur_ref[:, ATTN_W:ATTN_W + KV_W]], axis=0).astype(BF16)
    vt = jnp.concatenate([prev_ref[:, KV_W:2 * KV_W], cur_ref[:, ATTN_W + KV_W:PA_W]], axis=0).T.astype(BF16)
    g_of_col = lax.broadcasted_iota(I32, (1, Q_PER_KV * WINDOW), 1) // WINDOW
    for kv in range(N_KV_HEADS):
        ks = slice(kv * HEAD_DIM, (kv + 1) * HEAD_DIM)
        heads = range(kv * Q_PER_KV, (kv + 1) * Q_PER_KV)
        sink = jnp.zeros(g_of_col.shape, F32)
        for g, h in enumerate(heads):
            sink = jnp.where(g_of_col == g, sink_ref[h], sink)
        for j in range(ATTN_QB):
            rows = slice(j * WINDOW, (j + 1) * WINDOW)
            keys = slice(j * WINDOW, (j + 2) * WINDOW)
            q = jnp.concatenate([cur_ref[rows, h * HEAD_DIM:(h + 1) * HEAD_DIM] for h in heads], axis=0)
            bias = bias0_ref[kv] if j == 0 else bias_ref[kv]
            st = _dot_nt(k[keys, ks], (q * (HEAD_DIM ** -0.5)).astype(BF16)) + bias
            m = jnp.maximum(sink, jnp.max(st, axis=0, keepdims=True))
            e = jnp.exp(st - m)
            inv = 1.0 / (jnp.exp(sink - m) + jnp.sum(e, axis=0, keepdims=True))
            ot = _dot(vt[ks, keys], (e * inv).astype(BF16))
            for pair in range(Q_PER_KV // 2):
                two = jnp.concatenate(
                    [ot[:, (2 * pair + g) * WINDOW:(2 * pair + g + 1) * WINDOW] for g in range(2)], axis=0)
                lo = (kv * Q_PER_KV + 2 * pair) * HEAD_DIM
                o_ref[rows, lo:lo + 2 * HEAD_DIM] = two.T.astype(BF16)

    @pl.when(pl.program_id(1) == pl.num_programs(1) - 1)
    def _():
        last = slice((ATTN_QB - 1) * WINDOW, ATTN_QB * WINDOW)
        nk_ref[...] = cur_ref[last, ATTN_W:ATTN_W + KV_W]
        nv_ref[...] = cur_ref[last, ATTN_W + KV_W:PA_W]


def _attn_prompt(pa, sink):
    nb = SEQ // WINDOW
    ns = nb // ATTN_QB
    r = np.arange(WINDOW)[:, None]
    c = np.arange(2 * WINDOW)[None, :]
    band = (c > r) & (c <= WINDOW + r)
    bias = np.stack([_alibi_bias(WINDOW + r - c, band & (c >= WINDOW)), _alibi_bias(WINDOW + r - c, band)])
    bias = jnp.asarray(bias.transpose(0, 1, 3, 2))
    bias_spec = lambda variant: pl.BlockSpec((None, N_KV_HEADS, 2 * WINDOW, Q_PER_KV * WINDOW),
                                             lambda b, i, s: (variant(i), 0, 0, 0))
    state = pl.BlockSpec((None, WINDOW, KV_W), lambda b, i, s: (b, 0, 0))
    return pl.pallas_call(
        _attn_prompt_kernel,
        grid_spec=pltpu.PrefetchScalarGridSpec(
            num_scalar_prefetch=1,
            grid=(BATCH, ns),
            in_specs=[
                pl.BlockSpec((ATTN_QB * WINDOW, PA_W), lambda b, i, s: (b * ns + i, 0)),
                pl.BlockSpec((WINDOW, 2 * KV_W),
                             lambda b, i, s: (b * nb + jnp.maximum(ATTN_QB * i - 1, 0), ATTN_W // (2 * KV_W))),
                bias_spec(lambda i: jnp.minimum(i, 1)), bias_spec(lambda i: 1),
            ],
            out_specs=[pl.BlockSpec((ATTN_QB * WINDOW, ATTN_W), lambda b, i, s: (b * ns + i, 0)), state, state],
        ),
        out_shape=[jax.ShapeDtypeStruct((NP, ATTN_W), BF16),
                   jax.ShapeDtypeStruct((BATCH, WINDOW, KV_W), F32),
                   jax.ShapeDtypeStruct((BATCH, WINDOW, KV_W), F32)],
        compiler_params=_params("arbitrary", "arbitrary"),
        name="attn_prompt",
    )(sink, pa, pa, bias, bias)


ATTN_SB = 16


def _attn_sample_kernel(sink_ref, cur_ref, kc_ref, vc_ref, bias_c_ref, bias_n_ref, o_ref, nk_ref, nv_ref):
    cur = cur_ref[...].reshape(ATTN_SB, DEC_SEQ, PA_W)
    for kv in range(N_KV_HEADS):
        ks = slice(kv * HEAD_DIM, (kv + 1) * HEAD_DIM)
        heads = range(kv * Q_PER_KV, (kv + 1) * Q_PER_KV)
        q = jnp.concatenate([cur[:, :, h * HEAD_DIM:(h + 1) * HEAD_DIM] for h in heads], axis=1)
        q = (q * (HEAD_DIM ** -0.5)).astype(BF16)
        kn = cur[:, :, ATTN_W + kv * HEAD_DIM:ATTN_W + (kv + 1) * HEAD_DIM].astype(BF16)
        vn = cur[:, :, ATTN_W + KV_W + kv * HEAD_DIM:ATTN_W + KV_W + (kv + 1) * HEAD_DIM].astype(BF16)
        sc = _bdot_nt(q, kc_ref[:, :, ks].astype(BF16)) + bias_c_ref[kv]
        sn = _bdot_nt(q, kn) + bias_n_ref[kv]
        sink = _head_column(sink_ref, kv, DEC_SEQ)
        o = _softmax_sink_pv([(sc, vc_ref[:, :, ks].astype(BF16)), (sn, vn)], sink, _bdot)
        for g, h in enumerate(heads):
            o_ref[:, h * HEAD_DIM:(h + 1) * HEAD_DIM] = (
                o[:, g * DEC_SEQ:(g + 1) * DEC_SEQ, :].reshape(ATTN_SB * DEC_SEQ, HEAD_DIM).astype(BF16))
    keep = WINDOW - DEC_SEQ
    nk_ref[:, 0:keep, :] = kc_ref[:, DEC_SEQ:WINDOW, :]
    nk_ref[:, keep:WINDOW, :] = cur[:, :, ATTN_W:ATTN_W + KV_W]
    nv_ref[:, 0:keep, :] = vc_ref[:, DEC_SEQ:WINDOW, :]
    nv_ref[:, keep:WINDOW, :] = cur[:, :, ATTN_W + KV_W:PA_W]


def _attn_sample(pa, sink, cache_k, cache_v, layer):
    rows = ATTN_SB * DEC_SEQ
    first = NP // rows
    t = np.arange(DEC_SEQ)[:, None]
    jc = np.arange(WINDOW)[None, :]
    jn = np.arange(DEC_SEQ)[None, :]
    bias_c = jnp.asarray(_alibi_bias(WINDOW + t - jc, jc > t))
    bias_n = jnp.asarray(_alibi_bias(t - jn, jn <= t))
    cache = pl.BlockSpec((None, ATTN_SB, WINDOW, KV_W), lambda j, s: (layer, j, 0, 0))
    new = pl.BlockSpec((ATTN_SB, WINDOW, KV_W), lambda j, s: (j, 0, 0))
    const = lambda a: pl.BlockSpec(a.shape, lambda j, s: (0,) * a.ndim)
    return pl.pallas_call(
        _attn_sample_kernel,
        grid_spec=pltpu.PrefetchScalarGridSpec(
            num_scalar_prefetch=1,
            grid=(DEC_BATCH // ATTN_SB,),
            in_specs=[pl.BlockSpec((rows, PA_W), lambda j, s: (first + j, 0)), cache, cache,
                      const(bias_c), const(bias_n)],
            out_specs=[pl.BlockSpec((rows, ATTN_W), lambda j, s: (j, 0)), new, new],
        ),
        out_shape=[jax.ShapeDtypeStruct((NS, ATTN_W), BF16),
                   jax.ShapeDtypeStruct((DEC_BATCH, WINDOW, KV_W), F32),
                   jax.ShapeDtypeStruct((DEC_BATCH, WINDOW, KV_W), F32)],
        compiler_params=_params("arbitrary"),
        name="attn_sample",
    )(sink, pa, cache_k, cache_v, bias_c, bias_n)


def _chunk_cumsum(x, chunk):
    pos = lax.broadcasted_iota(I32, x.shape, 0) % chunk
    sh = 1
    while sh < chunk:
        x = x + jnp.where(pos >= sh, pltpu.roll(x, sh, 0), 0.0)
        sh *= 2
    return x


def _gla_log_decay(pag_ref, wa2_ref, ba_ref):
    z = _dot(pag_ref[...].astype(BF16), wa2_ref[...]) + ba_ref[...]
    return _log_sigmoid(z) / GLA_TAU


def _gla_finish(o, rg, g):
    o = o * lax.rsqrt(jnp.mean(o * o, axis=-1, keepdims=True) + RMS_EPS)
    return (o * g * (rg * _sigmoid(rg))).astype(BF16)


GLA_TT = 2048
GLA_AG = 256


def _gla_prompt_kernel(pg_ref, pag_ref, wa2_ref, ba_ref, g_ref, o_ref, s_ref,
                       qd_ref, kd_ref, kdec_ref, a_ref, sall_ref, st_ref, acc_ref):
    nc = GLA_TT // GLA_CHUNK

    @pl.when(pl.program_id(1) == 0)
    def _():
        st_ref[...] = jnp.zeros_like(st_ref)

    cum = _chunk_cumsum(_gla_log_decay(pag_ref, wa2_ref, ba_ref), GLA_CHUNK)
    cum3 = cum.reshape(nc, GLA_CHUNK, GLA_KW)
    tot3 = cum3[:, GLA_CHUNK - 1:GLA_CHUNK, :]
    k = pg_ref[:, GLA_KW:2 * GLA_KW]
    qd_ref[...] = (pg_ref[:, 0:GLA_KW] * (GLA_DK ** -0.5) * jnp.exp(cum)).astype(BF16)
    kd_ref[...] = (k * jnp.exp(-cum)).astype(BF16)
    kdec_ref[...] = (k * jnp.exp(tot3 - cum3).reshape(GLA_TT, GLA_KW)).astype(BF16)
    etot = jnp.exp(tot3)

    def hs(h):
        return slice(h * GLA_DK, (h + 1) * GLA_DK)

    def vs(h):
        return slice(2 * GLA_KW + h * GLA_DV, 2 * GLA_KW + (h + 1) * GLA_DV)

    for c in range(nc):
        rows = slice(c * GLA_CHUNK, (c + 1) * GLA_CHUNK)
        for h in range(GLA_HEADS):
            a_ref[c, :, hs(h)] = _dot_tn(pg_ref[rows, vs(h)].astype(BF16), kdec_ref[rows, hs(h)])

    st = st_ref[...]
    for c in range(nc):
        sall_ref[c] = st.astype(BF16)
        st = etot[c] * st + a_ref[c]
    st_ref[...] = st

    r = lax.broadcasted_iota(I32, (GLA_AG, GLA_AG), 0)
    col = lax.broadcasted_iota(I32, (GLA_AG, GLA_AG), 1)
    causal = (r // GLA_CHUNK == col // GLA_CHUNK) & (col <= r)
    for h in range(GLA_HEADS):
        out = slice(h * GLA_DV, (h + 1) * GLA_DV)
        for c in range(nc):
            rows = slice(c * GLA_CHUNK, (c + 1) * GLA_CHUNK)
            acc_ref[rows, out] = _dot_nt(qd_ref[rows, hs(h)], sall_ref[c, :, hs(h)])
        for a in range(GLA_TT // GLA_AG):
            rows = slice(a * GLA_AG, (a + 1) * GLA_AG)
            att = jnp.where(causal, _dot_nt(qd_ref[rows, hs(h)], kd_ref[rows, hs(h)]), 0.0)
            acc_ref[rows, out] += _dot(att.astype(BF16), pg_ref[rows, vs(h)].astype(BF16))
        rg = pg_ref[:, 2 * GLA_KW + GLA_VW + h * GLA_DV:2 * GLA_KW + GLA_VW + (h + 1) * GLA_DV]
        o_ref[:, out] = _gla_finish(acc_ref[:, out], rg, g_ref[:, out])

    @pl.when(pl.program_id(1) == pl.num_programs(1) - 1)
    def _():
        for h in range(GLA_HEADS):
            s_ref[h] = st_ref[:, hs(h)].T


def _gla_prompt(pg, pag, wa2, ba, g):
    nt = SEQ // GLA_TT
    nc = GLA_TT // GLA_CHUNK
    const = lambda shape: pl.BlockSpec(shape, lambda b, j: (0,) * len(shape))
    return pl.pallas_call(
        _gla_prompt_kernel,
        grid=(BATCH, nt),
        in_specs=[
            pl.BlockSpec((GLA_TT, PG_W), lambda b, j: (b * nt + j, 0)),
            pl.BlockSpec((GLA_TT, LANES), lambda b, j: (b * nt + j, 0)),
            const((LANES, GLA_KW)), const((1, GLA_KW)), const((1, GLA_VW)),
        ],
        out_specs=[
            pl.BlockSpec((GLA_TT, GLA_VW), lambda b, j: (b * nt + j, 0)),
            pl.BlockSpec((None, GLA_HEADS, GLA_DK, GLA_DV), lambda b, j: (b, 0, 0, 0)),
        ],
        out_shape=[jax.ShapeDtypeStruct((NP, GLA_VW), BF16),
                   jax.ShapeDtypeStruct((BATCH, GLA_HEADS, GLA_DK, GLA_DV), F32)],
        scratch_shapes=[pltpu.VMEM((GLA_TT, GLA_KW), BF16), pltpu.VMEM((GLA_TT, GLA_KW), BF16),
                        pltpu.VMEM((GLA_TT, GLA_KW), BF16),
                        pltpu.VMEM((nc, GLA_DV, GLA_KW), F32), pltpu.VMEM((nc, GLA_DV, GLA_KW), BF16),
                        pltpu.VMEM((GLA_DV, GLA_KW), F32), pltpu.VMEM((GLA_TT, GLA_VW), F32)],
        compiler_params=_params("arbitrary", "arbitrary"),
        name="gla_prompt",
    )(pg, pag, wa2, ba, g)


GLA_SB = 8


def _gla_sample_kernel(pg_ref, pag_ref, wa2_ref, ba_ref, g_ref, s0_ref, o_ref, s_ref):
    cum_all = _chunk_cumsum(_gla_log_decay(pag_ref, wa2_ref, ba_ref), DEC_SEQ)
    tri = (lax.broadcasted_iota(I32, (DEC_SEQ, DEC_SEQ), 0) >= lax.broadcasted_iota(I32, (DEC_SEQ, DEC_SEQ), 1))
    for s in range(GLA_SB):
        rows = slice(s * DEC_SEQ, (s + 1) * DEC_SEQ)
        cum = cum_all[rows, :]
        tot = cum[DEC_SEQ - 1:DEC_SEQ, :]
        q = pg_ref[rows, 0:GLA_KW] * (GLA_DK ** -0.5)
        k = pg_ref[rows, GLA_KW:2 * GLA_KW]
        qd = (q * jnp.exp(cum)).astype(BF16)
        kd = (k * jnp.exp(-cum)).astype(BF16)
        kdec = (k * jnp.exp(tot - cum)).astype(BF16)
        etot = jnp.exp(tot)
        etot_col = [jnp.broadcast_to(etot[:, p * LANES:(p + 1) * LANES], (SUBLANES, LANES)).T[:, 0:1]
                    for p in range(GLA_KW // LANES)]
        for h in range(GLA_HEADS):
            ks = slice(h * GLA_DK, (h + 1) * GLA_DK)
            vs = slice(2 * GLA_KW + h * GLA_DV, 2 * GLA_KW + (h + 1) * GLA_DV)
            rs = slice(2 * GLA_KW + GLA_VW + h * GLA_DV, 2 * GLA_KW + GLA_VW + (h + 1) * GLA_DV)
            v = pg_ref[rows, vs].astype(BF16)
            st = s0_ref[s, h]
            att = jnp.where(tri, _dot_nt(qd[:, ks], kd[:, ks]), 0.0)
            o = _dot(qd[:, ks], st.astype(BF16)) + _dot(att.astype(BF16), v)
            per = LANES // GLA_DK
            col = etot_col[h // per][(h % per) * GLA_DK:(h % per + 1) * GLA_DK, :]
            s_ref[s, h] = col * st + _dot_tn(kdec[:, ks], v)
            o_ref[rows, h * GLA_DV:(h + 1) * GLA_DV] = _gla_finish(o, pg_ref[rows, rs], g_ref[:, h * GLA_DV:(h + 1) * GLA_DV])


def _gla_sample(pg, pag, wa2, ba, g, state, layer):
    rows = GLA_SB * DEC_SEQ
    first = NP // rows
    const = lambda shape: pl.BlockSpec(shape, lambda j: (0,) * len(shape))
    return pl.pallas_call(
        _gla_sample_kernel,
        grid=(DEC_BATCH // GLA_SB,),
        in_specs=[
            pl.BlockSpec((rows, PG_W), lambda j: (first + j, 0)),
            pl.BlockSpec((rows, LANES), lambda j: (first + j, 0)),
            const((LANES, GLA_KW)), const((1, GLA_KW)), const((1, GLA_VW)),
            pl.BlockSpec((None, GLA_SB, GLA_HEADS, GLA_DK, GLA_DV), lambda j: (layer, j, 0, 0, 0)),
        ],
        out_specs=[
            pl.BlockSpec((rows, GLA_VW), lambda j: (j, 0)),
            pl.BlockSpec((GLA_SB, GLA_HEADS, GLA_DK, GLA_DV), lambda j: (j, 0, 0, 0)),
        ],
        out_shape=[jax.ShapeDtypeStruct((NS, GLA_VW), BF16),
                   jax.ShapeDtypeStruct((DEC_BATCH, GLA_HEADS, GLA_DK, GLA_DV), F32)],
        compiler_params=_params("arbitrary"),
        name="gla_sample",
    )(pg, pag, wa2, ba, g, state)


POOL_HIST = 16


def _pool_groups(ext, cnt, pw_ref, ps_ref, out_rows):
    ax = ext.ndim - 2
    outs = []
    for g, w in enumerate(POOL_WINDOWS):
        x = ext[..., g * POOL_GW:(g + 1) * POOL_GW]
        s, sh = x, 1
        while sh < w:
            s = s + pltpu.roll(s, sh, ax)
            sh *= 2
        inv = 1.0 / cnt[g]
        if ext.ndim == 3:
            d = (s[:, POOL_HIST:, :] * inv - x[:, POOL_HIST:, :]).reshape(out_rows, POOL_GW)
        else:
            d = s[POOL_HIST:, :] * inv - x[POOL_HIST:, :]
        y = _dot(d.astype(BF16), pw_ref[g]) * ps_ref[:, g * POOL_GW:(g + 1) * POOL_GW]
        outs.append(y.astype(BF16))
    return outs


POOL_SB = 16


def _pool_sample_kernel(ext_ref, pw_ref, ps_ref, o_ref):
    cnt = [float(w) for w in POOL_WINDOWS]
    for g, y in enumerate(_pool_groups(ext_ref[...], cnt, pw_ref, ps_ref, POOL_SB * DEC_SEQ)):
        o_ref[:, g * POOL_GW:(g + 1) * POOL_GW] = y


def _pool_sample(ext, pw, ps):
    rows = POOL_SB * DEC_SEQ
    return pl.pallas_call(
        _pool_sample_kernel,
        grid=(DEC_BATCH // POOL_SB,),
        in_specs=[
            pl.BlockSpec((POOL_SB, POOL_HIST + DEC_SEQ, POOL_W), lambda j: (j, 0, 0)),
            pl.BlockSpec((POOL_GROUPS, POOL_GW, POOL_GW), lambda j: (0, 0, 0)),
            pl.BlockSpec((1, POOL_W), lambda j: (0, 0)),
        ],
        out_specs=pl.BlockSpec((rows, POOL_W), lambda j: (j, 0)),
        out_shape=jax.ShapeDtypeStruct((NS, POOL_W), BF16),
        compiler_params=_params("arbitrary"),
        name="pool_sample",
    )(ext, pw, ps)


def _route(sc, sel):
    gscore = []
    for g in range(N_GROUPS):
        v = sel[EXPERTS_PER_GROUP * g:EXPERTS_PER_GROUP * (g + 1)]
        best = None
        for a, b in PAIRS:
            pair = v[a] + v[b]
            best = pair if best is None else jnp.maximum(best, pair)
        gscore.append(best)
    gi = jnp.zeros_like(gscore[0], dtype=I32)
    best = gscore[0]
    for g in range(1, N_GROUPS):
        upd = gscore[g] > best
        gi = jnp.where(upd, g, gi)
        best = jnp.where(upd, gscore[g], best)

    def in_group(rows, j):
        out = rows[(N_GROUPS - 1) * EXPERTS_PER_GROUP + j]
        for g in range(N_GROUPS - 2, -1, -1):
            out = jnp.where(gi == g, rows[g * EXPERTS_PER_GROUP + j], out)
        return out

    u = [in_group(sel, j) for j in range(EXPERTS_PER_GROUP)]
    s_in = [in_group(sc, j) for j in range(EXPERTS_PER_GROUP)]

    def argmax4(vals):
        idx = jnp.zeros_like(gi)
        m = vals[0]
        for j in range(1, EXPERTS_PER_GROUP):
            upd = vals[j] > m
            idx = jnp.where(upd, j, idx)
            m = jnp.where(upd, vals[j], m)
        return idx

    def pick(vals, idx):
        out = vals[EXPERTS_PER_GROUP - 1]
        for j in range(EXPERTS_PER_GROUP - 2, -1, -1):
            out = jnp.where(idx == j, vals[j], out)
        return out

    i1 = argmax4(u)
    i2 = argmax4([jnp.where(i1 == j, NEG_INF, u[j]) for j in range(EXPERTS_PER_GROUP)])
    w1, w2 = pick(s_in, i1), pick(s_in, i2)
    tot = w1 + w2
    w1, w2 = w1 / tot, w2 / tot
    lo, hi = jnp.minimum(i1, i2), jnp.maximum(i1, i2)
    first_lo = i1 < i2
    w_lo, w_hi = jnp.where(first_lo, w1, w2), jnp.where(first_lo, w2, w1)
    pair = jnp.where(lo == 0, hi - 1, jnp.where(hi == EXPERTS_PER_GROUP - 1, lo + 2, len(PAIRS) - 1))
    swapped = pair == len(PAIRS) - 1
    return gi * len(PAIRS) + pair, jnp.where(swapped, w_hi, w_lo), jnp.where(swapped, w_lo, w_hi)


def _merge_kernel(*refs):
    (oap_ref, oas_ref, ogp_ref, ogs_ref, opp_ref, ops_ref, wg_ref, wb_ref, wo_ref,
     g1_ref, b1_ref, wr_ref, rb_ref, xe_ref, meta_ref) = refs[-15:]
    x = refs[0][...] if len(refs) == 16 else _pair_tile(refs[0], refs[1])
    xb = x.astype(BF16)
    merged = None
    for n, (brp, brs) in enumerate(((oap_ref, oas_ref), (ogp_ref, ogs_ref), (opp_ref, ops_ref))):
        gate = _sigmoid(_dot(xb, wg_ref[:, n * D_MODEL:(n + 1) * D_MODEL]))
        term = gate * _dot(_pair_tile(brp, brs), wb_ref[n])
        merged = term if merged is None else merged + term
    mix = _dot(merged.astype(BF16), wo_ref[...])
    x1 = _layer_norm(DN_ALPHA * x + mix, g1_ref[...], b1_ref[...])
    xe_ref[:, 0:D_MODEL] = x1
    sc_t = _sigmoid(_dot_nt(wr_ref[...], x1.astype(BF16)))
    sel_t = sc_t + rb_ref[...]
    sc = [sc_t[e:e + 1, :] for e in range(N_EXPERTS)]
    sel = [sel_t[e:e + 1, :] for e in range(N_EXPERTS)]
    cls, w_first, w_second = _route(sc, sel)
    cls = cls.astype(F32)

    def rows(n):
        rid = lax.broadcasted_iota(I32, (n, TM), 0)
        return jnp.where(rid == 0, cls, jnp.where(rid == 1, w_first, jnp.where(rid == 2, w_second, 0.0)))

    meta_ref[...] = rows(SUBLANES)
    xe_ref[:, D_MODEL:XE_W] = rows(LANES).T


def _merge(x, branches, wg, wb, wo, g1, b1, wr_t, rb, layer):
    row = lambda w: pl.BlockSpec((TM, w), lambda i: (i, 0))
    once = pl.Buffered(1)
    lay = lambda *shape: pl.BlockSpec((None,) + shape, lambda i: (layer,) + (0,) * len(shape), pipeline_mode=once)
    const = lambda *shape: pl.BlockSpec(shape, lambda i: (0,) * len(shape), pipeline_mode=once)
    return pl.pallas_call(
        _merge_kernel,
        grid=(N_TILES,),
        in_specs=[*(_pair_specs(D_MODEL) if isinstance(x, tuple) else (row(D_MODEL),)),
                  *(_pair_specs(BRANCH_W) * N_BRANCH),
                  lay(D_MODEL, N_BRANCH * D_MODEL), lay(N_BRANCH, BRANCH_W, D_MODEL), lay(D_MODEL, D_MODEL),
                  lay(1, D_MODEL), lay(1, D_MODEL), const(N_EXPERTS, D_MODEL), const(N_EXPERTS, 1)],
        out_specs=[row(XE_W), pl.BlockSpec((None, SUBLANES, TM), lambda i: (i, 0, 0))],
        out_shape=[jax.ShapeDtypeStruct((N_TOK, XE_W), F32), jax.ShapeDtypeStruct((N_TILES, SUBLANES, TM), F32)],
        compiler_params=_params("arbitrary"),
        name="merge",
    )(*(x if isinstance(x, tuple) else (x,)), *branches, wg, wb, wo, g1, b1, wr_t, rb)


def _plan_kernel(cls_ref, pos_ref, tcls_ref, nused_ref):
    cls = cls_ref[...]
    lane_r = lax.broadcasted_iota(I32, (LANES, 2 * LANES), 0)
    lane_c = lax.broadcasted_iota(I32, (LANES, 2 * LANES), 1)
    lane_mat = ((lane_c >= LANES) | (lane_r < lane_c)).astype(BF16)
    row_r = lax.broadcasted_iota(I32, (2 * PLAN_ROWS, PLAN_ROWS), 0)
    row_c = lax.broadcasted_iota(I32, (2 * PLAN_ROWS, PLAN_ROWS), 1)
    row_mat = ((row_r >= PLAN_ROWS) | (row_c < row_r)).astype(BF16)
    tile_start = (lax.broadcasted_iota(I32, (SUBLANES, LANES), 1) * TM_E).astype(F32)
    pos = jnp.zeros((PLAN_ROWS, LANES), F32)
    off = jnp.zeros((PLAN_ROWS, LANES), F32)
    tcls = jnp.zeros((SUBLANES, LANES), I32)
    for c in range(N_CLASSES):
        m = cls == c
        lanes = _dot(m.astype(BF16), lane_mat)
        rows = _dot(row_mat, lanes[:, LANES:].astype(BF16))
        rank = lanes[:, 0:LANES] + rows[0:PLAN_ROWS]
        count = rows[PLAN_ROWS:]
        pos = jnp.where(m, off + rank, pos)
        off = off + jnp.ceil(count * (1.0 / TM_E)) * TM_E
        tcls = tcls + (off[0:SUBLANES] <= tile_start).astype(I32)
    pos_ref[...] = pos.astype(I32)
    tcls_ref[...] = tcls
    nused_ref[...] = (off[0:SUBLANES] * (1.0 / TM_E)).astype(I32)


def _plan(cls2d):
    return pl.pallas_call(
        _plan_kernel,
        out_shape=[jax.ShapeDtypeStruct((PLAN_ROWS, LANES), I32),
                   jax.ShapeDtypeStruct((SUBLANES, LANES), I32),
                   jax.ShapeDtypeStruct((SUBLANES, LANES), I32)],
        compiler_params=pltpu.CompilerParams(vmem_limit_bytes=VMEM_LIMIT),
        name="plan",
    )(cls2d)


def _row_copies(pos_ref, tile_ref, sorted_hbm, sem, scatter):
    def copy(r):
        row, srt = tile_ref.at[pl.ds(r, 1)], sorted_hbm.at[pl.ds(pos_ref[0, r], 1)]
        return pltpu.make_async_copy(row, srt, sem) if scatter else pltpu.make_async_copy(srt, row, sem)

    def wait(r, carry):
        copy(r).wait()
        return carry

    rows = tile_ref.shape[0]
    for r in range(rows):
        copy(r).start(priority=r % DMA_THREADS)
    lax.fori_loop(0, rows, wait, 0, unroll=8)


def _dispatch_kernel(pos_ref, x_ref, xs_in_hbm, xs_hbm, sem):
    del xs_in_hbm
    _row_copies(pos_ref, x_ref, xs_hbm, sem, scatter=True)


def _combine_kernel(pos_ref, ys_hbm, xp_ref, xs_ref, sem):
    @pl.when(pl.program_id(0) < PT)
    def _():
        _row_copies(pos_ref, xp_ref, ys_hbm, sem, scatter=False)

    @pl.when(pl.program_id(0) >= PT)
    def _():
        _row_copies(pos_ref, xs_ref, ys_hbm, sem, scatter=False)


_POS_SPEC = pl.BlockSpec((None, 1, TM), lambda i: (i, 0, 0), memory_space=pltpu.SMEM)
_ANY = pl.BlockSpec(memory_space=pl.ANY)


TM_D = 1024


def _dispatch(pos, xe, xs_prev):
    pos = pos.reshape(N_TOK // TM_D, 1, TM_D)
    return pl.pallas_call(
        _dispatch_kernel,
        grid=(N_TOK // TM_D,),
        in_specs=[pl.BlockSpec((None, 1, TM_D), lambda i: (i, 0, 0), memory_space=pltpu.SMEM),
                  pl.BlockSpec((TM_D, XE_W), lambda i: (i, 0)), _ANY],
        out_specs=_ANY,
        out_shape=jax.ShapeDtypeStruct((N_SORTED, XE_W), F32),
        scratch_shapes=[pltpu.SemaphoreType.DMA(())],
        input_output_aliases={2: 0},
        compiler_params=_params("arbitrary"),
        name="dispatch",
    )(pos, xe, xs_prev)


def _combine(pos, ys):
    return pl.pallas_call(
        _combine_kernel,
        grid=(N_TILES,),
        in_specs=[_POS_SPEC, _ANY],
        out_specs=list(_pair_specs(D_MODEL)),
        out_shape=[jax.ShapeDtypeStruct((NP, D_MODEL), F32), jax.ShapeDtypeStruct((NS, D_MODEL), F32)],
        scratch_shapes=[pltpu.SemaphoreType.DMA(())],
        compiler_params=_params("arbitrary"),
        name="combine",
    )(pos, ys)


def _experts_kernel(ea_ref, eb_ref, new_a_ref, new_b_ref, nused_ref, xs_ref,
                    wga_ref, wua_ref, wda_ref, wgb_ref, wub_ref, wdb_ref, g2_ref, b2_ref, ys_ref,
                    wga_s, wua_s, wda_s, wgb_s, wub_s, wdb_s, pre_ref):
    del ea_ref, eb_ref
    i = pl.program_id(0)
    nused = nused_ref[0]

    @pl.when(new_a_ref[i] == 1)
    def _():
        for src, dst in ((wga_ref, wga_s), (wua_ref, wua_s), (wda_ref, wda_s)):
            dst[...] = src[...].astype(BF16)

    @pl.when(new_b_ref[i] == 1)
    def _():
        for src, dst in ((wgb_ref, wgb_s), (wub_ref, wub_s), (wdb_ref, wdb_s)):
            dst[...] = src[...].astype(BF16)

    def mlp():
        x1 = xs_ref[:, 0:D_MODEL]
        xb = x1.astype(BF16)

        def expert(wg, wu, wd):
            a = _dot(xb, wg[...])
            h = a * _sigmoid(a) * _dot(xb, wu[...])
            return _dot(h.astype(BF16), wd[...])

        ffn = xs_ref[:, D_MODEL + 1:D_MODEL + 2] * expert(wga_s, wua_s, wda_s)
        ffn = ffn + xs_ref[:, D_MODEL + 2:D_MODEL + 3] * expert(wgb_s, wub_s, wdb_s)
        pre_ref[i % 2] = DN_ALPHA * x1 + ffn

    def norm():
        ys_ref[...] = _layer_norm(pre_ref[(i + 1) % 2], g2_ref[...], b2_ref[...])

    @pl.when(i == 0)
    def _():
        mlp()

    @pl.when((i > 0) & (i < nused))
    def _():
        norm()
        mlp()

    @pl.when(i == nused)
    def _():
        norm()

    @pl.when(i > nused)
    def _():
        ys_ref[...] = jnp.zeros_like(ys_ref)


def _experts(tile_a, tile_b, new_a, new_b, nused, xs, w_gate, w_up, w_down, g2, b2, layer):
    tile = lambda w: pl.BlockSpec((TM_E, w), lambda i, ea, eb, na, nb, nu: (jnp.minimum(i, nu[0] - 1), 0))
    wa = lambda *shape: pl.BlockSpec((None, None) + shape, lambda i, ea, eb, na, nb, nu: (layer, ea[i], 0, 0))
    wb = lambda *shape: pl.BlockSpec((None, None) + shape, lambda i, ea, eb, na, nb, nu: (layer, eb[i], 0, 0))
    lay = pl.BlockSpec((None, 1, D_MODEL), lambda i, ea, eb, na, nb, nu: (layer, 0, 0))
    up, down = pltpu.VMEM((D_MODEL, D_EXPERT), BF16), pltpu.VMEM((D_EXPERT, D_MODEL), BF16)
    return pl.pallas_call(
        _experts_kernel,
        grid_spec=pltpu.PrefetchScalarGridSpec(
            num_scalar_prefetch=5,
            grid=(NT_E + 1,),
            in_specs=[tile(XE_W),
                      wa(D_MODEL, D_EXPERT), wa(D_MODEL, D_EXPERT), wa(D_EXPERT, D_MODEL),
                      wb(D_MODEL, D_EXPERT), wb(D_MODEL, D_EXPERT), wb(D_EXPERT, D_MODEL),
                      lay, lay],
            out_specs=pl.BlockSpec((TM_E, D_MODEL), lambda i, ea, eb, na, nb, nu: (jnp.maximum(i - 1, 0), 0)),
            scratch_shapes=[up, up, down, up, up, down, pltpu.VMEM((2, TM_E, D_MODEL), F32)],
        ),
        out_shape=jax.ShapeDtypeStruct((N_SORTED, D_MODEL), F32),
        compiler_params=_params("arbitrary"),
        name="experts",
    )(tile_a, tile_b, new_a, new_b, nused, xs, w_gate, w_up, w_down, w_gate, w_up, w_down, g2, b2)


_PAIR_LO = tuple(a for a, _ in PAIRS)
_PAIR_HI = tuple(b for _, b in PAIRS)


def kernel(x_prompt, x_sample, cache_attn_k, cache_attn_v, state_gla, state_pool, w_in, w_gate, attn_sink,
           gla_w_a2, gla_b_a, gla_norm_g, pool_w, pool_scale, w_branch, w_o, ln1_g, ln1_b, ln2_g, ln2_b,
           w_router, router_bias, w_e_gate, w_e_up, w_e_down):
    ag0 = PA_W + PG_W
    w_in_p = (w_in[:, :, :ag0].astype(BF16), w_in[:, :, ag0 + GLA_RANK:].astype(BF16),
              jnp.pad(w_in[:, :, ag0:ag0 + GLA_RANK], ((0, 0), (0, 0), (0, LANES - GLA_RANK))).astype(BF16))
    w_gate_b, w_branch_b, w_o_b = w_gate.astype(BF16), w_branch.astype(BF16), w_o.astype(BF16)
    wa2_p = jnp.pad(gla_w_a2, ((0, 0), (0, LANES - GLA_RANK), (0, 0))).astype(BF16)
    pool_w_b = pool_w.astype(BF16)
    pool_scale_r = pool_scale.reshape(DEPTH, 1, POOL_W)
    wr_t = w_router.T.astype(BF16)
    rb = router_bias.reshape(N_EXPERTS, 1).astype(F32)
    cache_k = cache_attn_k.reshape(DEPTH, DEC_BATCH, WINDOW, KV_W)
    cache_v = cache_attn_v.reshape(DEPTH, DEC_BATCH, WINDOW, KV_W)
    pair_lo, pair_hi = jnp.array(_PAIR_LO, I32), jnp.array(_PAIR_HI, I32)

    x = (x_prompt.reshape(NP, D_MODEL), x_sample.reshape(NS, D_MODEL))
    xs = jnp.zeros((N_SORTED, XE_W), F32)
    ys = pos = None
    pk, pv, ps, pp, sk, sv, ss, sp = ([] for _ in range(8))
    for l in range(DEPTH):
        if l == 0:
            pa, pg, pu, pag, op_p, u_last = _in_proj(x, w_in_p, pool_w_b, pool_scale_r, l)
        else:
            pa, pg, pu, pag, op_p, u_last, x = _in_proj(ys, w_in_p, pool_w_b, pool_scale_r, l, pos)
        row2 = lambda a: a[l].reshape(1, -1)

        oa_p, k_p, v_p = _attn_prompt(pa, attn_sink[l])
        oa_s, nk, nv = _attn_sample(pa, attn_sink[l], cache_k, cache_v, l)
        og_p, s_p = _gla_prompt(pg, pag, wa2_p[l], row2(gla_b_a), row2(gla_norm_g))
        og_s, s_s = _gla_sample(pg, pag, wa2_p[l], row2(gla_b_a), row2(gla_norm_g), state_gla, l)
        u_s = pu[NP:].reshape(DEC_BATCH, DEC_SEQ, POOL_W)
        ext_s = jnp.concatenate(
            [jnp.zeros((DEC_BATCH, POOL_HIST - POOL_STATE, POOL_W), F32), state_pool[l], u_s], axis=1)
        op_s = _pool_sample(ext_s, pool_w_b[l], row2(pool_scale))

        xe, meta = _merge(x, (oa_p, oa_s, og_p, og_s, op_p, op_s), w_gate_b, w_branch_b, w_o_b,
                          ln1_g.reshape(DEPTH, 1, D_MODEL), ln1_b.reshape(DEPTH, 1, D_MODEL), wr_t, rb, l)
        cls = meta[:, 0, :].reshape(N_TOK // LANES, LANES).astype(I32)
        cls2d = jnp.pad(cls, ((0, PLAN_ROWS - N_TOK // LANES), (0, 0)), constant_values=-1)
        pos2d, tcls, nused = _plan(cls2d)
        pos = pos2d.reshape(-1)[:N_TOK].reshape(N_TILES, 1, TM)
        n_used = nused[0, :1]
        tile_cls = tcls[0, :NT_E + 1]
        tile_cls = jnp.where(jnp.arange(NT_E + 1) < n_used, tile_cls, tile_cls[n_used[0] - 1])
        tile_a = (tile_cls // len(PAIRS)) * EXPERTS_PER_GROUP + pair_lo[tile_cls % len(PAIRS)]
        tile_b = (tile_cls // len(PAIRS)) * EXPERTS_PER_GROUP + pair_hi[tile_cls % len(PAIRS)]
        first = jnp.ones((1,), I32)
        new_a = jnp.concatenate([first, (tile_a[1:] != tile_a[:-1]).astype(I32)])
        new_b = jnp.concatenate([first, (tile_b[1:] != tile_b[:-1]).astype(I32)])
        xs = _dispatch(pos, xe, xs)
        ys = _experts(tile_a, tile_b, new_a, new_b, n_used, xs, w_e_gate, w_e_up, w_e_down,
                      ln2_g.reshape(DEPTH, 1, D_MODEL), ln2_b.reshape(DEPTH, 1, D_MODEL), l)

        pk.append(k_p.reshape(BATCH, WINDOW, N_KV_HEADS, HEAD_DIM))
        pv.append(v_p.reshape(BATCH, WINDOW, N_KV_HEADS, HEAD_DIM))
        ps.append(s_p)
        pp.append(u_last[:, POOL_HIST - POOL_STATE:, :])
        sk.append(nk.reshape(DEC_BATCH, WINDOW, N_KV_HEADS, HEAD_DIM))
        sv.append(nv.reshape(DEC_BATCH, WINDOW, N_KV_HEADS, HEAD_DIM))
        ss.append(s_s)
        sp.append(ext_s[:, POOL_HIST + DEC_SEQ - POOL_STATE:, :])
    y_prompt, y_sample = _combine(pos, ys)
    return (y_prompt.reshape(BATCH, SEQ, D_MODEL), y_sample.reshape(DEC_BATCH, DEC_SEQ, D_MODEL),
            jnp.stack(pk), jnp.stack(pv), jnp.stack(ps), jnp.stack(pp),
            jnp.stack(sk), jnp.stack(sv), jnp.stack(ss), jnp.stack(sp))
```
